```python
import jax, jax.numpy as jnp
from jax import lax
import numpy as np

D_MODEL = 2048
BATCH = 2
SEQ = 8192
DEPTH = 4

N_MIXERS = 2
GRID_W = 64
NORM_EPS = 1e-6
MLSTM_HEADS = 4
MLSTM_DK = 256
MLSTM_DV = D_MODEL // MLSTM_HEADS
MLSTM_QK_WIDTH = MLSTM_HEADS * MLSTM_DK
MLSTM_IN_DIM = 2 * MLSTM_QK_WIDTH + 2 * D_MODEL + 4 * MLSTM_HEADS
MLSTM_CHUNK = 64
GATE_SOFTCAP = 15.0
FORGET_BIAS = 3.0
NA_HEADS = 16
NA_HEAD_DIM = D_MODEL // NA_HEADS
NA_KH = 8
NA_KW = 16
FFN_DIM = 7 * D_MODEL // 2
N_EXPERTS = 8
TOP_K = 2

kernel_name = "bidir_mlstm_natten_moe_hybrid"


def rms_norm(x, gain):
    xf = x.astype(jnp.float32)
    y = xf * lax.rsqrt(jnp.mean(xf * xf, axis=-1, keepdims=True) + NORM_EPS)
    return (y * gain.astype(jnp.float32)).astype(x.dtype)


def _mlstm_chunkwise(q, k, v, log_i, log_f):
    B, H, S, dk = q.shape
    dv = v.shape[-1]
    L = MLSTM_CHUNK
    NC = S // L

    def to_chunks(a):
        return jnp.moveaxis(a.reshape((B, H, NC, L) + a.shape[3:]), 2, 0)

    xs = (to_chunks(q), to_chunks(k), to_chunks(v), to_chunks(log_i), to_chunks(log_f))
    lower = jnp.tril(jnp.ones((L, L), dtype=bool))

    def step(carry, inp):
        C, n, m = carry
        qj, kj, vj, li, lf = inp
        b = jnp.cumsum(lf, axis=-1)
        dmat = jnp.where(lower, b[..., :, None] - b[..., None, :] + li[..., None, :], -jnp.inf)
        inter = b + m[..., None]
        mj = jnp.maximum(inter, jnp.max(dmat, axis=-1))
        w_inter = jnp.exp(inter - mj)
        w_intra = jnp.exp(dmat - mj[..., None]) * jnp.einsum('bhld,bhsd->bhls', qj, kj)
        num = w_inter[..., None] * jnp.einsum('bhld,bhde->bhle', qj, C) + jnp.einsum('bhls,bhse->bhle', w_intra, vj)
        den = w_inter * jnp.einsum('bhld,bhd->bhl', qj, n) + jnp.sum(w_intra, axis=-1)
        h = num / jnp.maximum(jnp.abs(den), jnp.exp(-mj))[..., None]
        g = b[..., -1]
        ls = g[..., None] - b + li
        m_new = jnp.maximum(g + m, jnp.max(ls, axis=-1))
        decay = jnp.exp(g + m - m_new)
        ks = kj * jnp.exp(ls - m_new[..., None])[..., None]
        C_new = decay[..., None, None] * C + jnp.einsum('bhsd,bhse->bhde', ks, vj)
        n_new = decay[..., None] * n + jnp.sum(ks, axis=2)
        return (C_new, n_new, m_new), h

    init = (jnp.zeros((B, H, dk, dv), jnp.float32), jnp.zeros((B, H, dk), jnp.float32),
            jnp.full((B, H), -jnp.inf, jnp.float32))
    _, hs = lax.scan(step, init, xs)
    return jnp.moveaxis(hs, 0, 2).reshape(B, H, S, dv)


def mlstm_mixer(h, w_in, b_gates, head_norm, w_out):
    B, S, D = h.shape
    proj = h @ w_in
    q, k, v, o, g = jnp.split(proj, [MLSTM_QK_WIDTH, 2 * MLSTM_QK_WIDTH, 2 * MLSTM_QK_WIDTH + D, 2 * MLSTM_QK_WIDTH + 2 * D], axis=-1)
    q = q.reshape(B, S, MLSTM_HEADS, MLSTM_DK).transpose(0, 2, 1, 3).astype(jnp.float32)
    k = k.reshape(B, S, MLSTM_HEADS, MLSTM_DK).transpose(0, 2, 1, 3).astype(jnp.float32) * (MLSTM_DK ** -0.5)
    v = v.reshape(B, S, MLSTM_HEADS, MLSTM_DV).transpose(0, 2, 1, 3).astype(jnp.float32)
    g = g.astype(jnp.float32) + b_gates.astype(jnp.float32)
    g = GATE_SOFTCAP * jnp.tanh(g / GATE_SOFTCAP)
    g = g.reshape(B, S, 4, MLSTM_HEADS).transpose(2, 0, 3, 1)
    h_fwd = _mlstm_chunkwise(q, k, v, g[0], jax.nn.log_sigmoid(g[1]))
    flip = lambda a: jnp.flip(a, axis=2)
    h_bwd = flip(_mlstm_chunkwise(flip(q), flip(k), flip(v), flip(g[2]), flip(jax.nn.log_sigmoid(g[3]))))
    hs = rms_norm(h_fwd + h_bwd, head_norm.reshape(MLSTM_HEADS, 1, MLSTM_DV))
    hs = hs.transpose(0, 2, 1, 3).reshape(B, S, D)
    return (hs * jax.nn.sigmoid(o.astype(jnp.float32))).astype(h.dtype) @ w_out


def na_mixer(h, w_qkv, q_norm, k_norm, rpb, w_out):
    B, S, D = h.shape
    rows = S // GRID_W
    kh = min(NA_KH, rows)
    kw = NA_KW
    qkv = (h @ w_qkv).reshape(B, rows, GRID_W, 3, NA_HEADS, NA_HEAD_DIM)
    q = rms_norm(qkv[:, :, :, 0], q_norm) * (NA_HEAD_DIM ** -0.5)
    k = rms_norm(qkv[:, :, :, 1], k_norm)
    v = qkv[:, :, :, 2]
    cols = jnp.arange(GRID_W)
    col_idx = jnp.clip(cols - kw // 2, 0, GRID_W - kw)[:, None] + jnp.arange(kw)[None, :]
    dc = col_idx - cols[:, None] + (NA_KW - 1)

    def row_block(r):
        rs = jnp.clip(r - kh // 2, 0, rows - kh)
        q_r = lax.dynamic_index_in_dim(q, r, axis=1, keepdims=False)
        k_r = lax.dynamic_slice_in_dim(k, rs, kh, axis=1)[:, :, col_idx]
        v_r = lax.dynamic_slice_in_dim(v, rs, kh, axis=1)[:, :, col_idx]
        dr = rs + jnp.arange(kh) - r + (NA_KH - 1)
        bias = rpb[:, dr[:, None, None], dc[None, :, :]]
        s = jnp.einsum('bchd,bacwhd->bhcaw', q_r, k_r).astype(jnp.float32) + jnp.transpose(bias, (0, 2, 1, 3)).astype(jnp.float32)
        p = jax.nn.softmax(s.reshape(B, NA_HEADS, GRID_W, kh * kw), axis=-1).reshape(s.shape).astype(v.dtype)
        return jnp.einsum('bhcaw,bacwhd->bchd', p, v_r)

    o = lax.map(row_block, jnp.arange(rows))
    o = jnp.moveaxis(o, 0, 1).reshape(B, S, D)
    return o @ w_out


def swiglu(h, w_gate, w_up, w_down):
    return (jax.nn.silu(h @ w_gate) * (h @ w_up)) @ w_down


def moe_swiglu(h, w_router, w_gate, w_up, w_down):
    B, S, D = h.shape
    t = h.reshape(B * S, D)
    logits = (t @ w_router).astype(jnp.float32)
    top_vals, top_idx = lax.top_k(logits, TOP_K)
    top_w = jax.nn.softmax(top_vals, axis=-1)
    combine = jnp.sum(jax.nn.one_hot(top_idx, N_EXPERTS, dtype=jnp.float32) * top_w[..., None], axis=1)
    y = jnp.zeros((B * S, D), jnp.float32)
    for e in range(N_EXPERTS):
        y = y + combine[:, e:e + 1] * swiglu(t, w_gate[e], w_up[e], w_down[e]).astype(jnp.float32)
    return y.astype(h.dtype).reshape(B, S, D)


def setup_inputs(seed: int = 0) -> dict:
    key = jax.random.key(seed)
    keys = iter(jax.random.split(key, 16 * DEPTH + 1))
    D = D_MODEL
    res_scale = (2 * DEPTH) ** -0.5

    def nrm(shape, scale):
        return scale * jax.random.normal(next(keys), shape, jnp.float32)

    def gain(n):
        return 1.0 + 0.02 * jax.random.normal(next(keys), (n,), jnp.float32)

    p = {"x": jax.random.normal(next(keys), (BATCH, SEQ, D), jnp.float32)}
    for i in range(DEPTH):
        pre = "l%d_" % i
        p[pre + "mix_norm"] = gain(D)
        if i % N_MIXERS == 0:
            p[pre + "mlstm_w_in"] = nrm((D, MLSTM_IN_DIM), D ** -0.5)
            ib = nrm((2, MLSTM_HEADS), 0.1)
            fb = FORGET_BIAS + nrm((2, MLSTM_HEADS), 0.5)
            p[pre + "mlstm_b_gates"] = jnp.stack([ib[0], fb[0], ib[1], fb[1]]).reshape(-1)
            p[pre + "mlstm_head_norm"] = gain(D)
            p[pre + "mlstm_w_out"] = nrm((D, D), D ** -0.5 * res_scale)
        else:
            p[pre + "na_w_qkv"] = nrm((D, 3 * D), D ** -0.5)
            p[pre + "na_q_norm"] = gain(NA_HEAD_DIM)
            p[pre + "na_k_norm"] = gain(NA_HEAD_DIM)
            p[pre + "na_rpb"] = nrm((NA_HEADS, 2 * NA_KH - 1, 2 * NA_KW - 1), 0.2)
            p[pre + "na_w_out"] = nrm((D, D), D ** -0.5 * res_scale)
        p[pre + "ffn_norm"] = gain(D)
        if i % 2 == 0:
            p[pre + "ffn_w_gate"] = nrm((D, FFN_DIM), D ** -0.5)
            p[pre + "ffn_w_up"] = nrm((D, FFN_DIM), D ** -0.5)
            p[pre + "ffn_w_down"] = nrm((FFN_DIM, D), FFN_DIM ** -0.5 * res_scale)
        else:
            p[pre + "moe_w_router"] = nrm((D, N_EXPERTS), D ** -0.5)
            p[pre + "moe_w_gate"] = nrm((N_EXPERTS, D, FFN_DIM), D ** -0.5)
            p[pre + "moe_w_up"] = nrm((N_EXPERTS, D, FFN_DIM), D ** -0.5)
            p[pre + "moe_w_down"] = nrm((N_EXPERTS, FFN_DIM, D), FFN_DIM ** -0.5 * res_scale)
    return p


def reference(x,
              l0_mix_norm, l0_mlstm_w_in, l0_mlstm_b_gates, l0_mlstm_head_norm, l0_mlstm_w_out,
              l0_ffn_norm, l0_ffn_w_gate, l0_ffn_w_up, l0_ffn_w_down,
              l1_mix_norm, l1_na_w_qkv, l1_na_q_norm, l1_na_k_norm, l1_na_rpb, l1_na_w_out,
              l1_ffn_norm, l1_moe_w_router, l1_moe_w_gate, l1_moe_w_up, l1_moe_w_down,
              l2_mix_norm, l2_mlstm_w_in, l2_mlstm_b_gates, l2_mlstm_head_norm, l2_mlstm_w_out,
              l2_ffn_norm, l2_ffn_w_gate, l2_ffn_w_up, l2_ffn_w_down,
              l3_mix_norm, l3_na_w_qkv, l3_na_q_norm, l3_na_k_norm, l3_na_rpb, l3_na_w_out,
              l3_ffn_norm, l3_moe_w_router, l3_moe_w_gate, l3_moe_w_up, l3_moe_w_down):
    mix_norms = [l0_mix_norm, l1_mix_norm, l2_mix_norm, l3_mix_norm]
    mixer_params = [
        (l0_mlstm_w_in, l0_mlstm_b_gates, l0_mlstm_head_norm, l0_mlstm_w_out),
        (l1_na_w_qkv, l1_na_q_norm, l1_na_k_norm, l1_na_rpb, l1_na_w_out),
        (l2_mlstm_w_in, l2_mlstm_b_gates, l2_mlstm_head_norm, l2_mlstm_w_out),
        (l3_na_w_qkv, l3_na_q_norm, l3_na_k_norm, l3_na_rpb, l3_na_w_out),
    ]
    ffn_norms = [l0_ffn_norm, l1_ffn_norm, l2_ffn_norm, l3_ffn_norm]
    ffn_params = [
        (l0_ffn_w_gate, l0_ffn_w_up, l0_ffn_w_down),
        (l1_moe_w_router, l1_moe_w_gate, l1_moe_w_up, l1_moe_w_down),
        (l2_ffn_w_gate, l2_ffn_w_up, l2_ffn_w_down),
        (l3_moe_w_router, l3_moe_w_gate, l3_moe_w_up, l3_moe_w_down),
    ]
    mixers = (mlstm_mixer, na_mixer)
    channel_mixers = (swiglu, moe_swiglu)
    for i in range(DEPTH):
        x = x + mixers[i % N_MIXERS](rms_norm(x, mix_norms[i]), *mixer_params[i])
        x = x + channel_mixers[i % 2](rms_norm(x, ffn_norms[i]), *ffn_params[i])
    return x
```

```python
import functools

import jax
import jax.numpy as jnp
import numpy as np
from jax import lax
from jax.experimental import pallas as pl
from jax.experimental.pallas import tpu as pltpu

F32 = jnp.float32
BF16 = jnp.bfloat16
U32 = jnp.uint32

NORM_EPS = 1e-6
LANES = 128
VMEM_LIMIT_BYTES = 56 << 20
NEG_BIG = -1e30

MLSTM_HEADS = 4
MLSTM_DK = 256
GATE_SOFTCAP = 15.0
MLSTM_CHUNK = 256
GRID_W = 64
NA_HEADS = 16
NA_HEAD_DIM = 128
NA_KH = 8
NA_KW = 16
NA_ROWS_PER_BLOCK = 4
NA_HEADS_PER_STEP = 4
N_EXPERTS = 8
TOP_K = 2

MM_TM, MM_TN = 512, 1024
FFN_TM, FFN_TF = 512, 512
MOE_TM = 512
ROUTER_TT = 512
SCATTER_TT = 1024
COMBINE_TT = 256
NORM_ROWS = 128


def _params(*semantics):
    return pltpu.CompilerParams(dimension_semantics=semantics, vmem_limit_bytes=VMEM_LIMIT_BYTES)


def _dot(a, b):
    return jnp.dot(a, b, preferred_element_type=F32)


def _dot_nt(a, b):
    return lax.dot_general(a, b, (((1,), (1,)), ((), ())), preferred_element_type=F32)


def _split3(x):
    hi = x.astype(BF16)
    r1 = x - hi.astype(F32)
    mid = r1.astype(BF16)
    lo = (r1 - mid.astype(F32)).astype(BF16)
    return hi, mid, lo


def _rms_rows(x, gain):
    ms = jnp.mean(x * x, axis=-1, keepdims=True)
    return x * lax.rsqrt(ms + NORM_EPS) * gain


def _fill_rmsnorm(x_ref, g_ref, a_ref):
    def body(c, carry):
        r = pl.multiple_of(c * NORM_ROWS, NORM_ROWS)
        a_ref[pl.ds(r, NORM_ROWS), :] = _rms_rows(x_ref[pl.ds(r, NORM_ROWS), :], g_ref[...]).astype(BF16)
        return carry
    lax.fori_loop(0, x_ref.shape[0] // NORM_ROWS, body, 0)


def _norm_matmul_kernel(x_ref, g_ref, w_ref, o_ref, a_ref):
    @pl.when(pl.program_id(1) == 0)
    def _():
        _fill_rmsnorm(x_ref, g_ref, a_ref)
    o_ref[...] = _dot(a_ref[...], w_ref[...]).astype(o_ref.dtype)


def norm_matmul(x, gain, w, out_dtype):
    m, k = x.shape
    n = w.shape[1]
    tm, tn = min(MM_TM, m), min(MM_TN, n)
    return pl.pallas_call(
        _norm_matmul_kernel,
        out_shape=jax.ShapeDtypeStruct((m, n), out_dtype),
        grid=(m // tm, n // tn),
        in_specs=[pl.BlockSpec((tm, k), lambda i, j: (i, 0)),
                  pl.BlockSpec((1, k), lambda i, j: (0, 0)),
                  pl.BlockSpec((k, tn), lambda i, j: (0, j))],
        out_specs=pl.BlockSpec((tm, tn), lambda i, j: (i, j)),
        scratch_shapes=[pltpu.VMEM((tm, k), BF16)],
        compiler_params=_params("arbitrary", "arbitrary"),
        name="norm_matmul",
    )(x, gain.reshape(1, k), w)


def _matmul_res_kernel(a_ref, w_ref, r_ref, o_ref):
    o_ref[...] = r_ref[...] + _dot(a_ref[...], w_ref[...])


def matmul_residual(a, w, res):
    m, k = a.shape
    n = w.shape[1]
    tm, tn = min(MM_TM, m), min(MM_TN, n)
    return pl.pallas_call(
        _matmul_res_kernel,
        out_shape=jax.ShapeDtypeStruct((m, n), F32),
        grid=(m // tm, n // tn),
        in_specs=[pl.BlockSpec((tm, k), lambda i, j: (i, 0)),
                  pl.BlockSpec((k, tn), lambda i, j: (0, j)),
                  pl.BlockSpec((tm, tn), lambda i, j: (i, j))],
        out_specs=pl.BlockSpec((tm, tn), lambda i, j: (i, j)),
        compiler_params=_params("arbitrary", "arbitrary"),
        name="matmul_residual",
    )(a, w, res)


def _mlstm_out_kernel(hf_ref, hb_ref, og_ref, g_ref, w_ref, r_ref, o_ref, a_ref, *, heads):
    @pl.when(pl.program_id(1) == 0)
    def _():
        dv = a_ref.shape[1] // heads

        def body(c, carry):
            r = pl.multiple_of(c * NORM_ROWS, NORM_ROWS)
            rows = pl.ds(r, NORM_ROWS)
            for h in range(heads):
                cols = slice(h * dv, (h + 1) * dv)
                hh = hf_ref[rows, cols].astype(F32) + hb_ref[rows, cols].astype(F32)
                y = _rms_rows(hh, g_ref[:, cols]) * jax.nn.sigmoid(og_ref[rows, cols].astype(F32))
                a_ref[rows, cols] = y.astype(BF16)
            return carry
        lax.fori_loop(0, a_ref.shape[0] // NORM_ROWS, body, 0)
    o_ref[...] = r_ref[...] + _dot(a_ref[...], w_ref[...])


def mlstm_out_proj(hf, hb, proj, head_norm, w, res):
    m, d = hf.shape
    n = w.shape[1]
    tm, tn = min(MM_TM, m), min(MM_TN, n)
    o_col_block = (proj.shape[1] - d) // d
    return pl.pallas_call(
        functools.partial(_mlstm_out_kernel, heads=MLSTM_HEADS),
        out_shape=jax.ShapeDtypeStruct((m, n), F32),
        grid=(m // tm, n // tn),
        in_specs=[pl.BlockSpec((tm, d), lambda i, j: (i, 0)),
                  pl.BlockSpec((tm, d), lambda i, j: (i, 0)),
                  pl.BlockSpec((tm, d), lambda i, j: (i, o_col_block)),
                  pl.BlockSpec((1, d), lambda i, j: (0, 0)),
                  pl.BlockSpec((d, tn), lambda i, j: (0, j)),
                  pl.BlockSpec((tm, tn), lambda i, j: (i, j))],
        out_specs=pl.BlockSpec((tm, tn), lambda i, j: (i, j)),
        scratch_shapes=[pltpu.VMEM((tm, d), BF16)],
        compiler_params=_params("arbitrary", "arbitrary"),
        name="mlstm_out_proj",
    )(hf, hb, proj, head_norm.reshape(1, d), w, res)


def _swiglu_step(a, wg_ref, wu_ref, wd_ref):
    gate = _dot(a, wg_ref[...])
    up = _dot(a, wu_ref[...])
    hidden = (gate * jax.nn.sigmoid(gate) * up).astype(BF16)
    return _dot(hidden, wd_ref[...])


def _ffn_kernel(x_ref, g_ref, wg_ref, wu_ref, wd_ref, o_ref, a_ref):
    @pl.when(pl.program_id(1) == 0)
    def _():
        _fill_rmsnorm(x_ref, g_ref, a_ref)
        o_ref[...] = x_ref[...]
    o_ref[...] += _swiglu_step(a_ref[...], wg_ref, wu_ref, wd_ref)


def dense_ffn(x, gain, wg, wu, wd):
    m, d = x.shape
    f = wg.shape[1]
    tm, tf = min(FFN_TM, m), min(FFN_TF, f)
    return pl.pallas_call(
        _ffn_kernel,
        out_shape=jax.ShapeDtypeStruct((m, d), F32),
        grid=(m // tm, f // tf),
        in_specs=[pl.BlockSpec((tm, d), lambda i, j: (i, 0)),
                  pl.BlockSpec((1, d), lambda i, j: (0, 0)),
                  pl.BlockSpec((d, tf), lambda i, j: (0, j)),
                  pl.BlockSpec((d, tf), lambda i, j: (0, j)),
                  pl.BlockSpec((tf, d), lambda i, j: (j, 0))],
        out_specs=pl.BlockSpec((tm, d), lambda i, j: (i, 0)),
        scratch_shapes=[pltpu.VMEM((tm, d), BF16)],
        compiler_params=_params("arbitrary", "arbitrary"),
        name="dense_ffn",
    )(x, gain.reshape(1, d), wg, wu, wd)


def _unpack_bf16_pairs(words):
    hi = lax.bitcast_convert_type(words & jnp.uint32(0xFFFF0000), F32).astype(BF16)
    lo = lax.bitcast_convert_type(words << 16, F32).astype(BF16)
    return hi, lo


def _pack_bf16_pairs(hi, lo):
    return lax.bitcast_convert_type(hi, U32) | (lax.bitcast_convert_type(lo, U32) >> 16)


def _moe_ffn_kernel(te_ref, nu_ref, xs_ref, wg_ref, wu_ref, wd_ref, o_ref, a_ref):
    i, j = pl.program_id(0), pl.program_id(1)

    @pl.when(i < nu_ref[0])
    def _():
        @pl.when(j == 0)
        def _():
            half = a_ref.shape[1] // 2

            def body(c, carry):
                r = pl.multiple_of(c * NORM_ROWS, NORM_ROWS)
                hi, lo = _unpack_bf16_pairs(xs_ref[pl.ds(r, NORM_ROWS), :])
                a_ref[pl.ds(r, NORM_ROWS), :half] = hi
                a_ref[pl.ds(r, NORM_ROWS), half:] = lo
                return carry
            lax.fori_loop(0, a_ref.shape[0] // NORM_ROWS, body, 0)
            o_ref[...] = jnp.zeros_like(o_ref)
        o_ref[...] += _swiglu_step(a_ref[...], wg_ref, wu_ref, wd_ref)


def moe_ffn(tile_expert, n_used, xs, wg, wu, wd):
    p, half = xs.shape
    d = 2 * half
    f = wg.shape[2]
    tm, tf = MOE_TM, min(FFN_TF, f)
    nf = f // tf

    def row_map(i, j, te, nu):
        return (jnp.minimum(i, nu[0] - 1), 0)

    def f_idx(i, j, nu):
        return jnp.where(i < nu[0], j, nf - 1)

    def te_idx(i, te, nu):
        return te[jnp.minimum(i, nu[0] - 1)]

    return pl.pallas_call(
        _moe_ffn_kernel,
        out_shape=jax.ShapeDtypeStruct((p, d), F32),
        grid_spec=pltpu.PrefetchScalarGridSpec(
            num_scalar_prefetch=2,
            grid=(p // tm, nf),
            in_specs=[pl.BlockSpec((tm, half), row_map),
                      pl.BlockSpec((None, d, tf), lambda i, j, te, nu: (te_idx(i, te, nu), 0, f_idx(i, j, nu))),
                      pl.BlockSpec((None, d, tf), lambda i, j, te, nu: (te_idx(i, te, nu), 0, f_idx(i, j, nu))),
                      pl.BlockSpec((None, tf, d), lambda i, j, te, nu: (te_idx(i, te, nu), f_idx(i, j, nu), 0))],
            out_specs=pl.BlockSpec((tm, d), row_map),
            scratch_shapes=[pltpu.VMEM((tm, d), BF16)]),
        compiler_params=_params("arbitrary", "arbitrary"),
        name="moe_ffn",
    )(tile_expert, n_used, xs, wg, wu, wd)


def _mlstm_gates_kernel(x_ref, g_ref, wt_ref, b_ref, o_ref):
    xn = _rms_rows(x_ref[...], g_ref[...])
    xh = xn.astype(BF16)
    xl = (xn - xh.astype(F32)).astype(BF16)
    w = wt_ref[...]
    wh = w.astype(BF16)
    wl = (w - wh.astype(F32)).astype(BF16)
    pre = _dot_nt(wh, xh) + (_dot_nt(wh, xl) + _dot_nt(wl, xh)) + b_ref[...]
    g = GATE_SOFTCAP * jnp.tanh(pre / GATE_SOFTCAP)
    log_sig = jnp.minimum(g, 0.0) - jnp.log1p(jnp.exp(-jnp.abs(g)))
    row = lax.broadcasted_iota(jnp.int32, g.shape, 0)
    is_forget = ((row >= MLSTM_HEADS) & (row < 2 * MLSTM_HEADS)) | (row >= 3 * MLSTM_HEADS)
    o_ref[...] = jnp.where(is_forget, log_sig, g)


def mlstm_gates(x, gain, w_gates_t, b_gates):
    t, d = x.shape
    ng = w_gates_t.shape[0]
    tt = min(ROUTER_TT, t)
    return pl.pallas_call(
        _mlstm_gates_kernel,
        out_shape=jax.ShapeDtypeStruct((ng, t), F32),
        grid=(t // tt,),
        in_specs=[pl.BlockSpec((tt, d), lambda i: (i, 0)),
                  pl.BlockSpec((1, d), lambda i: (0, 0)),
                  pl.BlockSpec((ng, d), lambda i: (0, 0)),
                  pl.BlockSpec((ng, 1), lambda i: (0, 0))],
        out_specs=pl.BlockSpec((ng, tt), lambda i: (0, i)),
        compiler_params=_params("arbitrary"),
        name="mlstm_gates",
    )(x, gain.reshape(1, d), w_gates_t, b_gates.reshape(ng, 1))


def _mlstm_chunk(q_ref, k_ref, v_ref, li_ref, lf_ref, h_ref, c_ref, n_ref, m_ref, d, reverse):
    L = q_ref.shape[0]
    q = q_ref[...] * jnp.asarray(MLSTM_DK ** -0.5, BF16)
    k = k_ref[...]
    v = v_ref[...]
    li = li_ref[0]
    lf = lf_ref[0]
    r_i = lax.broadcasted_iota(jnp.int32, (L, L), 0)
    c_i = lax.broadcasted_iota(jnp.int32, (L, L), 1)
    visible = (c_i >= r_i) if reverse else (c_i <= r_i)
    cum = (r_i >= c_i) if reverse else (r_i <= c_i)

    hi, mid, lo = (p.astype(F32) for p in _split3(lf))
    prow = lax.broadcasted_iota(jnp.int32, (16, L), 0)
    pieces = jnp.where(prow == 0, hi, jnp.where(prow == 1, mid, jnp.where(prow == 2, lo, 0.0)))
    b = jnp.sum(_dot(pieces.astype(BF16), jnp.where(cum, 1.0, 0.0).astype(BF16)), axis=0, keepdims=True)
    g = jnp.sum(lf, axis=-1, keepdims=True)
    u = li - b

    m_prev = m_ref[d][:, :1]
    a_mat = jnp.where(visible, jnp.broadcast_to(u, (L, L)), -jnp.inf)
    m_row = jnp.maximum(jnp.max(a_mat, axis=-1, keepdims=True), m_prev)
    w_intra = (jnp.exp(a_mat - m_row) * _dot_nt(q, k)).astype(BF16)
    w_inter = jnp.exp(m_prev - m_row)

    ones = jnp.ones((L, LANES), BF16)
    num = w_inter * _dot(q, c_ref[d].astype(BF16)) + _dot(w_intra, v)
    den = w_inter * _dot(q, n_ref[d].astype(BF16)) + _dot(w_intra, ones)
    b_col = jnp.sum(jnp.where(r_i == c_i, jnp.broadcast_to(b, (L, L)), 0.0), axis=-1, keepdims=True)
    inv = 1.0 / jnp.maximum(jnp.abs(den), jnp.exp(-(b_col + m_row)))
    h_ref[...] = (num * jnp.concatenate([inv] * (num.shape[1] // LANES), axis=1)).astype(h_ref.dtype)

    m_new = g + jnp.maximum(m_prev, jnp.max(u, axis=-1, keepdims=True))
    decay = jnp.exp(g + m_prev - m_new)
    ks_t = (k.astype(F32).T * jnp.exp(g + u - m_new)).astype(BF16)
    c_ref[d] = decay * c_ref[d] + _dot(ks_t, v)
    n_ref[d] = decay * n_ref[d] + _dot(ks_t, ones)
    m_ref[d] = jnp.broadcast_to(m_new, (1, LANES))


def _mlstm_kernel(qf, kf, vf, lif, lff, qb, kb, vb, lib, lfb, hf_ref, hb_ref, c_ref, n_ref, m_ref):
    @pl.when(pl.program_id(1) == 0)
    def _():
        c_ref[...] = jnp.zeros_like(c_ref)
        n_ref[...] = jnp.zeros_like(n_ref)
        m_ref[...] = jnp.full_like(m_ref, -jnp.inf)
    _mlstm_chunk(qf, kf, vf, lif, lff, hf_ref, c_ref, n_ref, m_ref, 0, False)
    _mlstm_chunk(qb, kb, vb, lib, lfb, hb_ref, c_ref, n_ref, m_ref, 1, True)


def mlstm_scan(proj, gates, batch, seq, d_model):
    heads, dk = MLSTM_HEADS, MLSTM_DK
    dv = d_model // heads
    L = min(MLSTM_CHUNK, seq)
    nc = seq // L
    t = batch * seq
    k_col0 = heads
    v_col0 = 2 * heads * dk // dv

    def fwd(bh, j):
        return (bh // heads) * nc + j

    def bwd(bh, j):
        return (bh // heads) * nc + (nc - 1 - j)

    def qkv_specs(pos):
        return [pl.BlockSpec((L, dk), lambda bh, j: (pos(bh, j), bh % heads)),
                pl.BlockSpec((L, dk), lambda bh, j: (pos(bh, j), k_col0 + bh % heads)),
                pl.BlockSpec((L, dv), lambda bh, j: (pos(bh, j), v_col0 + bh % heads))]

    def gate_spec(kind, pos):
        return pl.BlockSpec((1, 1, L), lambda bh, j: (kind * heads + bh % heads, 0, pos(bh, j)))

    out_f = pl.BlockSpec((L, dv), lambda bh, j: (fwd(bh, j), bh % heads))
    out_b = pl.BlockSpec((L, dv), lambda bh, j: (bwd(bh, j), bh % heads))
    return pl.pallas_call(
        _mlstm_kernel,
        out_shape=(jax.ShapeDtypeStruct((t, d_model), BF16), jax.ShapeDtypeStruct((t, d_model), BF16)),
        grid=(batch * heads, nc),
        in_specs=(qkv_specs(fwd) + [gate_spec(0, fwd), gate_spec(1, fwd)]
                  + qkv_specs(bwd) + [gate_spec(2, bwd), gate_spec(3, bwd)]),
        out_specs=(out_f, out_b),
        scratch_shapes=[pltpu.VMEM((2, dk, dv), F32), pltpu.VMEM((2, dk, LANES), F32),
                        pltpu.VMEM((2, 1, LANES), F32)],
        compiler_params=_params("arbitrary", "arbitrary"),
        name="mlstm_scan",
    )(proj, proj, proj, gates, gates, proj, proj, proj, gates, gates)


def _na_kernel(q_ref, k0_ref, k1_ref, k2_ref, v0_ref, v1_ref, v2_ref, bias_ref, qg_ref, kg_ref, o_ref):
    hd = NA_HEAD_DIM
    for h in range(q_ref.shape[1] // hd):
        cols = slice(h * hd, (h + 1) * hd)
        q = _rms_rows(q_ref[:, cols].astype(F32), qg_ref[...]) * (hd ** -0.5)
        k = jnp.concatenate([k0_ref[:, cols], k1_ref[:, cols], k2_ref[:, cols]], axis=0)
        k = _rms_rows(k.astype(F32), kg_ref[...])
        v = jnp.concatenate([v0_ref[:, cols], v1_ref[:, cols], v2_ref[:, cols]], axis=0)
        s = _dot_nt(q.astype(BF16), k.astype(BF16)) + bias_ref[0, h]
        p = jnp.exp(s - jnp.max(s, axis=-1, keepdims=True))
        o = _dot(p.astype(BF16), v) * (1.0 / jnp.sum(p, axis=-1, keepdims=True))
        o_ref[:, cols] = o.astype(o_ref.dtype)


def _na_bias_tables(rpb, rows):
    rb, w = NA_ROWS_PER_BLOCK, GRID_W
    nb = rows // rb
    tables = []
    for blk in (0, 1, nb - 1):
        ws = min(max(blk - 1, 0), nb - 3) * rb
        r = blk * rb + np.arange(rb)[:, None, None, None]
        c = np.arange(w)[None, :, None, None]
        a = ws + np.arange(3 * rb)[None, None, :, None]
        kc = np.arange(w)[None, None, None, :]
        rs = np.clip(r - NA_KH // 2, 0, rows - NA_KH)
        cs = np.clip(c - NA_KW // 2, 0, w - NA_KW)
        ok = (a >= rs) & (a < rs + NA_KH) & (kc >= cs) & (kc < cs + NA_KW)
        dr = np.clip(a - r + NA_KH - 1, 0, 2 * NA_KH - 2) + 0 * kc + 0 * c
        dc = np.clip(kc - c + NA_KW - 1, 0, 2 * NA_KW - 2) + 0 * a + 0 * r
        ok = np.broadcast_to(ok, dr.shape).reshape(rb * w, 3 * rb * w)
        vals = rpb[:, dr.reshape(rb * w, 3 * rb * w), dc.reshape(rb * w, 3 * rb * w)]
        tables.append(jnp.where(ok[None], vals.astype(F32), NEG_BIG))
    return jnp.stack(tables)


def na_attention(qkv, q_norm, k_norm, rpb, batch, seq, d_model):
    rb, w, hd = NA_ROWS_PER_BLOCK, GRID_W, NA_HEAD_DIM
    rows = seq // w
    nb = rows // rb
    tq = rb * w
    hps = NA_HEADS_PER_STEP
    ng = NA_HEADS // hps
    cw = hps * hd
    t = batch * seq
    bias = _na_bias_tables(rpb, rows)

    def kv_spec(part, off):
        return pl.BlockSpec((tq, cw), lambda g, b, i: (b * nb + jnp.clip(i - 1, 0, nb - 3) + off, part * ng + g))

    def bias_type(i):
        return jnp.where(i == 0, 0, jnp.where(i == nb - 1, 2, 1))

    return pl.pallas_call(
        _na_kernel,
        out_shape=jax.ShapeDtypeStruct((t, d_model), BF16),
        grid=(ng, batch, nb),
        in_specs=([pl.BlockSpec((tq, cw), lambda g, b, i: (b * nb + i, g))]
                  + [kv_spec(1, off) for off in range(3)] + [kv_spec(2, off) for off in range(3)]
                  + [pl.BlockSpec((1, hps, tq, 3 * tq), lambda g, b, i: (bias_type(i), g, 0, 0)),
                     pl.BlockSpec((1, hd), lambda g, b, i: (0, 0)),
                     pl.BlockSpec((1, hd), lambda g, b, i: (0, 0))]),
        out_specs=pl.BlockSpec((tq, cw), lambda g, b, i: (b * nb + i, g)),
        compiler_params=_params("arbitrary", "arbitrary", "arbitrary"),
        name="na_attention",
    )(qkv, qkv, qkv, qkv, qkv, qkv, qkv, bias, q_norm.reshape(1, hd), k_norm.reshape(1, hd))


def _router_kernel(x_ref, g_ref, wr_ref, xp_ref, meta_ref, cnt_ref, carry_ref):
    @pl.when(pl.program_id(0) == 0)
    def _():
        carry_ref[...] = jnp.zeros_like(carry_ref)

    tt = x_ref.shape[0]
    xn = _rms_rows(x_ref[...], g_ref[...])
    xh = xn.astype(BF16)
    xhf = xh.astype(F32)
    xl = (xn - xhf).astype(BF16)
    wr = wr_ref[...]
    wh = wr.astype(BF16)
    wl = (wr - wh.astype(F32)).astype(BF16)
    logits = _dot(xh, wh) + (_dot(xh, wl) + _dot(xl, wh))
    col = lax.broadcasted_iota(jnp.int32, logits.shape, 1).astype(F32)
    lg = jnp.where(col < N_EXPERTS, logits, -jnp.inf)
    m1 = jnp.max(lg, axis=-1, keepdims=True)
    i1 = jnp.min(jnp.where(lg == m1, col, float(LANES)), axis=-1, keepdims=True)
    lg2 = jnp.where(col == i1, -jnp.inf, lg)
    m2 = jnp.max(lg2, axis=-1, keepdims=True)
    i2 = jnp.min(jnp.where(lg2 == m2, col, float(LANES)), axis=-1, keepdims=True)
    e2 = jnp.exp(m2 - m1)
    w1 = 1.0 / (1.0 + e2)
    w2 = e2 * w1

    hit1, hit2 = col == i1, col == i2
    cnt = jnp.where(hit1 | hit2, 1.0, 0.0)
    r_i = lax.broadcasted_iota(jnp.int32, (tt, tt), 0)
    c_i = lax.broadcasted_iota(jnp.int32, (tt, tt), 1)
    before = jnp.where(c_i < r_i, 1.0, 0.0).astype(BF16)
    prefix = _dot(before, cnt.astype(BF16)) + carry_ref[...]
    rank1 = jnp.sum(jnp.where(hit1, prefix, 0.0), axis=-1, keepdims=True)
    rank2 = jnp.sum(jnp.where(hit2, prefix, 0.0), axis=-1, keepdims=True)
    carry_ref[...] += jnp.sum(cnt, axis=0, keepdims=True)
    cnt_ref[...] = carry_ref[...]

    meta = jnp.zeros(logits.shape, F32)
    for lane, val in enumerate((i1, i2, w1, w2, rank1, rank2)):
        meta = jnp.where(col == lane, val, meta)
    meta_ref[...] = meta
    half = xhf.shape[1] // 2
    xp_ref[...] = _pack_bf16_pairs(xhf[:, :half], xhf[:, half:])


def moe_router(x, gain, w_router):
    t, d = x.shape
    tt = min(ROUTER_TT, t)
    wr = jnp.zeros((d, LANES), F32).at[:, :N_EXPERTS].set(w_router)
    return pl.pallas_call(
        _router_kernel,
        out_shape=(jax.ShapeDtypeStruct((t, d // 2), U32), jax.ShapeDtypeStruct((t, LANES), F32),
                   jax.ShapeDtypeStruct((1, LANES), F32)),
        grid=(t // tt,),
        in_specs=[pl.BlockSpec((tt, d), lambda i: (i, 0)),
                  pl.BlockSpec((1, d), lambda i: (0, 0)),
                  pl.BlockSpec((d, LANES), lambda i: (0, 0))],
        out_specs=(pl.BlockSpec((tt, d // 2), lambda i: (i, 0)),
                   pl.BlockSpec((tt, LANES), lambda i: (i, 0)),
                   pl.BlockSpec((1, LANES), lambda i: (0, 0))),
        scratch_shapes=[pltpu.VMEM((1, LANES), F32)],
        compiler_params=_params("arbitrary"),
        name="moe_router",
    )(x, gain.reshape(1, d), wr)


def _scatter_kernel(pad_lo_ref, pad_hi_ref, p0_ref, p1_ref, src_ref, dst_ref, sem):
    tt = p0_ref.shape[2]
    base = pl.program_id(0) * tt

    def row_copy(src_row, dst_row):
        return pltpu.make_async_copy(src_ref.at[pl.ds(src_row, 1)], dst_ref.at[pl.ds(dst_row, 1)], sem)

    @pl.when(pl.program_id(0) == 0)
    def _():
        for e in range(N_EXPERTS):
            def pad_body(r, carry):
                row_copy(0, r).start()
                return carry
            lax.fori_loop(pad_lo_ref[e], pad_hi_ref[e], pad_body, 0)

            def pad_wait(r, carry):
                row_copy(0, 0).wait()
                return carry
            lax.fori_loop(pad_lo_ref[e], pad_hi_ref[e], pad_wait, 0)

    def start_body(t, carry):
        row_copy(base + t, p0_ref[0, 0, t]).start()
        row_copy(base + t, p1_ref[0, 0, t]).start()
        return carry
    lax.fori_loop(0, tt, start_body, 0)

    def wait_body(t, carry):
        row_copy(0, 0).wait()
        row_copy(0, 0).wait()
        return carry
    lax.fori_loop(0, tt, wait_body, 0)


def moe_scatter_rows(xp, pos0, pos1, pad_lo, pad_hi, n_rows):
    t, half = xp.shape
    tt = min(SCATTER_TT, t)
    nblk = t // tt
    smem_rows = pl.BlockSpec((1, 1, tt), lambda i, lo, hi: (i, 0, 0), memory_space=pltpu.SMEM)
    return pl.pallas_call(
        _scatter_kernel,
        out_shape=jax.ShapeDtypeStruct((n_rows, half), xp.dtype),
        grid_spec=pltpu.PrefetchScalarGridSpec(
            num_scalar_prefetch=2,
            grid=(nblk,),
            in_specs=[smem_rows, smem_rows, pl.BlockSpec(memory_space=pl.ANY)],
            out_specs=pl.BlockSpec(memory_space=pl.ANY),
            scratch_shapes=[pltpu.SemaphoreType.DMA]),
        compiler_params=pltpu.CompilerParams(dimension_semantics=("arbitrary",), has_side_effects=True),
        name="moe_scatter_rows",
    )(pad_lo, pad_hi, pos0.reshape(nblk, 1, tt), pos1.reshape(nblk, 1, tt), xp)


def _combine_kernel(p0_ref, p1_ref, x_ref, meta_ref, y_ref, o_ref, buf_ref, sem):
    tt = x_ref.shape[0]

    def row_copy(k, src_row, t):
        return pltpu.make_async_copy(y_ref.at[pl.ds(src_row, 1)], buf_ref.at[k, pl.ds(t, 1)], sem)

    def start_body(t, carry):
        row_copy(0, p0_ref[0, 0, t], t).start()
        row_copy(1, p1_ref[0, 0, t], t).start()
        return carry
    lax.fori_loop(0, tt, start_body, 0)

    def wait_body(t, carry):
        row_copy(0, 0, 0).wait()
        row_copy(1, 0, 0).wait()
        return carry
    lax.fori_loop(0, tt, wait_body, 0)

    meta = meta_ref[...]
    o_ref[...] = x_ref[...] + (meta[:, 2:3] * buf_ref[0] + meta[:, 3:4] * buf_ref[1])


def moe_combine(x, meta, pos0, pos1, y):
    t, d = x.shape
    tt = min(COMBINE_TT, t)
    nblk = t // tt
    smem_rows = pl.BlockSpec((1, 1, tt), lambda i: (i, 0, 0), memory_space=pltpu.SMEM)
    return pl.pallas_call(
        _combine_kernel,
        out_shape=jax.ShapeDtypeStruct((t, d), F32),
        grid=(nblk,),
        in_specs=[smem_rows, smem_rows,
                  pl.BlockSpec((tt, d), lambda i: (i, 0)),
                  pl.BlockSpec((tt, LANES), lambda i: (i, 0)),
                  pl.BlockSpec(memory_space=pl.ANY)],
        out_specs=pl.BlockSpec((tt, d), lambda i: (i, 0)),
        scratch_shapes=[pltpu.VMEM((2, tt, d), F32), pltpu.SemaphoreType.DMA],
        compiler_params=_params("arbitrary"),
        name="moe_combine",
    )(pos0.reshape(nblk, 1, tt), pos1.reshape(nblk, 1, tt), x, meta, y)


def mlstm_layer(x, batch, seq, mix_norm, w_in, b_gates, head_norm, w_out):
    t, d = x.shape
    n_main = w_in.shape[1] - 4 * MLSTM_HEADS
    proj = norm_matmul(x, mix_norm, w_in[:, :n_main].astype(BF16), BF16)
    gates = mlstm_gates(x, mix_norm, w_in[:, n_main:].T, b_gates)
    hf, hb = mlstm_scan(proj, gates.reshape(4 * MLSTM_HEADS, 1, t), batch, seq, d)
    return mlstm_out_proj(hf, hb, proj, head_norm, w_out.astype(BF16), x)


def na_layer(x, batch, seq, mix_norm, w_qkv, q_norm, k_norm, rpb, w_out):
    d = x.shape[1]
    qkv = norm_matmul(x, mix_norm, w_qkv.astype(BF16), BF16)
    attn = na_attention(qkv, q_norm, k_norm, rpb, batch, seq, d)
    return matmul_residual(attn, w_out.astype(BF16), x)


def moe_layer(x, ffn_norm, w_router, w_gate, w_up, w_down):
    t, d = x.shape
    tm = MOE_TM
    xp, meta, counts = moe_router(x, ffn_norm, w_router)
    counts = counts[0, :N_EXPERTS].astype(jnp.int32)
    tiles = (counts + tm - 1) // tm
    tile_end = jnp.cumsum(tiles)
    row_start = (tile_end - tiles) * tm
    n_tiles = TOP_K * t // tm + N_EXPERTS
    tile_expert = jnp.minimum(jnp.sum(jnp.arange(n_tiles)[:, None] >= tile_end[None, :], axis=1),
                              N_EXPERTS - 1).astype(jnp.int32)
    n_used = tile_end[-1:].astype(jnp.int32)
    e0, e1 = meta[:, 0].astype(jnp.int32), meta[:, 1].astype(jnp.int32)
    pos0 = row_start[e0] + meta[:, 4].astype(jnp.int32)
    pos1 = row_start[e1] + meta[:, 5].astype(jnp.int32)
    xs = moe_scatter_rows(xp, pos0, pos1, (row_start + counts).astype(jnp.int32),
                          (tile_end * tm).astype(jnp.int32), n_tiles * tm)
    y = moe_ffn(tile_expert, n_used, xs, w_gate.astype(BF16), w_up.astype(BF16), w_down.astype(BF16))
    return moe_combine(x, meta, pos0, pos1, y)


def kernel(x, l0_mix_norm, l0_mlstm_w_in, l0_mlstm_b_gates, l0_mlstm_head_norm, l0_mlstm_w_out, l0_ffn_norm, l0_ffn_w_gate, l0_ffn_w_up, l0_ffn_w_down, l1_mix_norm, l1_na_w_qkv, l1_na_q_norm, l1_na_k_norm, l1_na_rpb, l1_na_w_out, l1_ffn_norm, l1_moe_w_router, l1_moe_w_gate, l1_moe_w_up, l1_moe_w_down, l2_mix_norm, l2_mlstm_w_in, l2_mlstm_b_gates, l2_mlstm_head_norm, l2_mlstm_w_out, l2_ffn_norm, l2_ffn_w_gate, l2_ffn_w_up, l2_ffn_w_down, l3_mix_norm, l3_na_w_qkv, l3_na_q_norm, l3_na_k_norm, l3_na_rpb, l3_na_w_out, l3_ffn_norm, l3_moe_w_router, l3_moe_w_gate, l3_moe_w_up, l3_moe_w_down):
    batch, seq, d = x.shape
    h = x.reshape(batch * seq, d)
    h = mlstm_layer(h, batch, seq, l0_mix_norm, l0_mlstm_w_in, l0_mlstm_b_gates, l0_mlstm_head_norm, l0_mlstm_w_out)
    h = dense_ffn(h, l0_ffn_norm, l0_ffn_w_gate.astype(BF16), l0_ffn_w_up.astype(BF16), l0_ffn_w_down.astype(BF16))
    h = na_layer(h, batch, seq, l1_mix_norm, l1_na_w_qkv, l1_na_q_norm, l1_na_k_norm, l1_na_rpb, l1_na_w_out)
    h = moe_layer(h, l1_ffn_norm, l1_moe_w_router, l1_moe_w_gate, l1_moe_w_up, l1_moe_w_down)
    h = mlstm_layer(h, batch, seq, l2_mix_norm, l2_mlstm_w_in, l2_mlstm_b_gates, l2_mlstm_head_norm, l2_mlstm_w_out)
    h = dense_ffn(h, l2_ffn_norm, l2_ffn_w_gate.astype(BF16), l2_ffn_w_up.astype(BF16), l2_ffn_w_down.astype(BF16))
    h = na_layer(h, batch, seq, l3_mix_norm, l3_na_w_qkv, l3_na_q_norm, l3_na_k_norm, l3_na_rpb, l3_na_w_out)
    h = moe_layer(h, l3_ffn_norm, l3_moe_w_router, l3_moe_w_gate, l3_moe_w_up, l3_moe_w_down)
    return h.reshape(batch, seq, d)
```

```python
import functools

import jax
import jax.numpy as jnp
import numpy as np
from jax import lax
from jax.experimental import pallas as pl
from jax.experimental.pallas import tpu as pltpu

F32 = jnp.float32
BF16 = jnp.bfloat16
U32 = jnp.uint32

NORM_EPS = 1e-6
LANES = 128
VMEM_LIMIT_BYTES = 56 << 20
NEG_BIG = -1e30

MLSTM_HEADS = 4
MLSTM_DK = 256
GATE_SOFTCAP = 15.0
MLSTM_CHUNK = 256
GRID_W = 64
NA_HEADS = 16
NA_HEAD_DIM = 128
NA_KH = 8
NA_KW = 16
NA_ROWS_PER_BLOCK = 4
NA_HEADS_PER_STEP = 4
N_EXPERTS = 8
TOP_K = 2

MM_TM, MM_TN = 1024, 1024
MLSTM_OUT_TM = 512
FFN_TM, FFN_TF = 512, 512
MOE_TM = 512
ROUTER_TT = 512
SCATTER_TT = 1024
COMBINE_TT = 256
NORM_ROWS = 128


def _params(*semantics):
    return pltpu.CompilerParams(dimension_semantics=semantics, vmem_limit_bytes=VMEM_LIMIT_BYTES)


def _dot(a, b):
    return jnp.dot(a, b, preferred_element_type=F32)


def _dot_nt(a, b):
    return lax.dot_general(a, b, (((1,), (1,)), ((), ())), preferred_element_type=F32)


def _split3(x):
    hi = x.astype(BF16)
    r1 = x - hi.astype(F32)
    mid = r1.astype(BF16)
    lo = (r1 - mid.astype(F32)).astype(BF16)
    return hi, mid, lo


def _rms_rows(x, gain):
    ms = jnp.mean(x * x, axis=-1, keepdims=True)
    return x * lax.rsqrt(ms + NORM_EPS) * gain


def _fill_rmsnorm(x_ref, g_ref, a_ref):
    def body(c, carry):
        r = pl.multiple_of(c * NORM_ROWS, NORM_ROWS)
        a_ref[pl.ds(r, NORM_ROWS), :] = _rms_rows(x_ref[pl.ds(r, NORM_ROWS), :], g_ref[...]).astype(BF16)
        return carry
    lax.fori_loop(0, x_ref.shape[0] // NORM_ROWS, body, 0)


def _norm_matmul_kernel(x_ref, g_ref, w_ref, o_ref, a_ref):
    @pl.when(pl.program_id(1) == 0)
    def _():
        _fill_rmsnorm(x_ref, g_ref, a_ref)
    o_ref[...] = _dot(a_ref[...], w_ref[...]).astype(o_ref.dtype)


def norm_matmul(x, gain, w, out_dtype):
    m, k = x.shape
    n = w.shape[1]
    tm, tn = min(MM_TM, m), min(MM_TN, n)
    return pl.pallas_call(
        _norm_matmul_kernel,
        out_shape=jax.ShapeDtypeStruct((m, n), out_dtype),
        grid=(m // tm, n // tn),
        in_specs=[pl.BlockSpec((tm, k), lambda i, j: (i, 0)),
                  pl.BlockSpec((1, k), lambda i, j: (0, 0)),
                  pl.BlockSpec((k, tn), lambda i, j: (0, j))],
        out_specs=pl.BlockSpec((tm, tn), lambda i, j: (i, j)),
        scratch_shapes=[pltpu.VMEM((tm, k), BF16)],
        compiler_params=_params("arbitrary", "arbitrary"),
        name="norm_matmul",
    )(x, gain.reshape(1, k), w)


def _matmul_res_kernel(a_ref, w_ref, r_ref, o_ref):
    o_ref[...] = r_ref[...] + _dot(a_ref[...], w_ref[...])


def matmul_residual(a, w, res):
    m, k = a.shape
    n = w.shape[1]
    tm, tn = min(MM_TM, m), min(MM_TN, n)
    return pl.pallas_call(
        _matmul_res_kernel,
        out_shape=jax.ShapeDtypeStruct((m, n), F32),
        grid=(m // tm, n // tn),
        in_specs=[pl.BlockSpec((tm, k), lambda i, j: (i, 0)),
                  pl.BlockSpec((k, tn), lambda i, j: (0, j)),
                  pl.BlockSpec((tm, tn), lambda i, j: (i, j))],
        out_specs=pl.BlockSpec((tm, tn), lambda i, j: (i, j)),
        compiler_params=_params("arbitrary", "arbitrary"),
        name="matmul_residual",
    )(a, w, res)


def _mlstm_out_kernel(hf_ref, hb_ref, og_ref, g_ref, w_ref, r_ref, o_ref, a_ref, *, heads):
    @pl.when(pl.program_id(1) == 0)
    def _():
        dv = a_ref.shape[1] // heads

        def body(c, carry):
            r = pl.multiple_of(c * NORM_ROWS, NORM_ROWS)
            rows = pl.ds(r, NORM_ROWS)
            for h in range(heads):
                cols = slice(h * dv, (h + 1) * dv)
                hh = hf_ref[rows, cols].astype(F32) + hb_ref[rows, cols].astype(F32)
                y = _rms_rows(hh, g_ref[:, cols]) * jax.nn.sigmoid(og_ref[rows, cols].astype(F32))
                a_ref[rows, cols] = y.astype(BF16)
            return carry
        lax.fori_loop(0, a_ref.shape[0] // NORM_ROWS, body, 0)
    o_ref[...] = r_ref[...] + _dot(a_ref[...], w_ref[...])


def mlstm_out_proj(hf, hb, proj, head_norm, w, res):
    m, d = hf.shape
    n = w.shape[1]
    tm, tn = min(MLSTM_OUT_TM, m), min(MM_TN, n)
    o_col_block = (proj.shape[1] - d) // d
    return pl.pallas_call(
        functools.partial(_mlstm_out_kernel, heads=MLSTM_HEADS),
        out_shape=jax.ShapeDtypeStruct((m, n), F32),
        grid=(m // tm, n // tn),
        in_specs=[pl.BlockSpec((tm, d), lambda i, j: (i, 0)),
                  pl.BlockSpec((tm, d), lambda i, j: (i, 0)),
                  pl.BlockSpec((tm, d), lambda i, j: (i, o_col_block)),
                  pl.BlockSpec((1, d), lambda i, j: (0, 0)),
                  pl.BlockSpec((d, tn), lambda i, j: (0, j)),
                  pl.BlockSpec((tm, tn), lambda i, j: (i, j))],
        out_specs=pl.BlockSpec((tm, tn), lambda i, j: (i, j)),
        scratch_shapes=[pltpu.VMEM((tm, d), BF16)],
        compiler_params=_params("arbitrary", "arbitrary"),
        name="mlstm_out_proj",
    )(hf, hb, proj, head_norm.reshape(1, d), w, res)


def _swiglu_step(a, wg_ref, wu_ref, wd_ref):
    gate = _dot(a, wg_ref[...])
    up = _dot(a, wu_ref[...])
    hidden = (gate * jax.nn.sigmoid(gate) * up).astype(BF16)
    return _dot(hidden, wd_ref[...])


def _ffn_kernel(x_ref, g_ref, wg_ref, wu_ref, wd_ref, o_ref, a_ref):
    @pl.when(pl.program_id(1) == 0)
    def _():
        _fill_rmsnorm(x_ref, g_ref, a_ref)
        o_ref[...] = x_ref[...]
    o_ref[...] += _swiglu_step(a_ref[...], wg_ref, wu_ref, wd_ref)


def dense_ffn(x, gain, wg, wu, wd):
    m, d = x.shape
    f = wg.shape[1]
    tm, tf = min(FFN_TM, m), min(FFN_TF, f)
    return pl.pallas_call(
        _ffn_kernel,
        out_shape=jax.ShapeDtypeStruct((m, d), F32),
        grid=(m // tm, f // tf),
        in_specs=[pl.BlockSpec((tm, d), lambda i, j: (i, 0)),
                  pl.BlockSpec((1, d), lambda i, j: (0, 0)),
                  pl.BlockSpec((d, tf), lambda i, j: (0, j)),
                  pl.BlockSpec((d, tf), lambda i, j: (0, j)),
                  pl.BlockSpec((tf, d), lambda i, j: (j, 0))],
        out_specs=pl.BlockSpec((tm, d), lambda i, j: (i, 0)),
        scratch_shapes=[pltpu.VMEM((tm, d), BF16)],
        compiler_params=_params("arbitrary", "arbitrary"),
        name="dense_ffn",
    )(x, gain.reshape(1, d), wg, wu, wd)


def _unpack_bf16_pairs(words):
    hi = lax.bitcast_convert_type(words & jnp.uint32(0xFFFF0000), F32).astype(BF16)
    lo = lax.bitcast_convert_type(words << 16, F32).astype(BF16)
    return hi, lo


def _pack_bf16_pairs(hi, lo):
    return lax.bitcast_convert_type(hi, U32) | (lax.bitcast_convert_type(lo, U32) >> 16)


def _moe_ffn_kernel(te_ref, nu_ref, xs_ref, wg_ref, wu_ref, wd_ref, o_ref, a_ref):
    i, j = pl.program_id(0), pl.program_id(1)

    @pl.when(i < nu_ref[0])
    def _():
        @pl.when(j == 0)
        def _():
            half = a_ref.shape[1] // 2

            def body(c, carry):
                r = pl.multiple_of(c * NORM_ROWS, NORM_ROWS)
                hi, lo = _unpack_bf16_pairs(xs_ref[pl.ds(r, NORM_ROWS), :])
                a_ref[pl.ds(r, NORM_ROWS), :half] = hi
                a_ref[pl.ds(r, NORM_ROWS), half:] = lo
                return carry
            lax.fori_loop(0, a_ref.shape[0] // NORM_ROWS, body, 0)
            o_ref[...] = jnp.zeros_like(o_ref)
        o_ref[...] += _swiglu_step(a_ref[...], wg_ref, wu_ref, wd_ref)

    @pl.when((i >= nu_ref[0]) & (j == 0))
    def _():
        o_ref[...] = jnp.zeros_like(o_ref)


def moe_ffn(tile_expert, n_used, xs, wg, wu, wd):
    p, half = xs.shape
    d = 2 * half
    f = wg.shape[2]
    tm, tf = MOE_TM, min(FFN_TF, f)
    nf = f // tf

    def row_map(i, j, te, nu):
        return (jnp.minimum(i, nu[0] - 1), 0)

    def f_idx(i, j, nu):
        return jnp.where(i < nu[0], j, nf - 1)

    def te_idx(i, te, nu):
        return te[jnp.minimum(i, nu[0] - 1)]

    return pl.pallas_call(
        _moe_ffn_kernel,
        out_shape=jax.ShapeDtypeStruct((p, d), F32),
        grid_spec=pltpu.PrefetchScalarGridSpec(
            num_scalar_prefetch=2,
            grid=(p // tm, nf),
            in_specs=[pl.BlockSpec((tm, half), row_map),
                      pl.BlockSpec((None, d, tf), lambda i, j, te, nu: (te_idx(i, te, nu), 0, f_idx(i, j, nu))),
                      pl.BlockSpec((None, d, tf), lambda i, j, te, nu: (te_idx(i, te, nu), 0, f_idx(i, j, nu))),
                      pl.BlockSpec((None, tf, d), lambda i, j, te, nu: (te_idx(i, te, nu), f_idx(i, j, nu), 0))],
            out_specs=pl.BlockSpec((tm, d), lambda i, j, te, nu: (i, 0)),
            scratch_shapes=[pltpu.VMEM((tm, d), BF16)]),
        compiler_params=_params("arbitrary", "arbitrary"),
        name="moe_ffn",
    )(tile_expert, n_used, xs, wg, wu, wd)


def _mlstm_gates_kernel(x_ref, g_ref, wt_ref, b_ref, o_ref):
    xn = _rms_rows(x_ref[...], g_ref[...])
    xh = xn.astype(BF16)
    xl = (xn - xh.astype(F32)).astype(BF16)
    w = wt_ref[...]
    wh = w.astype(BF16)
    wl = (w - wh.astype(F32)).astype(BF16)
    pre = _dot_nt(wh, xh) + (_dot_nt(wh, xl) + _dot_nt(wl, xh)) + b_ref[...]
    g = GATE_SOFTCAP * jnp.tanh(pre / GATE_SOFTCAP)
    log_sig = jnp.minimum(g, 0.0) - jnp.log1p(jnp.exp(-jnp.abs(g)))
    row = lax.broadcasted_iota(jnp.int32, g.shape, 0)
    is_forget = ((row >= MLSTM_HEADS) & (row < 2 * MLSTM_HEADS)) | (row >= 3 * MLSTM_HEADS)
    o_ref[...] = jnp.where(is_forget, log_sig, g)


def mlstm_gates(x, gain, w_gates_t, b_gates):
    t, d = x.shape
    ng = w_gates_t.shape[0]
    tt = min(ROUTER_TT, t)
    return pl.pallas_call(
        _mlstm_gates_kernel,
        out_shape=jax.ShapeDtypeStruct((ng, t), F32),
        grid=(t // tt,),
        in_specs=[pl.BlockSpec((tt, d), lambda i: (i, 0)),
                  pl.BlockSpec((1, d), lambda i: (0, 0)),
                  pl.BlockSpec((ng, d), lambda i: (0, 0)),
                  pl.BlockSpec((ng, 1), lambda i: (0, 0))],
        out_specs=pl.BlockSpec((ng, tt), lambda i: (0, i)),
        compiler_params=_params("arbitrary"),
        name="mlstm_gates",
    )(x, gain.reshape(1, d), w_gates_t, b_gates.reshape(ng, 1))


def _mlstm_chunk(q_ref, k_ref, v_ref, li_ref, lf_ref, h_ref, c_ref, n_ref, m_ref, d, reverse):
    L = q_ref.shape[0]
    q = q_ref[...] * jnp.asarray(MLSTM_DK ** -0.5, BF16)
    k = k_ref[...]
    v = v_ref[...]
    li = li_ref[0]
    lf = lf_ref[0]
    r_i = lax.broadcasted_iota(jnp.int32, (L, L), 0)
    c_i = lax.broadcasted_iota(jnp.int32, (L, L), 1)
    visible = (c_i >= r_i) if reverse else (c_i <= r_i)
    cum = (r_i >= c_i) if reverse else (r_i <= c_i)

    hi, mid, lo = (p.astype(F32) for p in _split3(lf))
    prow = lax.broadcasted_iota(jnp.int32, (16, L), 0)
    pieces = jnp.where(prow == 0, hi, jnp.where(prow == 1, mid, jnp.where(prow == 2, lo, 0.0)))
    b = jnp.sum(_dot(pieces.astype(BF16), jnp.where(cum, 1.0, 0.0).astype(BF16)), axis=0, keepdims=True)
    g = jnp.sum(lf, axis=-1, keepdims=True)
    u = li - b

    m_prev = m_ref[d][:, :1]
    a_mat = jnp.where(visible, jnp.broadcast_to(u, (L, L)), -jnp.inf)
    m_row = jnp.maximum(jnp.max(a_mat, axis=-1, keepdims=True), m_prev)
    w_intra = (jnp.exp(a_mat - m_row) * _dot_nt(q, k)).astype(BF16)
    w_inter = jnp.exp(m_prev - m_row)

    ones = jnp.ones((L, LANES), BF16)
    num = w_inter * _dot(q, c_ref[d].astype(BF16)) + _dot(w_intra, v)
    den = w_inter * _dot(q, n_ref[d].astype(BF16)) + _dot(w_intra, ones)
    b_col = jnp.sum(jnp.where(r_i == c_i, jnp.broadcast_to(b, (L, L)), 0.0), axis=-1, keepdims=True)
    inv = 1.0 / jnp.maximum(jnp.abs(den), jnp.exp(-(b_col + m_row)))
    h_ref[...] = (num * jnp.concatenate([inv] * (num.shape[1] // LANES), axis=1)).astype(h_ref.dtype)

    m_new = g + jnp.maximum(m_prev, jnp.max(u, axis=-1, keepdims=True))
    decay = jnp.exp(g + m_prev - m_new)
    ks_t = (k.astype(F32).T * jnp.exp(g + u - m_new)).astype(BF16)
    c_ref[d] = decay * c_ref[d] + _dot(ks_t, v)
    n_ref[d] = decay * n_ref[d] + _dot(ks_t, ones)
    m_ref[d] = jnp.broadcast_to(m_new, (1, LANES))


def _mlstm_kernel(qf, kf, vf, lif, lff, qb, kb, vb, lib, lfb, hf_ref, hb_ref, c_ref, n_ref, m_ref):
    @pl.when(pl.program_id(1) == 0)
    def _():
        c_ref[...] = jnp.zeros_like(c_ref)
        n_ref[...] = jnp.zeros_like(n_ref)
        m_ref[...] = jnp.full_like(m_ref, -jnp.inf)
    _mlstm_chunk(qf, kf, vf, lif, lff, hf_ref, c_ref, n_ref, m_ref, 0, False)
    _mlstm_chunk(qb, kb, vb, lib, lfb, hb_ref, c_ref, n_ref, m_ref, 1, True)


def mlstm_scan(proj, gates, batch, seq, d_model):
    heads, dk = MLSTM_HEADS, MLSTM_DK
    dv = d_model // heads
    L = min(MLSTM_CHUNK, seq)
    nc = seq // L
    t = batch * seq
    k_col0 = heads
    v_col0 = 2 * heads * dk // dv

    def fwd(bh, j):
        return (bh // heads) * nc + j

    def bwd(bh, j):
        return (bh // heads) * nc + (nc - 1 - j)

    def qkv_specs(pos):
        return [pl.BlockSpec((L, dk), lambda bh, j: (pos(bh, j), bh % heads)),
                pl.BlockSpec((L, dk), lambda bh, j: (pos(bh, j), k_col0 + bh % heads)),
                pl.BlockSpec((L, dv), lambda bh, j: (pos(bh, j), v_col0 + bh % heads))]

    def gate_spec(kind, pos):
        return pl.BlockSpec((1, 1, L), lambda bh, j: (kind * heads + bh % heads, 0, pos(bh, j)))

    out_f = pl.BlockSpec((L, dv), lambda bh, j: (fwd(bh, j), bh % heads))
    out_b = pl.BlockSpec((L, dv), lambda bh, j: (bwd(bh, j), bh % heads))
    return pl.pallas_call(
        _mlstm_kernel,
        out_shape=(jax.ShapeDtypeStruct((t, d_model), BF16), jax.ShapeDtypeStruct((t, d_model), BF16)),
        grid=(batch * heads, nc),
        in_specs=(qkv_specs(fwd) + [gate_spec(0, fwd), gate_spec(1, fwd)]
                  + qkv_specs(bwd) + [gate_spec(2, bwd), gate_spec(3, bwd)]),
        out_specs=(out_f, out_b),
        scratch_shapes=[pltpu.VMEM((2, dk, dv), F32), pltpu.VMEM((2, dk, LANES), F32),
                        pltpu.VMEM((2, 1, LANES), F32)],
        compiler_params=_params("arbitrary", "arbitrary"),
        name="mlstm_scan",
    )(proj, proj, proj, gates, gates, proj, proj, proj, gates, gates)


def _na_kernel(q_ref, k0_ref, k1_ref, k2_ref, v0_ref, v1_ref, v2_ref, bias_ref, qg_ref, kg_ref, o_ref):
    hd = NA_HEAD_DIM
    for h in range(q_ref.shape[1] // hd):
        cols = slice(h * hd, (h + 1) * hd)
        q = _rms_rows(q_ref[:, cols].astype(F32), qg_ref[...]) * (hd ** -0.5)
        k = jnp.concatenate([k0_ref[:, cols], k1_ref[:, cols], k2_ref[:, cols]], axis=0)
        k = _rms_rows(k.astype(F32), kg_ref[...])
        v = jnp.concatenate([v0_ref[:, cols], v1_ref[:, cols], v2_ref[:, cols]], axis=0)
        s = _dot_nt(q.astype(BF16), k.astype(BF16)) + bias_ref[0, h]
        p = jnp.exp(s - jnp.max(s, axis=-1, keepdims=True))
        o = _dot(p.astype(BF16), v) * (1.0 / jnp.sum(p, axis=-1, keepdims=True))
        o_ref[:, cols] = o.astype(o_ref.dtype)


def _na_bias_tables(rpb, rows):
    rb, w = NA_ROWS_PER_BLOCK, GRID_W
    nb = rows // rb
    heads, n_dr, n_dc = rpb.shape
    period = 2 * w
    left = (w - 1) - (NA_KW - 1)
    vec = jnp.pad(rpb.astype(F32), ((0, 0), (0, 0), (left, period - left - n_dc)))
    toep = jnp.tile(vec, (1, 1, w))[:, :, :w * (period - 1)].reshape(heads, n_dr, w, period - 1)[..., w - 1:]
    zero = jnp.zeros((heads, w, w), F32)
    tables = []
    for blk in (0, 1, nb - 1):
        ws = min(max(blk - 1, 0), nb - 3) * rb
        r = blk * rb + np.arange(rb)[:, None, None, None]
        c = np.arange(w)[None, :, None, None]
        a = ws + np.arange(3 * rb)[None, None, :, None]
        kc = np.arange(w)[None, None, None, :]
        rs = np.clip(r - NA_KH // 2, 0, rows - NA_KH)
        cs = np.clip(c - NA_KW // 2, 0, w - NA_KW)
        ok = (a >= rs) & (a < rs + NA_KH) & (kc >= cs) & (kc < cs + NA_KW)
        ok = np.broadcast_to(ok, (rb, w, 3 * rb, w)).reshape(rb * w, 3 * rb * w)
        parts = []
        for qr in range(rb):
            drs = [ws + ar - (blk * rb + qr) + NA_KH - 1 for ar in range(3 * rb)]
            parts.append(jnp.stack([toep[:, dr] if 0 <= dr < n_dr else zero for dr in drs], axis=2))
        vals = jnp.stack(parts, axis=1).reshape(heads, rb * w, 3 * rb * w)
        tables.append(jnp.where(ok[None], vals, NEG_BIG))
    return jnp.stack(tables)


def na_attention(qkv, q_norm, k_norm, rpb, batch, seq, d_model):
    rb, w, hd = NA_ROWS_PER_BLOCK, GRID_W, NA_HEAD_DIM
    rows = seq // w
    nb = rows // rb
    tq = rb * w
    hps = NA_HEADS_PER_STEP
    ng = NA_HEADS // hps
    cw = hps * hd
    t = batch * seq
    bias = _na_bias_tables(rpb, rows)

    def kv_spec(part, off):
        return pl.BlockSpec((tq, cw), lambda g, b, i: (b * nb + jnp.clip(i - 1, 0, nb - 3) + off, part * ng + g))

    def bias_type(i):
        return jnp.where(i == 0, 0, jnp.where(i == nb - 1, 2, 1))

    return pl.pallas_call(
        _na_kernel,
        out_shape=jax.ShapeDtypeStruct((t, d_model), BF16),
        grid=(ng, batch, nb),
        in_specs=([pl.BlockSpec((tq, cw), lambda g, b, i: (b * nb + i, g))]
                  + [kv_spec(1, off) for off in range(3)] + [kv_spec(2, off) for off in range(3)]
                  + [pl.BlockSpec((1, hps, tq, 3 * tq), lambda g, b, i: (bias_type(i), g, 0, 0)),
                     pl.BlockSpec((1, hd), lambda g, b, i: (0, 0)),
                     pl.BlockSpec((1, hd), lambda g, b, i: (0, 0))]),
        out_specs=pl.BlockSpec((tq, cw), lambda g, b, i: (b * nb + i, g)),
        compiler_params=_params("arbitrary", "arbitrary", "arbitrary"),
        name="na_attention",
    )(qkv, qkv, qkv, qkv, qkv, qkv, qkv, bias, q_norm.reshape(1, hd), k_norm.reshape(1, hd))


def _router_kernel(x_ref, g_ref, wr_ref, xp_ref, meta_ref, cnt_ref, carry_ref):
    @pl.when(pl.program_id(0) == 0)
    def _():
        carry_ref[...] = jnp.zeros_like(carry_ref)

    tt = x_ref.shape[0]
    xn = _rms_rows(x_ref[...], g_ref[...])
    xh = xn.astype(BF16)
    xhf = xh.astype(F32)
    xl = (xn - xhf).astype(BF16)
    wr = wr_ref[...]
    wh = wr.astype(BF16)
    wl = (wr - wh.astype(F32)).astype(BF16)
    logits = _dot(xh, wh) + (_dot(xh, wl) + _dot(xl, wh))
    col = lax.broadcasted_iota(jnp.int32, logits.shape, 1).astype(F32)
    lg = jnp.where(col < N_EXPERTS, logits, -jnp.inf)
    m1 = jnp.max(lg, axis=-1, keepdims=True)
    i1 = jnp.min(jnp.where(lg == m1, col, float(LANES)), axis=-1, keepdims=True)
    lg2 = jnp.where(col == i1, -jnp.inf, lg)
    m2 = jnp.max(lg2, axis=-1, keepdims=True)
    i2 = jnp.min(jnp.where(lg2 == m2, col, float(LANES)), axis=-1, keepdims=True)
    e2 = jnp.exp(m2 - m1)
    w1 = 1.0 / (1.0 + e2)
    w2 = e2 * w1

    hit1, hit2 = col == i1, col == i2
    cnt = jnp.where(hit1 | hit2, 1.0, 0.0)
    r_i = lax.broadcasted_iota(jnp.int32, (tt, tt), 0)
    c_i = lax.broadcasted_iota(jnp.int32, (tt, tt), 1)
    before = jnp.where(c_i < r_i, 1.0, 0.0).astype(BF16)
    prefix = _dot(before, cnt.astype(BF16)) + carry_ref[...]
    rank1 = jnp.sum(jnp.where(hit1, prefix, 0.0), axis=-1, keepdims=True)
    rank2 = jnp.sum(jnp.where(hit2, prefix, 0.0), axis=-1, keepdims=True)
    carry_ref[...] += jnp.sum(cnt, axis=0, keepdims=True)
    cnt_ref[...] = carry_ref[...]

    meta = jnp.zeros(logits.shape, F32)
    for lane, val in enumerate((i1, i2, w1, w2, rank1, rank2)):
        meta = jnp.where(col == lane, val, meta)
    meta_ref[...] = meta
    half = xhf.shape[1] // 2
    xp_ref[...] = _pack_bf16_pairs(xhf[:, :half], xhf[:, half:])


def moe_router(x, gain, w_router):
    t, d = x.shape
    tt = min(ROUTER_TT, t)
    wr = jnp.zeros((d, LANES), F32).at[:, :N_EXPERTS].set(w_router)
    return pl.pallas_call(
        _router_kernel,
        out_shape=(jax.ShapeDtypeStruct((t, d // 2), U32), jax.ShapeDtypeStruct((t, LANES), F32),
                   jax.ShapeDtypeStruct((1, LANES), F32)),
        grid=(t // tt,),
        in_specs=[pl.BlockSpec((tt, d), lambda i: (i, 0)),
                  pl.BlockSpec((1, d), lambda i: (0, 0)),
                  pl.BlockSpec((d, LANES), lambda i: (0, 0))],
        out_specs=(pl.BlockSpec((tt, d // 2), lambda i: (i, 0)),
                   pl.BlockSpec((tt, LANES), lambda i: (i, 0)),
                   pl.BlockSpec((1, LANES), lambda i: (0, 0))),
        scratch_shapes=[pltpu.VMEM((1, LANES), F32)],
        compiler_params=_params("arbitrary"),
        name="moe_router",
    )(x, gain.reshape(1, d), wr)


def _scatter_kernel(pad_lo_ref, pad_hi_ref, p0_ref, p1_ref, src_ref, dst_ref, sem):
    tt = p0_ref.shape[2]

    def row_copy(src_row, dst_row):
        return pltpu.make_async_copy(src_ref.at[pl.ds(src_row, 1)], dst_ref.at[pl.ds(dst_row, 1)], sem)

    @pl.when(pl.program_id(0) == 0)
    def _():
        for e in range(N_EXPERTS):
            def pad_body(r, carry):
                row_copy(0, r).start()
                return carry
            lax.fori_loop(pad_lo_ref[e], pad_hi_ref[e], pad_body, 0)

            def pad_wait(r, carry):
                row_copy(0, 0).wait()
                return carry
            lax.fori_loop(pad_lo_ref[e], pad_hi_ref[e], pad_wait, 0)

    def start_body(t, carry):
        row_copy(t, p0_ref[0, 0, t]).start()
        row_copy(t, p1_ref[0, 0, t]).start()
        return carry
    lax.fori_loop(0, tt, start_body, 0, unroll=8)

    def wait_body(t, carry):
        row_copy(0, 0).wait()
        row_copy(0, 0).wait()
        return carry
    lax.fori_loop(0, tt, wait_body, 0, unroll=8)


def moe_scatter_rows(xp, pos0, pos1, pad_lo, pad_hi, n_rows):
    t, half = xp.shape
    tt = min(SCATTER_TT, t)
    nblk = t // tt
    smem_rows = pl.BlockSpec((1, 1, tt), lambda i, lo, hi: (i, 0, 0), memory_space=pltpu.SMEM)
    return pl.pallas_call(
        _scatter_kernel,
        out_shape=jax.ShapeDtypeStruct((n_rows, half), xp.dtype),
        grid_spec=pltpu.PrefetchScalarGridSpec(
            num_scalar_prefetch=2,
            grid=(nblk,),
            in_specs=[smem_rows, smem_rows, pl.BlockSpec((tt, half), lambda i, lo, hi: (i, 0))],
            out_specs=pl.BlockSpec(memory_space=pl.ANY),
            scratch_shapes=[pltpu.SemaphoreType.DMA]),
        compiler_params=pltpu.CompilerParams(dimension_semantics=("arbitrary",), has_side_effects=True),
        name="moe_scatter_rows",
    )(pad_lo, pad_hi, pos0.reshape(nblk, 1, tt), pos1.reshape(nblk, 1, tt), xp)


def _combine_kernel(p0_ref, p1_ref, x_ref, meta_ref, y_ref, o_ref, buf_ref, sem):
    tt = x_ref.shape[0]

    def row_copy(k, src_row, t):
        return pltpu.make_async_copy(y_ref.at[pl.ds(src_row, 1)], buf_ref.at[k, pl.ds(t, 1)], sem)

    def start_body(t, carry):
        row_copy(0, p0_ref[0, 0, t], t).start()
        row_copy(1, p1_ref[0, 0, t], t).start()
        return carry
    lax.fori_loop(0, tt, start_body, 0)

    def wait_body(t, carry):
        row_copy(0, 0, 0).wait()
        row_copy(1, 0, 0).wait()
        return carry
    lax.fori_loop(0, tt, wait_body, 0)

    meta = meta_ref[...]
    o_ref[...] = x_ref[...] + (meta[:, 2:3] * buf_ref[0] + meta[:, 3:4] * buf_ref[1])


def moe_combine(x, meta, pos0, pos1, y):
    t, d = x.shape
    tt = min(COMBINE_TT, t)
    nblk = t // tt
    smem_rows = pl.BlockSpec((1, 1, tt), lambda i: (i, 0, 0), memory_space=pltpu.SMEM)
    return pl.pallas_call(
        _combine_kernel,
        out_shape=jax.ShapeDtypeStruct((t, d), F32),
        grid=(nblk,),
        in_specs=[smem_rows, smem_rows,
                  pl.BlockSpec((tt, d), lambda i: (i, 0)),
                  pl.BlockSpec((tt, LANES), lambda i: (i, 0)),
                  pl.BlockSpec(memory_space=pl.ANY)],
        out_specs=pl.BlockSpec((tt, d), lambda i: (i, 0)),
        scratch_shapes=[pltpu.VMEM((2, tt, d), F32), pltpu.SemaphoreType.DMA],
        compiler_params=_params("arbitrary"),
        name="moe_combine",
    )(pos0.reshape(nblk, 1, tt), pos1.reshape(nblk, 1, tt), x, meta, y)


def mlstm_layer(x, batch, seq, mix_norm, w_in, b_gates, head_norm, w_out):
    t, d = x.shape
    n_main = w_in.shape[1] - 4 * MLSTM_HEADS
    proj = norm_matmul(x, mix_norm, w_in[:, :n_main].astype(BF16), BF16)
    gates = mlstm_gates(x, mix_norm, w_in[:, n_main:].T, b_gates)
    hf, hb = mlstm_scan(proj, gates.reshape(4 * MLSTM_HEADS, 1, t), batch, seq, d)
    return mlstm_out_proj(hf, hb, proj, head_norm, w_out.astype(BF16), x)


def na_layer(x, batch, seq, mix_norm, w_qkv, q_norm, k_norm, rpb, w_out):
    d = x.shape[1]
    qkv = norm_matmul(x, mix_norm, w_qkv.astype(BF16), BF16)
    attn = na_attention(qkv, q_norm, k_norm, rpb, batch, seq, d)
    return matmul_residual(attn, w_out.astype(BF16), x)


def moe_layer(x, ffn_norm, w_router, w_gate, w_up, w_down):
    t, d = x.shape
    tm = MOE_TM
    xp, meta, counts = moe_router(x, ffn_norm, w_router)
    counts = counts[0, :N_EXPERTS].astype(jnp.int32)
    tiles = (counts + tm - 1) // tm
    tile_end = jnp.cumsum(tiles)
    row_start = (tile_end - tiles) * tm
    n_tiles = TOP_K * t // tm + N_EXPERTS
    tile_expert = jnp.minimum(jnp.sum(jnp.arange(n_tiles)[:, None] >= tile_end[None, :], axis=1),
                              N_EXPERTS - 1).astype(jnp.int32)
    n_used = tile_end[-1:].astype(jnp.int32)
    e0, e1 = meta[:, 0].astype(jnp.int32), meta[:, 1].astype(jnp.int32)
    pos0 = row_start[e0] + meta[:, 4].astype(jnp.int32)
    pos1 = row_start[e1] + meta[:, 5].astype(jnp.int32)
    pad_hi = (tile_end * tm).at[-1].set(n_tiles * tm)
    xs = moe_scatter_rows(xp, pos0, pos1, (row_start + counts).astype(jnp.int32),
                          pad_hi.astype(jnp.int32), n_tiles * tm)
    y = moe_ffn(tile_expert, n_used, xs, w_gate.astype(BF16), w_up.astype(BF16), w_down.astype(BF16))
    return moe_combine(x, meta, pos0, pos1, y)


def kernel(x, l0_mix_norm, l0_mlstm_w_in, l0_mlstm_b_gates, l0_mlstm_head_norm, l0_mlstm_w_out, l0_ffn_norm, l0_ffn_w_gate, l0_ffn_w_up, l0_ffn_w_down, l1_mix_norm, l1_na_w_qkv, l1_na_q_norm, l1_na_k_norm, l1_na_rpb, l1_na_w_out, l1_ffn_norm, l1_moe_w_router, l1_moe_w_gate, l1_moe_w_up, l1_moe_w_down, l2_mix_norm, l2_mlstm_w_in, l2_mlstm_b_gates, l2_mlstm_head_norm, l2_mlstm_w_out, l2_ffn_norm, l2_ffn_w_gate, l2_ffn_w_up, l2_ffn_w_down, l3_mix_norm, l3_na_w_qkv, l3_na_q_norm, l3_na_k_norm, l3_na_rpb, l3_na_w_out, l3_ffn_norm, l3_moe_w_router, l3_moe_w_gate, l3_moe_w_up, l3_moe_w_down):
    batch, seq, d = x.shape
    h = x.reshape(batch * seq, d)
    h = mlstm_layer(h, batch, seq, l0_mix_norm, l0_mlstm_w_in, l0_mlstm_b_gates, l0_mlstm_head_norm, l0_mlstm_w_out)
    h = dense_ffn(h, l0_ffn_norm, l0_ffn_w_gate.astype(BF16), l0_ffn_w_up.astype(BF16), l0_ffn_w_down.astype(BF16))
    h = na_layer(h, batch, seq, l1_mix_norm, l1_na_w_qkv, l1_na_q_norm, l1_na_k_norm, l1_na_rpb, l1_na_w_out)
    h = moe_layer(h, l1_ffn_norm, l1_moe_w_router, l1_moe_w_gate, l1_moe_w_up, l1_moe_w_down)
    h = mlstm_layer(h, batch, seq, l2_mix_norm, l2_mlstm_w_in, l2_mlstm_b_gates, l2_mlstm_head_norm, l2_mlstm_w_out)
    h = dense_ffn(h, l2_ffn_norm, l2_ffn_w_gate.astype(BF16), l2_ffn_w_up.astype(BF16), l2_ffn_w_down.astype(BF16))
    h = na_layer(h, batch, seq, l3_mix_norm, l3_na_w_qkv, l3_na_q_norm, l3_na_k_norm, l3_na_rpb, l3_na_w_out)
    h = moe_layer(h, l3_ffn_norm, l3_moe_w_router, l3_moe_w_gate, l3_moe_w_up, l3_moe_w_down)
    return h.reshape(batch, seq, d)
```

```python
import functools

import jax
import jax.numpy as jnp
import numpy as np
from jax import lax
from jax.experimental import pallas as pl
from jax.experimental.pallas import tpu as pltpu

F32 = jnp.float32
BF16 = jnp.bfloat16
U32 = jnp.uint32

NORM_EPS = 1e-6
LANES = 128
VMEM_LIMIT_BYTES = 56 << 20
NEG_BIG = -1e30

MLSTM_HEADS = 4
MLSTM_DK = 256
GATE_SOFTCAP = 15.0
MLSTM_CHUNK = 256
GRID_W = 64
NA_HEADS = 16
NA_HEAD_DIM = 128
NA_KH = 8
NA_KW = 16
NA_ROWS_PER_BLOCK = 4
NA_HEADS_PER_STEP = 4
N_EXPERTS = 8
TOP_K = 2

MM_TM, MM_TN = 1024, 1024
MLSTM_OUT_TM = 512
FFN_TM, FFN_TF = 1024, 512
MOE_TM, MOE_TF = 1024, 512
ROUTER_TT = 512
SCATTER_TT = 1024
COMBINE_TT = 256
NORM_ROWS = 128


def _params(*semantics):
    return pltpu.CompilerParams(dimension_semantics=semantics, vmem_limit_bytes=VMEM_LIMIT_BYTES)


def _dot(a, b):
    return jnp.dot(a, b, preferred_element_type=F32)


def _dot_nt(a, b):
    return lax.dot_general(a, b, (((1,), (1,)), ((), ())), preferred_element_type=F32)


def _split3(x):
    hi = x.astype(BF16)
    r1 = x - hi.astype(F32)
    mid = r1.astype(BF16)
    lo = (r1 - mid.astype(F32)).astype(BF16)
    return hi, mid, lo


def _rms_rows(x, gain):
    ms = jnp.mean(x * x, axis=-1, keepdims=True)
    return x * lax.rsqrt(ms + NORM_EPS) * gain


def _fill_rmsnorm(x_ref, g_ref, a_ref):
    def body(c, carry):
        r = pl.multiple_of(c * NORM_ROWS, NORM_ROWS)
        a_ref[pl.ds(r, NORM_ROWS), :] = _rms_rows(x_ref[pl.ds(r, NORM_ROWS), :], g_ref[...]).astype(BF16)
        return carry
    lax.fori_loop(0, x_ref.shape[0] // NORM_ROWS, body, 0)


def _norm_matmul_kernel(x_ref, g_ref, w_ref, o_ref, a_ref):
    @pl.when(pl.program_id(1) == 0)
    def _():
        _fill_rmsnorm(x_ref, g_ref, a_ref)
    o_ref[...] = _dot(a_ref[...], w_ref[...]).astype(o_ref.dtype)


def norm_matmul(x, gain, w, out_dtype):
    m, k = x.shape
    n = w.shape[1]
    tm, tn = min(MM_TM, m), min(MM_TN, n)
    return pl.pallas_call(
        _norm_matmul_kernel,
        out_shape=jax.ShapeDtypeStruct((m, n), out_dtype),
        grid=(m // tm, n // tn),
        in_specs=[pl.BlockSpec((tm, k), lambda i, j: (i, 0)),
                  pl.BlockSpec((1, k), lambda i, j: (0, 0)),
                  pl.BlockSpec((k, tn), lambda i, j: (0, j))],
        out_specs=pl.BlockSpec((tm, tn), lambda i, j: (i, j)),
        scratch_shapes=[pltpu.VMEM((tm, k), BF16)],
        compiler_params=_params("arbitrary", "arbitrary"),
        name="norm_matmul",
    )(x, gain.reshape(1, k), w)


def _matmul_res_kernel(a_ref, w_ref, r_ref, o_ref):
    o_ref[...] = r_ref[...] + _dot(a_ref[...], w_ref[...])


def matmul_residual(a, w, res):
    m, k = a.shape
    n = w.shape[1]
    tm, tn = min(MM_TM, m), min(MM_TN, n)
    return pl.pallas_call(
        _matmul_res_kernel,
        out_shape=jax.ShapeDtypeStruct((m, n), F32),
        grid=(m // tm, n // tn),
        in_specs=[pl.BlockSpec((tm, k), lambda i, j: (i, 0)),
                  pl.BlockSpec((k, tn), lambda i, j: (0, j)),
                  pl.BlockSpec((tm, tn), lambda i, j: (i, j))],
        out_specs=pl.BlockSpec((tm, tn), lambda i, j: (i, j)),
        compiler_params=_params("arbitrary", "arbitrary"),
        name="matmul_residual",
    )(a, w, res)


def _mlstm_out_kernel(hf_ref, hb_ref, og_ref, g_ref, w_ref, r_ref, o_ref, a_ref, *, heads):
    @pl.when(pl.program_id(1) == 0)
    def _():
        dv = a_ref.shape[1] // heads

        def body(c, carry):
            r = pl.multiple_of(c * NORM_ROWS, NORM_ROWS)
            rows = pl.ds(r, NORM_ROWS)
            for h in range(heads):
                cols = slice(h * dv, (h + 1) * dv)
                hh = hf_ref[rows, cols].astype(F32) + hb_ref[rows, cols].astype(F32)
                y = _rms_rows(hh, g_ref[:, cols]) * jax.nn.sigmoid(og_ref[rows, cols].astype(F32))
                a_ref[rows, cols] = y.astype(BF16)
            return carry
        lax.fori_loop(0, a_ref.shape[0] // NORM_ROWS, body, 0)
    o_ref[...] = r_ref[...] + _dot(a_ref[...], w_ref[...])


def mlstm_out_proj(hf, hb, proj, head_norm, w, res):
    m, d = hf.shape
    n = w.shape[1]
    tm, tn = min(MLSTM_OUT_TM, m), min(MM_TN, n)
    o_col_block = (proj.shape[1] - d) // d
    return pl.pallas_call(
        functools.partial(_mlstm_out_kernel, heads=MLSTM_HEADS),
        out_shape=jax.ShapeDtypeStruct((m, n), F32),
        grid=(m // tm, n // tn),
        in_specs=[pl.BlockSpec((tm, d), lambda i, j: (i, 0)),
                  pl.BlockSpec((tm, d), lambda i, j: (i, 0)),
                  pl.BlockSpec((tm, d), lambda i, j: (i, o_col_block)),
                  pl.BlockSpec((1, d), lambda i, j: (0, 0)),
                  pl.BlockSpec((d, tn), lambda i, j: (0, j)),
                  pl.BlockSpec((tm, tn), lambda i, j: (i, j))],
        out_specs=pl.BlockSpec((tm, tn), lambda i, j: (i, j)),
        scratch_shapes=[pltpu.VMEM((tm, d), BF16)],
        compiler_params=_params("arbitrary", "arbitrary"),
        name="mlstm_out_proj",
    )(hf, hb, proj, head_norm.reshape(1, d), w, res)


def _swiglu_step(a, wg_ref, wu_ref, wd_ref):
    gate = _dot(a, wg_ref[...].astype(BF16))
    up = _dot(a, wu_ref[...].astype(BF16))
    hidden = (gate * jax.nn.sigmoid(gate) * up).astype(BF16)
    return _dot(hidden, wd_ref[...].astype(BF16))


def _ffn_kernel(x_ref, g_ref, wg_ref, wu_ref, wd_ref, o_ref, a_ref):
    @pl.when(pl.program_id(1) == 0)
    def _():
        _fill_rmsnorm(x_ref, g_ref, a_ref)
        o_ref[...] = x_ref[...]
    o_ref[...] += _swiglu_step(a_ref[...], wg_ref, wu_ref, wd_ref)


def dense_ffn(x, gain, wg, wu, wd):
    m, d = x.shape
    f = wg.shape[1]
    tm, tf = min(FFN_TM, m), min(FFN_TF, f)
    return pl.pallas_call(
        _ffn_kernel,
        out_shape=jax.ShapeDtypeStruct((m, d), F32),
        grid=(m // tm, f // tf),
        in_specs=[pl.BlockSpec((tm, d), lambda i, j: (i, 0), pipeline_mode=pl.Buffered(1)),
                  pl.BlockSpec((1, d), lambda i, j: (0, 0)),
                  pl.BlockSpec((d, tf), lambda i, j: (0, j)),
                  pl.BlockSpec((d, tf), lambda i, j: (0, j)),
                  pl.BlockSpec((tf, d), lambda i, j: (j, 0))],
        out_specs=pl.BlockSpec((tm, d), lambda i, j: (i, 0), pipeline_mode=pl.Buffered(1)),
        scratch_shapes=[pltpu.VMEM((tm, d), BF16)],
        compiler_params=_params("arbitrary", "arbitrary"),
        name="dense_ffn",
    )(x, gain.reshape(1, d), wg, wu, wd)


def _unpack_bf16_pairs(words):
    hi = lax.bitcast_convert_type(words & jnp.uint32(0xFFFF0000), F32).astype(BF16)
    lo = lax.bitcast_convert_type(words << 16, F32).astype(BF16)
    return hi, lo


def _pack_bf16_pairs(hi, lo):
    return lax.bitcast_convert_type(hi, U32) | (lax.bitcast_convert_type(lo, U32) >> 16)


def _moe_ffn_kernel(te_ref, nu_ref, xs_ref, wg_ref, wu_ref, wd_ref, o_ref, a_ref):
    i, j = pl.program_id(0), pl.program_id(1)

    @pl.when(i < nu_ref[0])
    def _():
        @pl.when(j == 0)
        def _():
            half = a_ref.shape[1] // 2

            def body(c, carry):
                r = pl.multiple_of(c * NORM_ROWS, NORM_ROWS)
                hi, lo = _unpack_bf16_pairs(xs_ref[pl.ds(r, NORM_ROWS), :])
                a_ref[pl.ds(r, NORM_ROWS), :half] = hi
                a_ref[pl.ds(r, NORM_ROWS), half:] = lo
                return carry
            lax.fori_loop(0, a_ref.shape[0] // NORM_ROWS, body, 0)
            o_ref[...] = jnp.zeros_like(o_ref)
        o_ref[...] += _swiglu_step(a_ref[...], wg_ref, wu_ref, wd_ref)

    @pl.when((i >= nu_ref[0]) & (j == 0))
    def _():
        o_ref[...] = jnp.zeros_like(o_ref)


def moe_ffn(tile_expert, n_used, xs, wg, wu, wd):
    p, half = xs.shape
    d = 2 * half
    f = wg.shape[2]
    tm, tf = MOE_TM, min(MOE_TF, f)
    nf = f // tf

    def row_map(i, j, te, nu):
        return (jnp.minimum(i, nu[0] - 1), 0)

    def f_idx(i, j, nu):
        return jnp.where(i < nu[0], j, nf - 1)

    def te_idx(i, te, nu):
        return te[jnp.minimum(i, nu[0] - 1)]

    return pl.pallas_call(
        _moe_ffn_kernel,
        out_shape=jax.ShapeDtypeStruct((p, d), F32),
        grid_spec=pltpu.PrefetchScalarGridSpec(
            num_scalar_prefetch=2,
            grid=(p // tm, nf),
            in_specs=[pl.BlockSpec((tm, half), row_map, pipeline_mode=pl.Buffered(1)),
                      pl.BlockSpec((None, d, tf), lambda i, j, te, nu: (te_idx(i, te, nu), 0, f_idx(i, j, nu))),
                      pl.BlockSpec((None, d, tf), lambda i, j, te, nu: (te_idx(i, te, nu), 0, f_idx(i, j, nu))),
                      pl.BlockSpec((None, tf, d), lambda i, j, te, nu: (te_idx(i, te, nu), f_idx(i, j, nu), 0))],
            out_specs=pl.BlockSpec((tm, d), lambda i, j, te, nu: (i, 0), pipeline_mode=pl.Buffered(1)),
            scratch_shapes=[pltpu.VMEM((tm, d), BF16)]),
        compiler_params=_params("arbitrary", "arbitrary"),
        name="moe_ffn",
    )(tile_expert, n_used, xs, wg, wu, wd)


def _mlstm_gates_kernel(x_ref, g_ref, wt_ref, b_ref, o_ref):
    xn = _rms_rows(x_ref[...], g_ref[...])
    xh = xn.astype(BF16)
    xl = (xn - xh.astype(F32)).astype(BF16)
    w = wt_ref[...]
    wh = w.astype(BF16)
    wl = (w - wh.astype(F32)).astype(BF16)
    pre = _dot_nt(wh, xh) + (_dot_nt(wh, xl) + _dot_nt(wl, xh)) + b_ref[...]
    g = GATE_SOFTCAP * jnp.tanh(pre / GATE_SOFTCAP)
    log_sig = jnp.minimum(g, 0.0) - jnp.log1p(jnp.exp(-jnp.abs(g)))
    row = lax.broadcasted_iota(jnp.int32, g.shape, 0)
    is_forget = ((row >= MLSTM_HEADS) & (row < 2 * MLSTM_HEADS)) | (row >= 3 * MLSTM_HEADS)
    o_ref[...] = jnp.where(is_forget, log_sig, g)


def mlstm_gates(x, gain, w_gates_t, b_gates):
    t, d = x.shape
    ng = w_gates_t.shape[0]
    tt = min(ROUTER_TT, t)
    return pl.pallas_call(
        _mlstm_gates_kernel,
        out_shape=jax.ShapeDtypeStruct((ng, t), F32),
        grid=(t // tt,),
        in_specs=[pl.BlockSpec((tt, d), lambda i: (i, 0)),
                  pl.BlockSpec((1, d), lambda i: (0, 0)),
                  pl.BlockSpec((ng, d), lambda i: (0, 0)),
                  pl.BlockSpec((ng, 1), lambda i: (0, 0))],
        out_specs=pl.BlockSpec((ng, tt), lambda i: (0, i)),
        compiler_params=_params("arbitrary"),
        name="mlstm_gates",
    )(x, gain.reshape(1, d), w_gates_t, b_gates.reshape(ng, 1))


def _mlstm_chunk(q_ref, k_ref, v_ref, li_ref, lf_ref, h_ref, c_ref, n_ref, m_ref, d, reverse):
    L = q_ref.shape[0]
    q = q_ref[...] * jnp.asarray(MLSTM_DK ** -0.5, BF16)
    k = k_ref[...]
    v = v_ref[...]
    li = li_ref[0]
    lf = lf_ref[0]
    r_i = lax.broadcasted_iota(jnp.int32, (L, L), 0)
    c_i = lax.broadcasted_iota(jnp.int32, (L, L), 1)
    visible = (c_i >= r_i) if reverse else (c_i <= r_i)
    cum = (r_i >= c_i) if reverse else (r_i <= c_i)

    hi, mid, lo = (p.astype(F32) for p in _split3(lf))
    prow = lax.broadcasted_iota(jnp.int32, (16, L), 0)
    pieces = jnp.where(prow == 0, hi, jnp.where(prow == 1, mid, jnp.where(prow == 2, lo, 0.0)))
    b = jnp.sum(_dot(pieces.astype(BF16), jnp.where(cum, 1.0, 0.0).astype(BF16)), axis=0, keepdims=True)
    g = jnp.sum(lf, axis=-1, keepdims=True)
    u = li - b

    m_prev = m_ref[d][:, :1]
    a_mat = jnp.where(visible, jnp.broadcast_to(u, (L, L)), -jnp.inf)
    m_row = jnp.maximum(jnp.max(a_mat, axis=-1, keepdims=True), m_prev)
    w_intra = (jnp.exp(a_mat - m_row) * _dot_nt(q, k)).astype(BF16)
    w_inter = jnp.exp(m_prev - m_row)

    ones = jnp.ones((L, LANES), BF16)
    num = w_inter * _dot(q, c_ref[d].astype(BF16)) + _dot(w_intra, v)
    den = w_inter * _dot(q, n_ref[d].astype(BF16)) + _dot(w_intra, ones)
    b_col = jnp.sum(jnp.where(r_i == c_i, jnp.broadcast_to(b, (L, L)), 0.0), axis=-1, keepdims=True)
    inv = 1.0 / jnp.maximum(jnp.abs(den), jnp.exp(-(b_col + m_row)))
    h_ref[...] = (num * jnp.concatenate([inv] * (num.shape[1] // LANES), axis=1)).astype(h_ref.dtype)

    m_new = g + jnp.maximum(m_prev, jnp.max(u, axis=-1, keepdims=True))
    decay = jnp.exp(g + m_prev - m_new)
    ks_t = (k.astype(F32).T * jnp.exp(g + u - m_new)).astype(BF16)
    c_ref[d] = decay * c_ref[d] + _dot(ks_t, v)
    n_ref[d] = decay * n_ref[d] + _dot(ks_t, ones)
    m_ref[d] = jnp.broadcast_to(m_new, (1, LANES))


def _mlstm_kernel(qf, kf, vf, lif, lff, qb, kb, vb, lib, lfb, hf_ref, hb_ref, c_ref, n_ref, m_ref):
    @pl.when(pl.program_id(1) == 0)
    def _():
        c_ref[...] = jnp.zeros_like(c_ref)
        n_ref[...] = jnp.zeros_like(n_ref)
        m_ref[...] = jnp.full_like(m_ref, -jnp.inf)
    _mlstm_chunk(qf, kf, vf, lif, lff, hf_ref, c_ref, n_ref, m_ref, 0, False)
    _mlstm_chunk(qb, kb, vb, lib, lfb, hb_ref, c_ref, n_ref, m_ref, 1, True)


def mlstm_scan(proj, gates, batch, seq, d_model):
    heads, dk = MLSTM_HEADS, MLSTM_DK
    dv = d_model // heads
    L = min(MLSTM_CHUNK, seq)
    nc = seq // L
    t = batch * seq
    k_col0 = heads
    v_col0 = 2 * heads * dk // dv

    def fwd(bh, j):
        return (bh // heads) * nc + j

    def bwd(bh, j):
        return (bh // heads) * nc + (nc - 1 - j)

    def qkv_specs(pos):
        return [pl.BlockSpec((L, dk), lambda bh, j: (pos(bh, j), bh % heads)),
                pl.BlockSpec((L, dk), lambda bh, j: (pos(bh, j), k_col0 + bh % heads)),
                pl.BlockSpec((L, dv), lambda bh, j: (pos(bh, j), v_col0 + bh % heads))]

    def gate_spec(kind, pos):
        return pl.BlockSpec((1, 1, L), lambda bh, j: (kind * heads + bh % heads, 0, pos(bh, j)))

    out_f = pl.BlockSpec((L, dv), lambda bh, j: (fwd(bh, j), bh % heads))
    out_b = pl.BlockSpec((L, dv), lambda bh, j: (bwd(bh, j), bh % heads))
    return pl.pallas_call(
        _mlstm_kernel,
        out_shape=(jax.ShapeDtypeStruct((t, d_model), BF16), jax.ShapeDtypeStruct((t, d_model), BF16)),
        grid=(batch * heads, nc),
        in_specs=(qkv_specs(fwd) + [gate_spec(0, fwd), gate_spec(1, fwd)]
                  + qkv_specs(bwd) + [gate_spec(2, bwd), gate_spec(3, bwd)]),
        out_specs=(out_f, out_b),
        scratch_shapes=[pltpu.VMEM((2, dk, dv), F32), pltpu.VMEM((2, dk, LANES), F32),
                        pltpu.VMEM((2, 1, LANES), F32)],
        compiler_params=_params("arbitrary", "arbitrary"),
        name="mlstm_scan",
    )(proj, proj, proj, gates, gates, proj, proj, proj, gates, gates)


def _na_kernel(q_ref, k0_ref, k1_ref, k2_ref, v0_ref, v1_ref, v2_ref, bias_ref, qg_ref, kg_ref, o_ref):
    hd = NA_HEAD_DIM
    for h in range(q_ref.shape[1] // hd):
        cols = slice(h * hd, (h + 1) * hd)
        q = _rms_rows(q_ref[:, cols].astype(F32), qg_ref[...]) * (hd ** -0.5)
        k = jnp.concatenate([k0_ref[:, cols], k1_ref[:, cols], k2_ref[:, cols]], axis=0)
        k = _rms_rows(k.astype(F32), kg_ref[...])
        v = jnp.concatenate([v0_ref[:, cols], v1_ref[:, cols], v2_ref[:, cols]], axis=0)
        s = _dot_nt(q.astype(BF16), k.astype(BF16)) + bias_ref[0, h]
        p = jnp.exp(s - jnp.max(s, axis=-1, keepdims=True))
        o = _dot(p.astype(BF16), v) * (1.0 / jnp.sum(p, axis=-1, keepdims=True))
        o_ref[:, cols] = o.astype(o_ref.dtype)


def _na_bias_tables(rpb, rows):
    rb, w = NA_ROWS_PER_BLOCK, GRID_W
    nb = rows // rb
    heads, n_dr, n_dc = rpb.shape
    period = 2 * w
    left = (w - 1) - (NA_KW - 1)
    vec = jnp.pad(rpb.astype(F32), ((0, 0), (0, 0), (left, period - left - n_dc)))
    toep = jnp.tile(vec, (1, 1, w))[:, :, :w * (period - 1)].reshape(heads, n_dr, w, period - 1)[..., w - 1:]
    zero = jnp.zeros((heads, w, w), F32)
    tables = []
    for blk in (0, 1, nb - 1):
        ws = min(max(blk - 1, 0), nb - 3) * rb
        r = blk * rb + np.arange(rb)[:, None, None, None]
        c = np.arange(w)[None, :, None, None]
        a = ws + np.arange(3 * rb)[None, None, :, None]
        kc = np.arange(w)[None, None, None, :]
        rs = np.clip(r - NA_KH // 2, 0, rows - NA_KH)
        cs = np.clip(c - NA_KW // 2, 0, w - NA_KW)
        ok = (a >= rs) & (a < rs + NA_KH) & (kc >= cs) & (kc < cs + NA_KW)
        ok = np.broadcast_to(ok, (rb, w, 3 * rb, w)).reshape(rb * w, 3 * rb * w)
        parts = []
        for qr in range(rb):
            drs = [ws + ar - (blk * rb + qr) + NA_KH - 1 for ar in range(3 * rb)]
            parts.append(jnp.stack([toep[:, dr] if 0 <= dr < n_dr else zero for dr in drs], axis=2))
        vals = jnp.stack(parts, axis=1).reshape(heads, rb * w, 3 * rb * w)
        tables.append(jnp.where(ok[None], vals, NEG_BIG))
    return jnp.stack(tables)


def na_attention(qkv, q_norm, k_norm, rpb, batch, seq, d_model):
    rb, w, hd = NA_ROWS_PER_BLOCK, GRID_W, NA_HEAD_DIM
    rows = seq // w
    nb = rows // rb
    tq = rb * w
    hps = NA_HEADS_PER_STEP
    ng = NA_HEADS // hps
    cw = hps * hd
    t = batch * seq
    bias = _na_bias_tables(rpb, rows)

    def kv_spec(part, off):
        return pl.BlockSpec((tq, cw), lambda g, b, i: (b * nb + jnp.clip(i - 1, 0, nb - 3) + off, part * ng + g))

    def bias_type(i):
        return jnp.where(i == 0, 0, jnp.where(i == nb - 1, 2, 1))

    return pl.pallas_call(
        _na_kernel,
        out_shape=jax.ShapeDtypeStruct((t, d_model), BF16),
        grid=(ng, batch, nb),
        in_specs=([pl.BlockSpec((tq, cw), lambda g, b, i: (b * nb + i, g))]
                  + [kv_spec(1, off) for off in range(3)] + [kv_spec(2, off) for off in range(3)]
                  + [pl.BlockSpec((1, hps, tq, 3 * tq), lambda g, b, i: (bias_type(i), g, 0, 0)),
                     pl.BlockSpec((1, hd), lambda g, b, i: (0, 0)),
                     pl.BlockSpec((1, hd), lambda g, b, i: (0, 0))]),
        out_specs=pl.BlockSpec((tq, cw), lambda g, b, i: (b * nb + i, g)),
        compiler_params=_params("arbitrary", "arbitrary", "arbitrary"),
        name="na_attention",
    )(qkv, qkv, qkv, qkv, qkv, qkv, qkv, bias, q_norm.reshape(1, hd), k_norm.reshape(1, hd))


def _router_kernel(x_ref, g_ref, wr_ref, xp_ref, meta_ref, cnt_ref, carry_ref):
    @pl.when(pl.program_id(0) == 0)
    def _():
        carry_ref[...] = jnp.zeros_like(carry_ref)

    tt = x_ref.shape[0]
    xn = _rms_rows(x_ref[...], g_ref[...])
    xh = xn.astype(BF16)
    xhf = xh.astype(F32)
    xl = (xn - xhf).astype(BF16)
    wr = wr_ref[...]
    wh = wr.astype(BF16)
    wl = (wr - wh.astype(F32)).astype(BF16)
    logits = _dot(xh, wh) + (_dot(xh, wl) + _dot(xl, wh))
    col = lax.broadcasted_iota(jnp.int32, logits.shape, 1).astype(F32)
    lg = jnp.where(col < N_EXPERTS, logits, -jnp.inf)
    m1 = jnp.max(lg, axis=-1, keepdims=True)
    i1 = jnp.min(jnp.where(lg == m1, col, float(LANES)), axis=-1, keepdims=True)
    lg2 = jnp.where(col == i1, -jnp.inf, lg)
    m2 = jnp.max(lg2, axis=-1, keepdims=True)
    i2 = jnp.min(jnp.where(lg2 == m2, col, float(LANES)), axis=-1, keepdims=True)
    e2 = jnp.exp(m2 - m1)
    w1 = 1.0 / (1.0 + e2)
    w2 = e2 * w1

    hit1, hit2 = col == i1, col == i2
    cnt = jnp.where(hit1 | hit2, 1.0, 0.0)
    r_i = lax.broadcasted_iota(jnp.int32, (tt, tt), 0)
    c_i = lax.broadcasted_iota(jnp.int32, (tt, tt), 1)
    before = jnp.where(c_i < r_i, 1.0, 0.0).astype(BF16)
    prefix = _dot(before, cnt.astype(BF16)) + carry_ref[...]
    rank1 = jnp.sum(jnp.where(hit1, prefix, 0.0), axis=-1, keepdims=True)
    rank2 = jnp.sum(jnp.where(hit2, prefix, 0.0), axis=-1, keepdims=True)
    carry_ref[...] += jnp.sum(cnt, axis=0, keepdims=True)
    cnt_ref[...] = carry_ref[...]

    meta = jnp.zeros(logits.shape, F32)
    for lane, val in enumerate((i1, i2, w1, w2, rank1, rank2)):
        meta = jnp.where(col == lane, val, meta)
    meta_ref[...] = meta
    half = xhf.shape[1] // 2
    xp_ref[...] = _pack_bf16_pairs(xhf[:, :half], xhf[:, half:])


def moe_router(x, gain, w_router):
    t, d = x.shape
    tt = min(ROUTER_TT, t)
    wr = jnp.zeros((d, LANES), F32).at[:, :N_EXPERTS].set(w_router)
    return pl.pallas_call(
        _router_kernel,
        out_shape=(jax.ShapeDtypeStruct((t, d // 2), U32), jax.ShapeDtypeStruct((t, LANES), F32),
                   jax.ShapeDtypeStruct((1, LANES), F32)),
        grid=(t // tt,),
        in_specs=[pl.BlockSpec((tt, d), lambda i: (i, 0)),
                  pl.BlockSpec((1, d), lambda i: (0, 0)),
                  pl.BlockSpec((d, LANES), lambda i: (0, 0))],
        out_specs=(pl.BlockSpec((tt, d // 2), lambda i: (i, 0)),
                   pl.BlockSpec((tt, LANES), lambda i: (i, 0)),
                   pl.BlockSpec((1, LANES), lambda i: (0, 0))),
        scratch_shapes=[pltpu.VMEM((1, LANES), F32)],
        compiler_params=_params("arbitrary"),
        name="moe_router",
    )(x, gain.reshape(1, d), wr)


def _scatter_kernel(pad_lo_ref, pad_hi_ref, p0_ref, p1_ref, src_ref, dst_ref, sem):
    tt = p0_ref.shape[2]

    def row_copy(src_row, dst_row):
        return pltpu.make_async_copy(src_ref.at[pl.ds(src_row, 1)], dst_ref.at[pl.ds(dst_row, 1)], sem)

    @pl.when(pl.program_id(0) == 0)
    def _():
        for e in range(N_EXPERTS):
            def pad_body(r, carry):
                row_copy(0, r).start()
                return carry
            lax.fori_loop(pad_lo_ref[e], pad_hi_ref[e], pad_body, 0)

            def pad_wait(r, carry):
                row_copy(0, 0).wait()
                return carry
            lax.fori_loop(pad_lo_ref[e], pad_hi_ref[e], pad_wait, 0)

    def start_body(t, carry):
        row_copy(t, p0_ref[0, 0, t]).start(priority=0)
        row_copy(t, p1_ref[0, 0, t]).start(priority=1)
        return carry
    lax.fori_loop(0, tt, start_body, 0, unroll=8)

    def wait_body(t, carry):
        row_copy(0, 0).wait()
        row_copy(0, 0).wait()
        return carry
    lax.fori_loop(0, tt, wait_body, 0, unroll=8)


def moe_scatter_rows(xp, pos0, pos1, pad_lo, pad_hi, n_rows):
    t, half = xp.shape
    tt = min(SCATTER_TT, t)
    nblk = t // tt
    smem_rows = pl.BlockSpec((1, 1, tt), lambda i, lo, hi: (i, 0, 0), memory_space=pltpu.SMEM)
    return pl.pallas_call(
        _scatter_kernel,
        out_shape=jax.ShapeDtypeStruct((n_rows, half), xp.dtype),
        grid_spec=pltpu.PrefetchScalarGridSpec(
            num_scalar_prefetch=2,
            grid=(nblk,),
            in_specs=[smem_rows, smem_rows, pl.BlockSpec((tt, half), lambda i, lo, hi: (i, 0))],
            out_specs=pl.BlockSpec(memory_space=pl.ANY),
            scratch_shapes=[pltpu.SemaphoreType.DMA]),
        compiler_params=pltpu.CompilerParams(dimension_semantics=("arbitrary",), has_side_effects=True),
        name="moe_scatter_rows",
    )(pad_lo, pad_hi, pos0.reshape(nblk, 1, tt), pos1.reshape(nblk, 1, tt), xp)


def _combine_kernel(p0_ref, p1_ref, x_ref, meta_ref, y_ref, o_ref, buf_ref, sem):
    tt = x_ref.shape[0]

    def row_copy(k, src_row, t):
        return pltpu.make_async_copy(y_ref.at[pl.ds(src_row, 1)], buf_ref.at[k, pl.ds(t, 1)], sem)

    def start_body(t, carry):
        row_copy(0, p0_ref[0, 0, t], t).start(priority=0)
        row_copy(1, p1_ref[0, 0, t], t).start(priority=1)
        return carry
    lax.fori_loop(0, tt, start_body, 0, unroll=8)

    def wait_body(t, carry):
        row_copy(0, 0, 0).wait()
        row_copy(1, 0, 0).wait()
        return carry
    lax.fori_loop(0, tt, wait_body, 0, unroll=8)

    meta = meta_ref[...]
    o_ref[...] = x_ref[...] + (meta[:, 2:3] * buf_ref[0] + meta[:, 3:4] * buf_ref[1])


def moe_combine(x, meta, pos0, pos1, y):
    t, d = x.shape
    tt = min(COMBINE_TT, t)
    nblk = t // tt
    smem_rows = pl.BlockSpec((1, 1, tt), lambda i: (i, 0, 0), memory_space=pltpu.SMEM)
    return pl.pallas_call(
        _combine_kernel,
        out_shape=jax.ShapeDtypeStruct((t, d), F32),
        grid=(nblk,),
        in_specs=[smem_rows, smem_rows,
                  pl.BlockSpec((tt, d), lambda i: (i, 0)),
                  pl.BlockSpec((tt, LANES), lambda i: (i, 0)),
                  pl.BlockSpec(memory_space=pl.ANY)],
        out_specs=pl.BlockSpec((tt, d), lambda i: (i, 0)),
        scratch_shapes=[pltpu.VMEM((2, tt, d), F32), pltpu.SemaphoreType.DMA],
        compiler_params=_params("arbitrary"),
        name="moe_combine",
    )(pos0.reshape(nblk, 1, tt), pos1.reshape(nblk, 1, tt), x, meta, y)


def mlstm_layer(x, batch, seq, mix_norm, w_in, b_gates, head_norm, w_out):
    t, d = x.shape
    n_main = w_in.shape[1] - 4 * MLSTM_HEADS
    proj = norm_matmul(x, mix_norm, w_in[:, :n_main].astype(BF16), BF16)
    gates = mlstm_gates(x, mix_norm, w_in[:, n_main:].T, b_gates)
    hf, hb = mlstm_scan(proj, gates.reshape(4 * MLSTM_HEADS, 1, t), batch, seq, d)
    return mlstm_out_proj(hf, hb, proj, head_norm, w_out.astype(BF16), x)


def na_layer(x, batch, seq, mix_norm, w_qkv, q_norm, k_norm, rpb, w_out):
    d = x.shape[1]
    qkv = norm_matmul(x, mix_norm, w_qkv.astype(BF16), BF16)
    attn = na_attention(qkv, q_norm, k_norm, rpb, batch, seq, d)
    return matmul_residual(attn, w_out.astype(BF16), x)


def moe_layer(x, ffn_norm, w_router, w_gate, w_up, w_down):
    t, d = x.shape
    tm = MOE_TM
    xp, meta, counts = moe_router(x, ffn_norm, w_router)
    counts = counts[0, :N_EXPERTS].astype(jnp.int32)
    tiles = (counts + tm - 1) // tm
    tile_end = jnp.cumsum(tiles)
    row_start = (tile_end - tiles) * tm
    n_tiles = TOP_K * t // tm + N_EXPERTS
    tile_expert = jnp.minimum(jnp.sum(jnp.arange(n_tiles)[:, None] >= tile_end[None, :], axis=1),
                              N_EXPERTS - 1).astype(jnp.int32)
    n_used = tile_end[-1:].astype(jnp.int32)
    e0, e1 = meta[:, 0].astype(jnp.int32), meta[:, 1].astype(jnp.int32)
    pos0 = row_start[e0] + meta[:, 4].astype(jnp.int32)
    pos1 = row_start[e1] + meta[:, 5].astype(jnp.int32)
    pad_hi = (tile_end * tm).at[-1].set(n_tiles * tm)
    xs = moe_scatter_rows(xp, pos0, pos1, (row_start + counts).astype(jnp.int32),
                          pad_hi.astype(jnp.int32), n_tiles * tm)
    y = moe_ffn(tile_expert, n_used, xs, w_gate, w_up, w_down)
    return moe_combine(x, meta, pos0, pos1, y)


def kernel(x, l0_mix_norm, l0_mlstm_w_in, l0_mlstm_b_gates, l0_mlstm_head_norm, l0_mlstm_w_out, l0_ffn_norm, l0_ffn_w_gate, l0_ffn_w_up, l0_ffn_w_down, l1_mix_norm, l1_na_w_qkv, l1_na_q_norm, l1_na_k_norm, l1_na_rpb, l1_na_w_out, l1_ffn_norm, l1_moe_w_router, l1_moe_w_gate, l1_moe_w_up, l1_moe_w_down, l2_mix_norm, l2_mlstm_w_in, l2_mlstm_b_gates, l2_mlstm_head_norm, l2_mlstm_w_out, l2_ffn_norm, l2_ffn_w_gate, l2_ffn_w_up, l2_ffn_w_down, l3_mix_norm, l3_na_w_qkv, l3_na_q_norm, l3_na_k_norm, l3_na_rpb, l3_na_w_out, l3_ffn_norm, l3_moe_w_router, l3_moe_w_gate, l3_moe_w_up, l3_moe_w_down):
    batch, seq, d = x.shape
    h = x.reshape(batch * seq, d)
    h = mlstm_layer(h, batch, seq, l0_mix_norm, l0_mlstm_w_in, l0_mlstm_b_gates, l0_mlstm_head_norm, l0_mlstm_w_out)
    h = dense_ffn(h, l0_ffn_norm, l0_ffn_w_gate, l0_ffn_w_up, l0_ffn_w_down)
    h = na_layer(h, batch, seq, l1_mix_norm, l1_na_w_qkv, l1_na_q_norm, l1_na_k_norm, l1_na_rpb, l1_na_w_out)
    h = moe_layer(h, l1_ffn_norm, l1_moe_w_router, l1_moe_w_gate, l1_moe_w_up, l1_moe_w_down)
    h = mlstm_layer(h, batch, seq, l2_mix_norm, l2_mlstm_w_in, l2_mlstm_b_gates, l2_mlstm_head_norm, l2_mlstm_w_out)
    h = dense_ffn(h, l2_ffn_norm, l2_ffn_w_gate, l2_ffn_w_up, l2_ffn_w_down)
    h = na_layer(h, batch, seq, l3_mix_norm, l3_na_w_qkv, l3_na_q_norm, l3_na_k_norm, l3_na_rpb, l3_na_w_out)
    h = moe_layer(h, l3_ffn_norm, l3_moe_w_router, l3_moe_w_gate, l3_moe_w_up, l3_moe_w_down)
    return h.reshape(batch, seq, d)
```

```python
import functools

import jax
import jax.numpy as jnp
import numpy as np
from jax import lax
from jax.experimental import pallas as pl
from jax.experimental.pallas import tpu as pltpu

F32 = jnp.float32
BF16 = jnp.bfloat16
U32 = jnp.uint32

NORM_EPS = 1e-6
LANES = 128
VMEM_LIMIT_BYTES = 56 << 20
NEG_BIG = -1e30

MLSTM_HEADS = 4
MLSTM_DK = 256
GATE_SOFTCAP = 15.0
MLSTM_CHUNK = 256
GRID_W = 64
NA_HEADS = 16
NA_HEAD_DIM = 128
NA_KH = 8
NA_KW = 16
NA_ROWS_PER_BLOCK = 4
NA_HEADS_PER_GROUP = 4
N_EXPERTS = 8
TOP_K = 2

MM_TM, MM_TN = 2048, 1024
MLSTM_OUT_TM = 512
FFN_TM, FFN_TF = 1024, 512
MOE_TM, MOE_TF = 1024, 512
MOE_SUBTILES = 2
ROUTER_TT = 512
SCATTER_TT = 1024
COMBINE_TT = 256
NORM_ROWS = 128


def _params(*semantics):
    return pltpu.CompilerParams(dimension_semantics=semantics, vmem_limit_bytes=VMEM_LIMIT_BYTES)


def _dot(a, b):
    return jnp.dot(a, b, preferred_element_type=F32)


def _dot_nt(a, b):
    return lax.dot_general(a, b, (((1,), (1,)), ((), ())), preferred_element_type=F32)


def _split3(x):
    hi = x.astype(BF16)
    r1 = x - hi.astype(F32)
    mid = r1.astype(BF16)
    lo = (r1 - mid.astype(F32)).astype(BF16)
    return hi, mid, lo


def _rms_rows(x, gain):
    ms = jnp.mean(x * x, axis=-1, keepdims=True)
    return x * lax.rsqrt(ms + NORM_EPS) * gain


def _fill_rmsnorm(x_ref, g_ref, a_ref):
    def body(c, carry):
        r = pl.multiple_of(c * NORM_ROWS, NORM_ROWS)
        a_ref[pl.ds(r, NORM_ROWS), :] = _rms_rows(x_ref[pl.ds(r, NORM_ROWS), :], g_ref[...]).astype(BF16)
        return carry
    lax.fori_loop(0, x_ref.shape[0] // NORM_ROWS, body, 0)


def _norm_matmul_kernel(x_ref, g_ref, w_ref, o_ref, a_ref):
    @pl.when(pl.program_id(1) == 0)
    def _():
        _fill_rmsnorm(x_ref, g_ref, a_ref)
    o_ref[...] = _dot(a_ref[...], w_ref[...]).astype(o_ref.dtype)


def norm_matmul(x, gain, w, out_dtype):
    m, k = x.shape
    n = w.shape[1]
    tm, tn = min(MM_TM, m), min(MM_TN, n)
    return pl.pallas_call(
        _norm_matmul_kernel,
        out_shape=jax.ShapeDtypeStruct((m, n), out_dtype),
        grid=(m // tm, n // tn),
        in_specs=[pl.BlockSpec((tm, k), lambda i, j: (i, 0), pipeline_mode=pl.Buffered(1)),
                  pl.BlockSpec((1, k), lambda i, j: (0, 0)),
                  pl.BlockSpec((k, tn), lambda i, j: (0, j))],
        out_specs=pl.BlockSpec((tm, tn), lambda i, j: (i, j)),
        scratch_shapes=[pltpu.VMEM((tm, k), BF16)],
        compiler_params=_params("arbitrary", "arbitrary"),
        name="norm_matmul",
    )(x, gain.reshape(1, k), w)


def _qkv_matmul_kernel(x_ref, g_ref, w_ref, hg_ref, o_ref, a_ref, *, n_norm_tiles):
    j = pl.program_id(1)

    @pl.when(j == 0)
    def _():
        _fill_rmsnorm(x_ref, g_ref, a_ref)
    acc = _dot(a_ref[...], w_ref[...])

    @pl.when(j < n_norm_tiles)
    def _():
        hd = hg_ref.shape[2]
        for g in range(o_ref.shape[1] // hd):
            cols = slice(g * hd, (g + 1) * hd)
            o_ref[:, cols] = _rms_rows(acc[:, cols], hg_ref[0]).astype(o_ref.dtype)

    @pl.when(j >= n_norm_tiles)
    def _():
        o_ref[...] = acc.astype(o_ref.dtype)


def qkv_matmul(x, gain, w, q_norm, k_norm):
    m, k = x.shape
    n = w.shape[1]
    d = n // 3
    hd = NA_HEAD_DIM
    tm, tn = min(MM_TM, m), min(MM_TN, d)
    per = d // tn
    head_gain = jnp.concatenate([jnp.tile((q_norm * hd ** -0.5)[None], (per, 1)), jnp.tile(k_norm[None], (per, 1)),
                                 jnp.ones((per, hd), F32)]).reshape(3 * per, 1, hd)
    return pl.pallas_call(
        functools.partial(_qkv_matmul_kernel, n_norm_tiles=2 * per),
        out_shape=jax.ShapeDtypeStruct((m, n), BF16),
        grid=(m // tm, n // tn),
        in_specs=[pl.BlockSpec((tm, k), lambda i, j: (i, 0), pipeline_mode=pl.Buffered(1)),
                  pl.BlockSpec((1, k), lambda i, j: (0, 0)),
                  pl.BlockSpec((k, tn), lambda i, j: (0, j)),
                  pl.BlockSpec((1, 1, hd), lambda i, j: (j, 0, 0))],
        out_specs=pl.BlockSpec((tm, tn), lambda i, j: (i, j)),
        scratch_shapes=[pltpu.VMEM((tm, k), BF16)],
        compiler_params=_params("arbitrary", "arbitrary"),
        name="qkv_matmul",
    )(x, gain.reshape(1, k), w, head_gain)


def _mlstm_out_kernel(hf_ref, hb_ref, og_ref, g_ref, w_ref, r_ref, o_ref, a_ref, *, heads):
    @pl.when(pl.program_id(1) == 0)
    def _():
        dv = a_ref.shape[1] // heads

        def body(c, carry):
            r = pl.multiple_of(c * NORM_ROWS, NORM_ROWS)
            rows = pl.ds(r, NORM_ROWS)
            for h in range(heads):
                cols = slice(h * dv, (h + 1) * dv)
                hh = hf_ref[rows, cols].astype(F32) + hb_ref[rows, cols].astype(F32)
                y = _rms_rows(hh, g_ref[:, cols]) * jax.nn.sigmoid(og_ref[rows, cols].astype(F32))
                a_ref[rows, cols] = y.astype(BF16)
            return carry
        lax.fori_loop(0, a_ref.shape[0] // NORM_ROWS, body, 0)
    o_ref[...] = r_ref[...] + _dot(a_ref[...], w_ref[...])


def mlstm_out_proj(hf, hb, proj, head_norm, w, res):
    m, d = hf.shape
    n = w.shape[1]
    tm, tn = min(MLSTM_OUT_TM, m), min(MM_TN, n)
    o_col_block = (proj.shape[1] - d) // d
    return pl.pallas_call(
        functools.partial(_mlstm_out_kernel, heads=MLSTM_HEADS),
        out_shape=jax.ShapeDtypeStruct((m, n), F32),
        grid=(m // tm, n // tn),
        in_specs=[pl.BlockSpec((tm, d), lambda i, j: (i, 0)),
                  pl.BlockSpec((tm, d), lambda i, j: (i, 0)),
                  pl.BlockSpec((tm, d), lambda i, j: (i, o_col_block)),
                  pl.BlockSpec((1, d), lambda i, j: (0, 0)),
                  pl.BlockSpec((d, tn), lambda i, j: (0, j)),
                  pl.BlockSpec((tm, tn), lambda i, j: (i, j))],
        out_specs=pl.BlockSpec((tm, tn), lambda i, j: (i, j)),
        scratch_shapes=[pltpu.VMEM((tm, d), BF16)],
        compiler_params=_params("arbitrary", "arbitrary"),
        name="mlstm_out_proj",
    )(hf, hb, proj, head_norm.reshape(1, d), w, res)


def _swiglu(a, wg, wu, wd):
    gate = _dot(a, wg)
    up = _dot(a, wu)
    hidden = (gate * jax.nn.sigmoid(gate) * up).astype(BF16)
    return _dot(hidden, wd)


def _ffn_kernel(x_ref, g_ref, wg_ref, wu_ref, wd_ref, o_ref, a_ref):
    @pl.when(pl.program_id(1) == 0)
    def _():
        _fill_rmsnorm(x_ref, g_ref, a_ref)
        o_ref[...] = x_ref[...]
    o_ref[...] += _swiglu(a_ref[...], wg_ref[...].astype(BF16), wu_ref[...].astype(BF16), wd_ref[...].astype(BF16))


def dense_ffn(x, gain, wg, wu, wd):
    m, d = x.shape
    f = wg.shape[1]
    tm, tf = min(FFN_TM, m), min(FFN_TF, f)
    return pl.pallas_call(
        _ffn_kernel,
        out_shape=jax.ShapeDtypeStruct((m, d), F32),
        grid=(m // tm, f // tf),
        in_specs=[pl.BlockSpec((tm, d), lambda i, j: (i, 0), pipeline_mode=pl.Buffered(1)),
                  pl.BlockSpec((1, d), lambda i, j: (0, 0)),
                  pl.BlockSpec((d, tf), lambda i, j: (0, j)),
                  pl.BlockSpec((d, tf), lambda i, j: (0, j)),
                  pl.BlockSpec((tf, d), lambda i, j: (j, 0))],
        out_specs=pl.BlockSpec((tm, d), lambda i, j: (i, 0), pipeline_mode=pl.Buffered(1)),
        scratch_shapes=[pltpu.VMEM((tm, d), BF16)],
        compiler_params=_params("arbitrary", "arbitrary"),
        name="dense_ffn",
    )(x, gain.reshape(1, d), wg, wu, wd)


def _unpack_bf16_pairs(words):
    hi = lax.bitcast_convert_type(words & jnp.uint32(0xFFFF0000), F32).astype(BF16)
    lo = lax.bitcast_convert_type(words << 16, F32).astype(BF16)
    return hi, lo


def _pack_bf16_pairs(hi, lo):
    return lax.bitcast_convert_type(hi, U32) | (lax.bitcast_convert_type(lo, U32) >> 16)


def _moe_ffn_kernel(te_ref, tr_ref, nu_ref, xs_ref, wg_ref, wu_ref, wd_ref, o_ref, a_ref):
    i, j = pl.program_id(0), pl.program_id(1)

    @pl.when(i < nu_ref[0])
    def _():
        @pl.when(j == 0)
        def _():
            half = a_ref.shape[1] // 2

            def body(c, carry):
                r = pl.multiple_of(c * NORM_ROWS, NORM_ROWS)
                hi, lo = _unpack_bf16_pairs(xs_ref[pl.ds(r, NORM_ROWS), :])
                a_ref[pl.ds(r, NORM_ROWS), :half] = hi
                a_ref[pl.ds(r, NORM_ROWS), half:] = lo
                return carry
            lax.fori_loop(0, a_ref.shape[0] // NORM_ROWS, body, 0)
            o_ref[...] = jnp.zeros_like(o_ref)

        wg, wu, wd = wg_ref[...].astype(BF16), wu_ref[...].astype(BF16), wd_ref[...].astype(BF16)
        sub = a_ref.shape[0] // MOE_SUBTILES
        rows_used = tr_ref[jnp.minimum(i, nu_ref[0] - 1)]
        for sb in range(MOE_SUBTILES):
            def sub_step(rows=slice(sb * sub, (sb + 1) * sub)):
                o_ref[rows, :] += _swiglu(a_ref[rows, :], wg, wu, wd)
            if sb == 0:
                sub_step()
            else:
                pl.when(rows_used > sb * sub)(sub_step)

    @pl.when((i >= nu_ref[0]) & (j == 0))
    def _():
        o_ref[...] = jnp.zeros_like(o_ref)


def moe_ffn(tile_expert, tile_rows, n_used, xs, wg, wu, wd):
    p, half = xs.shape
    d = 2 * half
    f = wg.shape[2]
    tm, tf = MOE_TM, min(MOE_TF, f)
    nf = f // tf

    def row_map(i, j, te, tr, nu):
        return (jnp.minimum(i, nu[0] - 1), 0)

    def f_idx(i, j, nu):
        return jnp.where(i < nu[0], j, nf - 1)

    def te_idx(i, te, nu):
        return te[jnp.minimum(i, nu[0] - 1)]

    return pl.pallas_call(
        _moe_ffn_kernel,
        out_shape=jax.ShapeDtypeStruct((p, d), F32),
        grid_spec=pltpu.PrefetchScalarGridSpec(
            num_scalar_prefetch=3,
            grid=(p // tm, nf),
            in_specs=[pl.BlockSpec((tm, half), row_map, pipeline_mode=pl.Buffered(1)),
                      pl.BlockSpec((None, d, tf), lambda i, j, te, tr, nu: (te_idx(i, te, nu), 0, f_idx(i, j, nu))),
                      pl.BlockSpec((None, d, tf), lambda i, j, te, tr, nu: (te_idx(i, te, nu), 0, f_idx(i, j, nu))),
                      pl.BlockSpec((None, tf, d), lambda i, j, te, tr, nu: (te_idx(i, te, nu), f_idx(i, j, nu), 0))],
            out_specs=pl.BlockSpec((tm, d), lambda i, j, te, tr, nu: (i, 0), pipeline_mode=pl.Buffered(1)),
            scratch_shapes=[pltpu.VMEM((tm, d), BF16)]),
        compiler_params=_params("arbitrary", "arbitrary"),
        name="moe_ffn",
    )(tile_expert, tile_rows, n_used, xs, wg, wu, wd)


def _mlstm_gates_kernel(x_ref, g_ref, wt_ref, b_ref, o_ref):
    xn = _rms_rows(x_ref[...], g_ref[...])
    xh = xn.astype(BF16)
    xl = (xn - xh.astype(F32)).astype(BF16)
    w = wt_ref[...]
    wh = w.astype(BF16)
    wl = (w - wh.astype(F32)).astype(BF16)
    pre = _dot_nt(wh, xh) + (_dot_nt(wh, xl) + _dot_nt(wl, xh)) + b_ref[...]
    g = GATE_SOFTCAP * jnp.tanh(pre / GATE_SOFTCAP)
    log_sig = jnp.minimum(g, 0.0) - jnp.log1p(jnp.exp(-jnp.abs(g)))
    row = lax.broadcasted_iota(jnp.int32, g.shape, 0)
    is_forget = ((row >= MLSTM_HEADS) & (row < 2 * MLSTM_HEADS)) | (row >= 3 * MLSTM_HEADS)
    o_ref[...] = jnp.where(is_forget, log_sig, g)


def mlstm_gates(x, gain, w_gates_t, b_gates):
    t, d = x.shape
    ng = w_gates_t.shape[0]
    tt = min(ROUTER_TT, t)
    return pl.pallas_call(
        _mlstm_gates_kernel,
        out_shape=jax.ShapeDtypeStruct((ng, t), F32),
        grid=(t // tt,),
        in_specs=[pl.BlockSpec((tt, d), lambda i: (i, 0)),
                  pl.BlockSpec((1, d), lambda i: (0, 0)),
                  pl.BlockSpec((ng, d), lambda i: (0, 0)),
                  pl.BlockSpec((ng, 1), lambda i: (0, 0))],
        out_specs=pl.BlockSpec((ng, tt), lambda i: (0, i)),
        compiler_params=_params("arbitrary"),
        name="mlstm_gates",
    )(x, gain.reshape(1, d), w_gates_t, b_gates.reshape(ng, 1))


def _mlstm_chunk(q_ref, k_ref, v_ref, li_ref, lf_ref, h_ref, c_ref, n_ref, m_ref, slot, head, reverse):
    L = q_ref.shape[0]
    dk = c_ref.shape[1]
    dv = c_ref.shape[2]
    q = q_ref[:, head * dk:(head + 1) * dk] * jnp.asarray(dk ** -0.5, BF16)
    k = k_ref[:, head * dk:(head + 1) * dk]
    v = v_ref[:, head * dv:(head + 1) * dv]
    li = li_ref[head]
    lf = lf_ref[head]
    r_i = lax.broadcasted_iota(jnp.int32, (L, L), 0)
    c_i = lax.broadcasted_iota(jnp.int32, (L, L), 1)
    visible = (c_i >= r_i) if reverse else (c_i <= r_i)
    cum = (r_i >= c_i) if reverse else (r_i <= c_i)

    hi, mid, lo = (p.astype(F32) for p in _split3(lf))
    prow = lax.broadcasted_iota(jnp.int32, (16, L), 0)
    pieces = jnp.where(prow == 0, hi, jnp.where(prow == 1, mid, jnp.where(prow == 2, lo, 0.0)))
    b = jnp.sum(_dot(pieces.astype(BF16), jnp.where(cum, 1.0, 0.0).astype(BF16)), axis=0, keepdims=True)
    g = jnp.sum(lf, axis=-1, keepdims=True)
    u = li - b

    m_prev = m_ref[slot][:, :1]
    a_mat = jnp.where(visible, jnp.broadcast_to(u, (L, L)), -jnp.inf)
    m_row = jnp.maximum(jnp.max(a_mat, axis=-1, keepdims=True), m_prev)
    w_intra = (jnp.exp(a_mat - m_row) * _dot_nt(q, k)).astype(BF16)
    w_inter = jnp.exp(m_prev - m_row)

    ones = jnp.ones((L, LANES), BF16)
    num = w_inter * _dot(q, c_ref[slot].astype(BF16)) + _dot(w_intra, v)
    den = w_inter * _dot(q, n_ref[slot].astype(BF16)) + _dot(w_intra, ones)
    b_col = jnp.sum(jnp.where(r_i == c_i, jnp.broadcast_to(b, (L, L)), 0.0), axis=-1, keepdims=True)
    inv = 1.0 / jnp.maximum(jnp.abs(den), jnp.exp(-(b_col + m_row)))
    h_ref[:, head * dv:(head + 1) * dv] = (num * jnp.concatenate([inv] * (dv // LANES), axis=1)).astype(h_ref.dtype)

    m_new = g + jnp.maximum(m_prev, jnp.max(u, axis=-1, keepdims=True))
    decay = jnp.exp(g + m_prev - m_new)
    ks_t = (k.astype(F32).T * jnp.exp(g + u - m_new)).astype(BF16)
    c_ref[slot] = decay * c_ref[slot] + _dot(ks_t, v)
    n_ref[slot] = decay * n_ref[slot] + _dot(ks_t, ones)
    m_ref[slot] = jnp.broadcast_to(m_new, (1, LANES))


def _mlstm_kernel(qf, kf, vf, lif, lff, qb, kb, vb, lib, lfb, hf_ref, hb_ref, c_ref, n_ref, m_ref):
    @pl.when(pl.program_id(1) == 0)
    def _():
        c_ref[...] = jnp.zeros_like(c_ref)
        n_ref[...] = jnp.zeros_like(n_ref)
        m_ref[...] = jnp.full_like(m_ref, -jnp.inf)
    heads = lif.shape[0]
    for h in range(heads):
        _mlstm_chunk(qf, kf, vf, lif, lff, hf_ref, c_ref, n_ref, m_ref, h, h, False)
        _mlstm_chunk(qb, kb, vb, lib, lfb, hb_ref, c_ref, n_ref, m_ref, heads + h, h, True)


def mlstm_scan(proj, gates, batch, seq, d_model):
    heads, dk = MLSTM_HEADS, MLSTM_DK
    dv = d_model // heads
    L = min(MLSTM_CHUNK, seq)
    nc = seq // L
    t = batch * seq
    qw = heads * dk

    def fwd(b, j):
        return b * nc + j

    def bwd(b, j):
        return b * nc + (nc - 1 - j)

    def qkv_specs(pos):
        return [pl.BlockSpec((L, qw), lambda b, j: (pos(b, j), 0)),
                pl.BlockSpec((L, qw), lambda b, j: (pos(b, j), 1)),
                pl.BlockSpec((L, d_model), lambda b, j: (pos(b, j), 2 * qw // d_model))]

    def gate_spec(kind, pos):
        return pl.BlockSpec((heads, 1, L), lambda b, j: (kind, 0, pos(b, j)))

    return pl.pallas_call(
        _mlstm_kernel,
        out_shape=(jax.ShapeDtypeStruct((t, d_model), BF16), jax.ShapeDtypeStruct((t, d_model), BF16)),
        grid=(batch, nc),
        in_specs=(qkv_specs(fwd) + [gate_spec(0, fwd), gate_spec(1, fwd)]
                  + qkv_specs(bwd) + [gate_spec(2, bwd), gate_spec(3, bwd)]),
        out_specs=(pl.BlockSpec((L, d_model), lambda b, j: (fwd(b, j), 0)),
                   pl.BlockSpec((L, d_model), lambda b, j: (bwd(b, j), 0))),
        scratch_shapes=[pltpu.VMEM((2 * heads, dk, dv), F32), pltpu.VMEM((2 * heads, dk, LANES), F32),
                        pltpu.VMEM((2 * heads, 1, LANES), F32)],
        compiler_params=_params("arbitrary", "arbitrary"),
        name="mlstm_scan",
    )(proj, proj, proj, gates, gates, proj, proj, proj, gates, gates)


def _na_fused_kernel(q_ref, k0_ref, k1_ref, k2_ref, v0_ref, v1_ref, v2_ref, bias_ref, wo_ref, x_ref, o_ref):
    hd = NA_HEAD_DIM
    gw = NA_HEADS_PER_GROUP * hd
    for g in range(q_ref.shape[1] // gw):
        outs = []
        for h in range(g * NA_HEADS_PER_GROUP, (g + 1) * NA_HEADS_PER_GROUP):
            cols = slice(h * hd, (h + 1) * hd)
            k = jnp.concatenate([k0_ref[:, cols], k1_ref[:, cols], k2_ref[:, cols]], axis=0)
            v = jnp.concatenate([v0_ref[:, cols], v1_ref[:, cols], v2_ref[:, cols]], axis=0)
            s = _dot_nt(q_ref[:, cols], k) + bias_ref[0, h]
            p = jnp.exp(s - jnp.max(s, axis=-1, keepdims=True))
            o = _dot(p.astype(BF16), v) * (1.0 / jnp.sum(p, axis=-1, keepdims=True))
            outs.append(o.astype(BF16))
        part = _dot(jnp.concatenate(outs, axis=1), wo_ref[g * gw:(g + 1) * gw, :])
        if g == 0:
            o_ref[...] = x_ref[...] + part
        else:
            o_ref[...] += part


def _na_bias_tables(rpb, rows):
    rb, w = NA_ROWS_PER_BLOCK, GRID_W
    nb = rows // rb
    heads, n_dr, n_dc = rpb.shape
    period = 2 * w
    left = (w - 1) - (NA_KW - 1)
    vec = jnp.pad(rpb.astype(F32), ((0, 0), (0, 0), (left, period - left - n_dc)))
    toep = jnp.tile(vec, (1, 1, w))[:, :, :w * (period - 1)].reshape(heads, n_dr, w, period - 1)[..., w - 1:]
    zero = jnp.zeros((heads, w, w), F32)
    tables = []
    for blk in (0, 1, nb - 1):
        ws = min(max(blk - 1, 0), nb - 3) * rb
        r = blk * rb + np.arange(rb)[:, None, None, None]
        c = np.arange(w)[None, :, None, None]
        a = ws + np.arange(3 * rb)[None, None, :, None]
        kc = np.arange(w)[None, None, None, :]
        rs = np.clip(r - NA_KH // 2, 0, rows - NA_KH)
        cs = np.clip(c - NA_KW // 2, 0, w - NA_KW)
        ok = (a >= rs) & (a < rs + NA_KH) & (kc >= cs) & (kc < cs + NA_KW)
        ok = np.broadcast_to(ok, (rb, w, 3 * rb, w)).reshape(rb * w, 3 * rb * w)
        parts = []
        for qr in range(rb):
            drs = [ws + ar - (blk * rb + qr) + NA_KH - 1 for ar in range(3 * rb)]
            parts.append(jnp.stack([toep[:, dr] if 0 <= dr < n_dr else zero for dr in drs], axis=2))
        vals = jnp.stack(parts, axis=1).reshape(heads, rb * w, 3 * rb * w)
        tables.append(jnp.where(ok[None], vals, NEG_BIG))
    return jnp.stack(tables)


def na_attention_out(qkv, rpb, w_out, x, batch, seq):
    rb, w = NA_ROWS_PER_BLOCK, GRID_W
    t, d = x.shape
    rows = seq // w
    nb = rows // rb
    tq = rb * w
    bias = _na_bias_tables(rpb, rows)
    once = pl.Buffered(1)

    def kv_spec(part, off):
        return pl.BlockSpec((tq, d), lambda b, i: (b * nb + jnp.clip(i - 1, 0, nb - 3) + off, part))

    def bias_type(i):
        return jnp.where(i == 0, 0, jnp.where(i == nb - 1, 2, 1))

    return pl.pallas_call(
        _na_fused_kernel,
        out_shape=jax.ShapeDtypeStruct((t, d), F32),
        grid=(batch, nb),
        in_specs=([pl.BlockSpec((tq, d), lambda b, i: (b * nb + i, 0))]
                  + [kv_spec(1, off) for off in range(3)] + [kv_spec(2, off) for off in range(3)]
                  + [pl.BlockSpec((1, NA_HEADS, tq, 3 * tq), lambda b, i: (bias_type(i), 0, 0, 0), pipeline_mode=once),
                     pl.BlockSpec((d, d), lambda b, i: (0, 0), pipeline_mode=once),
                     pl.BlockSpec((tq, d), lambda b, i: (b * nb + i, 0))]),
        out_specs=pl.BlockSpec((tq, d), lambda b, i: (b * nb + i, 0)),
        compiler_params=_params("arbitrary", "arbitrary"),
        name="na_attention_out",
    )(qkv, qkv, qkv, qkv, qkv, qkv, qkv, bias, w_out, x)


def _router_kernel(x_ref, g_ref, wr_ref, xp_ref, meta_ref, cnt_ref, carry_ref):
    @pl.when(pl.program_id(0) == 0)
    def _():
        carry_ref[...] = jnp.zeros_like(carry_ref)

    tt = x_ref.shape[0]
    xn = _rms_rows(x_ref[...], g_ref[...])
    xh = xn.astype(BF16)
    xhf = xh.astype(F32)
    xl = (xn - xhf).astype(BF16)
    wr = wr_ref[...]
    wh = wr.astype(BF16)
    wl = (wr - wh.astype(F32)).astype(BF16)
    logits = _dot(xh, wh) + (_dot(xh, wl) + _dot(xl, wh))
    col = lax.broadcasted_iota(jnp.int32, logits.shape, 1).astype(F32)
    lg = jnp.where(col < N_EXPERTS, logits, -jnp.inf)
    m1 = jnp.max(lg, axis=-1, keepdims=True)
    i1 = jnp.min(jnp.where(lg == m1, col, float(LANES)), axis=-1, keepdims=True)
    lg2 = jnp.where(col == i1, -jnp.inf, lg)
    m2 = jnp.max(lg2, axis=-1, keepdims=True)
    i2 = jnp.min(jnp.where(lg2 == m2, col, float(LANES)), axis=-1, keepdims=True)
    e2 = jnp.exp(m2 - m1)
    w1 = 1.0 / (1.0 + e2)
    w2 = e2 * w1

    hit1, hit2 = col == i1, col == i2
    cnt = jnp.where(hit1 | hit2, 1.0, 0.0)
    r_i = lax.broadcasted_iota(jnp.int32, (tt, tt), 0)
    c_i = lax.broadcasted_iota(jnp.int32, (tt, tt), 1)
    before = jnp.where(c_i < r_i, 1.0, 0.0).astype(BF16)
    prefix = _dot(before, cnt.astype(BF16)) + carry_ref[...]
    rank1 = jnp.sum(jnp.where(hit1, prefix, 0.0), axis=-1, keepdims=True)
    rank2 = jnp.sum(jnp.where(hit2, prefix, 0.0), axis=-1, keepdims=True)
    carry_ref[...] += jnp.sum(cnt, axis=0, keepdims=True)
    cnt_ref[...] = carry_ref[...]

    meta = jnp.zeros(logits.shape, F32)
    for lane, val in enumerate((i1, i2, w1, w2, rank1, rank2)):
        meta = jnp.where(col == lane, val, meta)
    meta_ref[...] = meta
    half = xhf.shape[1] // 2
    xp_ref[...] = _pack_bf16_pairs(xhf[:, :half], xhf[:, half:])


def moe_router(x, gain, w_router):
    t, d = x.shape
    tt = min(ROUTER_TT, t)
    wr = jnp.zeros((d, LANES), F32).at[:, :N_EXPERTS].set(w_router)
    return pl.pallas_call(
        _router_kernel,
        out_shape=(jax.ShapeDtypeStruct((t, d // 2), U32), jax.ShapeDtypeStruct((t, LANES), F32),
                   jax.ShapeDtypeStruct((1, LANES), F32)),
        grid=(t // tt,),
        in_specs=[pl.BlockSpec((tt, d), lambda i: (i, 0)),
                  pl.BlockSpec((1, d), lambda i: (0, 0)),
                  pl.BlockSpec((d, LANES), lambda i: (0, 0))],
        out_specs=(pl.BlockSpec((tt, d // 2), lambda i: (i, 0)),
                   pl.BlockSpec((tt, LANES), lambda i: (i, 0)),
                   pl.BlockSpec((1, LANES), lambda i: (0, 0))),
        scratch_shapes=[pltpu.VMEM((1, LANES), F32)],
        compiler_params=_params("arbitrary"),
        name="moe_router",
    )(x, gain.reshape(1, d), wr)


def _scatter_kernel(pad_lo_ref, pad_hi_ref, p0_ref, p1_ref, src_ref, dst_ref, sem):
    tt = p0_ref.shape[2]

    def row_copy(src_row, dst_row):
        return pltpu.make_async_copy(src_ref.at[pl.ds(src_row, 1)], dst_ref.at[pl.ds(dst_row, 1)], sem)

    @pl.when(pl.program_id(0) == 0)
    def _():
        for e in range(N_EXPERTS):
            def pad_body(r, carry):
                row_copy(0, r).start()
                return carry
            lax.fori_loop(pad_lo_ref[e], pad_hi_ref[e], pad_body, 0)

            def pad_wait(r, carry):
                row_copy(0, 0).wait()
                return carry
            lax.fori_loop(pad_lo_ref[e], pad_hi_ref[e], pad_wait, 0)

    def start_body(t, carry):
        row_copy(t, p0_ref[0, 0, t]).start()
        row_copy(t, p1_ref[0, 0, t]).start()
        return carry
    lax.fori_loop(0, tt, start_body, 0, unroll=8)

    def wait_body(t, carry):
        row_copy(0, 0).wait()
        row_copy(0, 0).wait()
        return carry
    lax.fori_loop(0, tt, wait_body, 0, unroll=8)


def moe_scatter_rows(xp, pos0, pos1, pad_lo, pad_hi, n_rows):
    t, half = xp.shape
    tt = min(SCATTER_TT, t)
    nblk = t // tt
    smem_rows = pl.BlockSpec((1, 1, tt), lambda i, lo, hi: (i, 0, 0), memory_space=pltpu.SMEM)
    return pl.pallas_call(
        _scatter_kernel,
        out_shape=jax.ShapeDtypeStruct((n_rows, half), xp.dtype),
        grid_spec=pltpu.PrefetchScalarGridSpec(
            num_scalar_prefetch=2,
            grid=(nblk,),
            in_specs=[smem_rows, smem_rows, pl.BlockSpec((tt, half), lambda i, lo, hi: (i, 0))],
            out_specs=pl.BlockSpec(memory_space=pl.ANY),
            scratch_shapes=[pltpu.SemaphoreType.DMA]),
        compiler_params=pltpu.CompilerParams(dimension_semantics=("arbitrary",), has_side_effects=True),
        name="moe_scatter_rows",
    )(pad_lo, pad_hi, pos0.reshape(nblk, 1, tt), pos1.reshape(nblk, 1, tt), xp)


def _combine_kernel(p0_ref, p1_ref, x_ref, meta_ref, y_ref, o_ref, buf_ref, sem):
    tt = x_ref.shape[0]

    def row_copy(k, src_row, t):
        return pltpu.make_async_copy(y_ref.at[pl.ds(src_row, 1)], buf_ref.at[k, pl.ds(t, 1)], sem)

    def start_body(t, carry):
        row_copy(0, p0_ref[0, 0, t], t).start(priority=0)
        row_copy(1, p1_ref[0, 0, t], t).start(priority=1)
        return carry
    lax.fori_loop(0, tt, start_body, 0, unroll=8)

    def wait_body(t, carry):
        row_copy(0, 0, 0).wait()
        row_copy(1, 0, 0).wait()
        return carry
    lax.fori_loop(0, tt, wait_body, 0, unroll=8)

    meta = meta_ref[...]
    o_ref[...] = x_ref[...] + (meta[:, 2:3] * buf_ref[0] + meta[:, 3:4] * buf_ref[1])


def moe_combine(x, meta, pos0, pos1, y):
    t, d = x.shape
    tt = min(COMBINE_TT, t)
    nblk = t // tt
    smem_rows = pl.BlockSpec((1, 1, tt), lambda i: (i, 0, 0), memory_space=pltpu.SMEM)
    return pl.pallas_call(
        _combine_kernel,
        out_shape=jax.ShapeDtypeStruct((t, d), F32),
        grid=(nblk,),
        in_specs=[smem_rows, smem_rows,
                  pl.BlockSpec((tt, d), lambda i: (i, 0)),
                  pl.BlockSpec((tt, LANES), lambda i: (i, 0)),
                  pl.BlockSpec(memory_space=pl.ANY)],
        out_specs=pl.BlockSpec((tt, d), lambda i: (i, 0)),
        scratch_shapes=[pltpu.VMEM((2, tt, d), F32), pltpu.SemaphoreType.DMA],
        compiler_params=_params("arbitrary"),
        name="moe_combine",
    )(pos0.reshape(nblk, 1, tt), pos1.reshape(nblk, 1, tt), x, meta, y)


def mlstm_layer(x, batch, seq, mix_norm, w_in, b_gates, head_norm, w_out):
    t, d = x.shape
    n_main = w_in.shape[1] - 4 * MLSTM_HEADS
    proj = norm_matmul(x, mix_norm, w_in[:, :n_main].astype(BF16), BF16)
    gates = mlstm_gates(x, mix_norm, w_in[:, n_main:].T, b_gates)
    hf, hb = mlstm_scan(proj, gates.reshape(4 * MLSTM_HEADS, 1, t), batch, seq, d)
    return mlstm_out_proj(hf, hb, proj, head_norm, w_out.astype(BF16), x)


def na_layer(x, batch, seq, mix_norm, w_qkv, q_norm, k_norm, rpb, w_out):
    qkv = qkv_matmul(x, mix_norm, w_qkv.astype(BF16), q_norm, k_norm)
    return na_attention_out(qkv, rpb, w_out.astype(BF16), x, batch, seq)


def moe_layer(x, ffn_norm, w_router, w_gate, w_up, w_down):
    t, d = x.shape
    tm = MOE_TM
    xp, meta, counts = moe_router(x, ffn_norm, w_router)
    counts = counts[0, :N_EXPERTS].astype(jnp.int32)
    tiles = (counts + tm - 1) // tm
    tile_end = jnp.cumsum(tiles)
    row_start = (tile_end - tiles) * tm
    n_tiles = TOP_K * t // tm + N_EXPERTS
    tile_expert = jnp.minimum(jnp.sum(jnp.arange(n_tiles)[:, None] >= tile_end[None, :], axis=1),
                              N_EXPERTS - 1).astype(jnp.int32)
    n_used = tile_end[-1:].astype(jnp.int32)
    first_tile = (tile_end - tiles)[tile_expert]
    tile_rows = jnp.clip(counts[tile_expert] - (jnp.arange(n_tiles) - first_tile) * tm, 0, tm).astype(jnp.int32)
    e0, e1 = meta[:, 0].astype(jnp.int32), meta[:, 1].astype(jnp.int32)
    pos0 = row_start[e0] + meta[:, 4].astype(jnp.int32)
    pos1 = row_start[e1] + meta[:, 5].astype(jnp.int32)
    pad_hi = (tile_end * tm).at[-1].set(n_tiles * tm)
    xs = moe_scatter_rows(xp, pos0, pos1, (row_start + counts).astype(jnp.int32),
                          pad_hi.astype(jnp.int32), n_tiles * tm)
    y = moe_ffn(tile_expert, tile_rows, n_used, xs, w_gate, w_up, w_down)
    return moe_combine(x, meta, pos0, pos1, y)


def kernel(x, l0_mix_norm, l0_mlstm_w_in, l0_mlstm_b_gates, l0_mlstm_head_norm, l0_mlstm_w_out, l0_ffn_norm, l0_ffn_w_gate, l0_ffn_w_up, l0_ffn_w_down, l1_mix_norm, l1_na_w_qkv, l1_na_q_norm, l1_na_k_norm, l1_na_rpb, l1_na_w_out, l1_ffn_norm, l1_moe_w_router, l1_moe_w_gate, l1_moe_w_up, l1_moe_w_down, l2_mix_norm, l2_mlstm_w_in, l2_mlstm_b_gates, l2_mlstm_head_norm, l2_mlstm_w_out, l2_ffn_norm, l2_ffn_w_gate, l2_ffn_w_up, l2_ffn_w_down, l3_mix_norm, l3_na_w_qkv, l3_na_q_norm, l3_na_k_norm, l3_na_rpb, l3_na_w_out, l3_ffn_norm, l3_moe_w_router, l3_moe_w_gate, l3_moe_w_up, l3_moe_w_down):
    batch, seq, d = x.shape
    h = x.reshape(batch * seq, d)
    h = mlstm_layer(h, batch, seq, l0_mix_norm, l0_mlstm_w_in, l0_mlstm_b_gates, l0_mlstm_head_norm, l0_mlstm_w_out)
    h = dense_ffn(h, l0_ffn_norm, l0_ffn_w_gate.astype(BF16), l0_ffn_w_up.astype(BF16), l0_ffn_w_down.astype(BF16))
    h = na_layer(h, batch, seq, l1_mix_norm, l1_na_w_qkv, l1_na_q_norm, l1_na_k_norm, l1_na_rpb, l1_na_w_out)
    h = moe_layer(h, l1_ffn_norm, l1_moe_w_router, l1_moe_w_gate, l1_moe_w_up, l1_moe_w_down)
    h = mlstm_layer(h, batch, seq, l2_mix_norm, l2_mlstm_w_in, l2_mlstm_b_gates, l2_mlstm_head_norm, l2_mlstm_w_out)
    h = dense_ffn(h, l2_ffn_norm, l2_ffn_w_gate.astype(BF16), l2_ffn_w_up.astype(BF16), l2_ffn_w_down.astype(BF16))
    h = na_layer(h, batch, seq, l3_mix_norm, l3_na_w_qkv, l3_na_q_norm, l3_na_k_norm, l3_na_rpb, l3_na_w_out)
    h = moe_layer(h, l3_ffn_norm, l3_moe_w_router, l3_moe_w_gate, l3_moe_w_up, l3_moe_w_down)
    return h.reshape(batch, seq, d)
```

```python
import functools

import jax
import jax.numpy as jnp
import numpy as np
from jax import lax
from jax.experimental import pallas as pl
from jax.experimental.pallas import tpu as pltpu

F32 = jnp.float32
BF16 = jnp.bfloat16
U32 = jnp.uint32

NORM_EPS = 1e-6
LANES = 128
VMEM_LIMIT_BYTES = 56 << 20
NEG_BIG = -1e30

MLSTM_HEADS = 4
MLSTM_DK = 256
GATE_SOFTCAP = 15.0
MLSTM_CHUNK = 256
GRID_W = 64
NA_HEADS = 16
NA_HEAD_DIM = 128
NA_KH = 8
NA_KW = 16
NA_ROWS_PER_BLOCK = 4
NA_HEADS_PER_GROUP = 4
N_EXPERTS = 8
TOP_K = 2

MM_TM, MM_TN = 1024, 1024
MLSTM_OUT_TM = 512
FFN_TM, FFN_TF = 1024, 512
MOE_TM, MOE_TF = 1024, 512
MOE_SUBTILES = 2
ROUTER_TT = 512
SCATTER_TT = 1024
COMBINE_TT = 256
NORM_ROWS = 128


def _params(*semantics):
    return pltpu.CompilerParams(dimension_semantics=semantics, vmem_limit_bytes=VMEM_LIMIT_BYTES)


def _dot(a, b):
    return jnp.dot(a, b, preferred_element_type=F32)


def _dot_nt(a, b):
    return lax.dot_general(a, b, (((1,), (1,)), ((), ())), preferred_element_type=F32)


def _split3(x):
    hi = x.astype(BF16)
    r1 = x - hi.astype(F32)
    mid = r1.astype(BF16)
    lo = (r1 - mid.astype(F32)).astype(BF16)
    return hi, mid, lo


def _rms_rows(x, gain):
    ms = jnp.mean(x * x, axis=-1, keepdims=True)
    return x * lax.rsqrt(ms + NORM_EPS) * gain


def _fill_rmsnorm(x_ref, g_ref, a_ref):
    def body(c, carry):
        r = pl.multiple_of(c * NORM_ROWS, NORM_ROWS)
        a_ref[pl.ds(r, NORM_ROWS), :] = _rms_rows(x_ref[pl.ds(r, NORM_ROWS), :], g_ref[...]).astype(BF16)
        return carry
    lax.fori_loop(0, x_ref.shape[0] // NORM_ROWS, body, 0)


def _norm_matmul_kernel(x_ref, g_ref, w_ref, o_ref, a_ref):
    @pl.when(pl.program_id(1) == 0)
    def _():
        _fill_rmsnorm(x_ref, g_ref, a_ref)
    o_ref[...] = _dot(a_ref[...], w_ref[...]).astype(o_ref.dtype)


def norm_matmul(x, gain, w, out_dtype, n_cols):
    m, k = x.shape
    n = n_cols
    tm, tn = min(MM_TM, m), min(MM_TN, n)
    return pl.pallas_call(
        _norm_matmul_kernel,
        out_shape=jax.ShapeDtypeStruct((m, n), out_dtype),
        grid=(m // tm, n // tn),
        in_specs=[pl.BlockSpec((tm, k), lambda i, j: (i, 0), pipeline_mode=pl.Buffered(1)),
                  pl.BlockSpec((1, k), lambda i, j: (0, 0)),
                  pl.BlockSpec((k, tn), lambda i, j: (0, j))],
        out_specs=pl.BlockSpec((tm, tn), lambda i, j: (i, j)),
        scratch_shapes=[pltpu.VMEM((tm, k), BF16)],
        compiler_params=_params("arbitrary", "arbitrary"),
        name="norm_matmul",
    )(x, gain.reshape(1, k), w)


def _qkv_matmul_kernel(x_ref, g_ref, w_ref, hg_ref, o_ref, a_ref, *, n_norm_tiles):
    j = pl.program_id(1)

    @pl.when(j == 0)
    def _():
        _fill_rmsnorm(x_ref, g_ref, a_ref)
    acc = _dot(a_ref[...], w_ref[...])

    @pl.when(j < n_norm_tiles)
    def _():
        hd = hg_ref.shape[2]
        for g in range(o_ref.shape[1] // hd):
            cols = slice(g * hd, (g + 1) * hd)
            o_ref[:, cols] = _rms_rows(acc[:, cols], hg_ref[0]).astype(o_ref.dtype)

    @pl.when(j >= n_norm_tiles)
    def _():
        o_ref[...] = acc.astype(o_ref.dtype)


def qkv_matmul(x, gain, w, q_norm, k_norm):
    m, k = x.shape
    n = w.shape[1]
    d = n // 3
    hd = NA_HEAD_DIM
    tm, tn = min(MM_TM, m), min(MM_TN, d)
    per = d // tn
    head_gain = jnp.concatenate([jnp.tile((q_norm * hd ** -0.5)[None], (per, 1)), jnp.tile(k_norm[None], (per, 1)),
                                 jnp.ones((per, hd), F32)]).reshape(3 * per, 1, hd)
    return pl.pallas_call(
        functools.partial(_qkv_matmul_kernel, n_norm_tiles=2 * per),
        out_shape=jax.ShapeDtypeStruct((m, n), BF16),
        grid=(m // tm, n // tn),
        in_specs=[pl.BlockSpec((tm, k), lambda i, j: (i, 0), pipeline_mode=pl.Buffered(1)),
                  pl.BlockSpec((1, k), lambda i, j: (0, 0)),
                  pl.BlockSpec((k, tn), lambda i, j: (0, j)),
                  pl.BlockSpec((1, 1, hd), lambda i, j: (j, 0, 0))],
        out_specs=pl.BlockSpec((tm, tn), lambda i, j: (i, j)),
        scratch_shapes=[pltpu.VMEM((tm, k), BF16)],
        compiler_params=_params("arbitrary", "arbitrary"),
        name="qkv_matmul",
    )(x, gain.reshape(1, k), w, head_gain)


def _mlstm_out_kernel(hf_ref, hb_ref, og_ref, g_ref, w_ref, r_ref, o_ref, a_ref, *, heads):
    @pl.when(pl.program_id(1) == 0)
    def _():
        dv = a_ref.shape[1] // heads

        def body(c, carry):
            r = pl.multiple_of(c * NORM_ROWS, NORM_ROWS)
            rows = pl.ds(r, NORM_ROWS)
            for h in range(heads):
                cols = slice(h * dv, (h + 1) * dv)
                hh = hf_ref[rows, cols].astype(F32) + hb_ref[rows, cols].astype(F32)
                y = _rms_rows(hh, g_ref[:, cols]) * jax.nn.sigmoid(og_ref[rows, cols].astype(F32))
                a_ref[rows, cols] = y.astype(BF16)
            return carry
        lax.fori_loop(0, a_ref.shape[0] // NORM_ROWS, body, 0)
    o_ref[...] = r_ref[...] + _dot(a_ref[...], w_ref[...])


def mlstm_out_proj(hf, hb, proj, head_norm, w, res):
    m, d = hf.shape
    n = w.shape[1]
    tm, tn = min(MLSTM_OUT_TM, m), min(MM_TN, n)
    o_col_block = (proj.shape[1] - d) // d
    return pl.pallas_call(
        functools.partial(_mlstm_out_kernel, heads=MLSTM_HEADS),
        out_shape=jax.ShapeDtypeStruct((m, n), F32),
        grid=(m // tm, n // tn),
        in_specs=[pl.BlockSpec((tm, d), lambda i, j: (i, 0)),
                  pl.BlockSpec((tm, d), lambda i, j: (i, 0)),
                  pl.BlockSpec((tm, d), lambda i, j: (i, o_col_block)),
                  pl.BlockSpec((1, d), lambda i, j: (0, 0)),
                  pl.BlockSpec((d, tn), lambda i, j: (0, j)),
                  pl.BlockSpec((tm, tn), lambda i, j: (i, j))],
        out_specs=pl.BlockSpec((tm, tn), lambda i, j: (i, j)),
        scratch_shapes=[pltpu.VMEM((tm, d), BF16)],
        compiler_params=_params("arbitrary", "arbitrary"),
        name="mlstm_out_proj",
    )(hf, hb, proj, head_norm.reshape(1, d), w, res)


def _swiglu(a, wg, wu, wd):
    gate = _dot(a, wg)
    up = _dot(a, wu)
    hidden = (gate * jax.nn.sigmoid(gate) * up).astype(BF16)
    return _dot(hidden, wd)


def _ffn_kernel(x_ref, g_ref, wg_ref, wu_ref, wd_ref, o_ref, a_ref):
    @pl.when(pl.program_id(1) == 0)
    def _():
        _fill_rmsnorm(x_ref, g_ref, a_ref)
        o_ref[...] = x_ref[...]
    o_ref[...] += _swiglu(a_ref[...], wg_ref[...].astype(BF16), wu_ref[...].astype(BF16), wd_ref[...].astype(BF16))


def dense_ffn(x, gain, wg, wu, wd):
    m, d = x.shape
    f = wg.shape[1]
    tm, tf = min(FFN_TM, m), min(FFN_TF, f)
    return pl.pallas_call(
        _ffn_kernel,
        out_shape=jax.ShapeDtypeStruct((m, d), F32),
        grid=(m // tm, f // tf),
        in_specs=[pl.BlockSpec((tm, d), lambda i, j: (i, 0), pipeline_mode=pl.Buffered(1)),
                  pl.BlockSpec((1, d), lambda i, j: (0, 0)),
                  pl.BlockSpec((d, tf), lambda i, j: (0, j)),
                  pl.BlockSpec((d, tf), lambda i, j: (0, j)),
                  pl.BlockSpec((tf, d), lambda i, j: (j, 0))],
        out_specs=pl.BlockSpec((tm, d), lambda i, j: (i, 0), pipeline_mode=pl.Buffered(1)),
        scratch_shapes=[pltpu.VMEM((tm, d), BF16)],
        compiler_params=_params("arbitrary", "arbitrary"),
        name="dense_ffn",
    )(x, gain.reshape(1, d), wg, wu, wd)


def _unpack_bf16_pairs(words):
    hi = lax.bitcast_convert_type(words & jnp.uint32(0xFFFF0000), F32).astype(BF16)
    lo = lax.bitcast_convert_type(words << 16, F32).astype(BF16)
    return hi, lo


def _pack_bf16_pairs(hi, lo):
    return lax.bitcast_convert_type(hi, U32) | (lax.bitcast_convert_type(lo, U32) >> 16)


def _moe_ffn_kernel(te_ref, tr_ref, nu_ref, xs_ref, wg_ref, wu_ref, wd_ref, o_ref, a_ref):
    i, j = pl.program_id(0), pl.program_id(1)

    @pl.when(i < nu_ref[0])
    def _():
        @pl.when(j == 0)
        def _():
            half = a_ref.shape[1] // 2

            def body(c, carry):
                r = pl.multiple_of(c * NORM_ROWS, NORM_ROWS)
                hi, lo = _unpack_bf16_pairs(xs_ref[pl.ds(r, NORM_ROWS), :])
                a_ref[pl.ds(r, NORM_ROWS), :half] = hi
                a_ref[pl.ds(r, NORM_ROWS), half:] = lo
                return carry
            lax.fori_loop(0, a_ref.shape[0] // NORM_ROWS, body, 0)
            o_ref[...] = jnp.zeros_like(o_ref)

        wg, wu, wd = wg_ref[...].astype(BF16), wu_ref[...].astype(BF16), wd_ref[...].astype(BF16)
        sub = a_ref.shape[0] // MOE_SUBTILES
        rows_used = tr_ref[jnp.minimum(i, nu_ref[0] - 1)]
        for sb in range(MOE_SUBTILES):
            def sub_step(rows=slice(sb * sub, (sb + 1) * sub)):
                o_ref[rows, :] += _swiglu(a_ref[rows, :], wg, wu, wd)
            if sb == 0:
                sub_step()
            else:
                pl.when(rows_used > sb * sub)(sub_step)

    @pl.when((i >= nu_ref[0]) & (j == 0))
    def _():
        o_ref[...] = jnp.zeros_like(o_ref)


def moe_ffn(tile_expert, tile_rows, n_used, xs, wg, wu, wd):
    p, half = xs.shape
    d = 2 * half
    f = wg.shape[2]
    tm, tf = MOE_TM, min(MOE_TF, f)
    nf = f // tf

    def row_map(i, j, te, tr, nu):
        return (jnp.minimum(i, nu[0] - 1), 0)

    def f_idx(i, j, nu):
        return jnp.where(i < nu[0], j, nf - 1)

    def te_idx(i, te, nu):
        return te[jnp.minimum(i, nu[0] - 1)]

    return pl.pallas_call(
        _moe_ffn_kernel,
        out_shape=jax.ShapeDtypeStruct((p, d), F32),
        grid_spec=pltpu.PrefetchScalarGridSpec(
            num_scalar_prefetch=3,
            grid=(p // tm, nf),
            in_specs=[pl.BlockSpec((tm, half), row_map, pipeline_mode=pl.Buffered(1)),
                      pl.BlockSpec((None, d, tf), lambda i, j, te, tr, nu: (te_idx(i, te, nu), 0, f_idx(i, j, nu))),
                      pl.BlockSpec((None, d, tf), lambda i, j, te, tr, nu: (te_idx(i, te, nu), 0, f_idx(i, j, nu))),
                      pl.BlockSpec((None, tf, d), lambda i, j, te, tr, nu: (te_idx(i, te, nu), f_idx(i, j, nu), 0))],
            out_specs=pl.BlockSpec((tm, d), lambda i, j, te, tr, nu: (i, 0), pipeline_mode=pl.Buffered(1)),
            scratch_shapes=[pltpu.VMEM((tm, d), BF16)]),
        compiler_params=_params("arbitrary", "arbitrary"),
        name="moe_ffn",
    )(tile_expert, tile_rows, n_used, xs, wg, wu, wd)


def _mlstm_gates_kernel(x_ref, g_ref, wt_ref, b_ref, o_ref):
    xn = _rms_rows(x_ref[...], g_ref[...])
    xh = xn.astype(BF16)
    xl = (xn - xh.astype(F32)).astype(BF16)
    w = wt_ref[...]
    wh = w.astype(BF16)
    wl = (w - wh.astype(F32)).astype(BF16)
    pre = _dot_nt(wh, xh) + (_dot_nt(wh, xl) + _dot_nt(wl, xh)) + b_ref[...]
    g = GATE_SOFTCAP * jnp.tanh(pre / GATE_SOFTCAP)
    log_sig = jnp.minimum(g, 0.0) - jnp.log1p(jnp.exp(-jnp.abs(g)))
    row = lax.broadcasted_iota(jnp.int32, g.shape, 0)
    is_forget = ((row >= MLSTM_HEADS) & (row < 2 * MLSTM_HEADS)) | (row >= 3 * MLSTM_HEADS)
    o_ref[...] = jnp.where(is_forget, log_sig, g)


def mlstm_gates(x, gain, w_gates_t, b_gates):
    t, d = x.shape
    ng = w_gates_t.shape[0]
    tt = min(ROUTER_TT, t)
    return pl.pallas_call(
        _mlstm_gates_kernel,
        out_shape=jax.ShapeDtypeStruct((ng, t), F32),
        grid=(t // tt,),
        in_specs=[pl.BlockSpec((tt, d), lambda i: (i, 0)),
                  pl.BlockSpec((1, d), lambda i: (0, 0)),
                  pl.BlockSpec((ng, d), lambda i: (0, 0)),
                  pl.BlockSpec((ng, 1), lambda i: (0, 0))],
        out_specs=pl.BlockSpec((ng, tt), lambda i: (0, i)),
        compiler_params=_params("arbitrary"),
        name="mlstm_gates",
    )(x, gain.reshape(1, d), w_gates_t, b_gates.reshape(ng, 1))


def _mlstm_chunk(q_ref, k_ref, v_ref, li_ref, lf_ref, h_ref, c_ref, n_ref, m_ref, slot, head, reverse):
    L = q_ref.shape[0]
    dk = c_ref.shape[1]
    dv = c_ref.shape[2]
    q = q_ref[:, head * dk:(head + 1) * dk] * jnp.asarray(dk ** -0.5, BF16)
    k = k_ref[:, head * dk:(head + 1) * dk]
    v = v_ref[:, head * dv:(head + 1) * dv]
    li = li_ref[head]
    lf = lf_ref[head]
    r_i = lax.broadcasted_iota(jnp.int32, (L, L), 0)
    c_i = lax.broadcasted_iota(jnp.int32, (L, L), 1)
    visible = (c_i >= r_i) if reverse else (c_i <= r_i)
    cum = (r_i >= c_i) if reverse else (r_i <= c_i)

    hi, mid, lo = (p.astype(F32) for p in _split3(lf))
    prow = lax.broadcasted_iota(jnp.int32, (16, L), 0)
    pieces = jnp.where(prow == 0, hi, jnp.where(prow == 1, mid, jnp.where(prow == 2, lo, 0.0)))
    b = jnp.sum(_dot(pieces.astype(BF16), jnp.where(cum, 1.0, 0.0).astype(BF16)), axis=0, keepdims=True)
    g = jnp.sum(lf, axis=-1, keepdims=True)
    u = li - b

    m_prev = m_ref[slot][:, :1]
    a_mat = jnp.where(visible, jnp.broadcast_to(u, (L, L)), -jnp.inf)
    m_row = jnp.maximum(jnp.max(a_mat, axis=-1, keepdims=True), m_prev)
    w_intra = (jnp.exp(a_mat - m_row) * _dot_nt(q, k)).astype(BF16)
    w_inter = jnp.exp(m_prev - m_row)

    ones = jnp.ones((L, LANES), BF16)
    num = w_inter * _dot(q, c_ref[slot].astype(BF16)) + _dot(w_intra, v)
    den = w_inter * _dot(q, n_ref[slot].astype(BF16)) + _dot(w_intra, ones)
    b_col = jnp.sum(jnp.where(r_i == c_i, jnp.broadcast_to(b, (L, L)), 0.0), axis=-1, keepdims=True)
    inv = 1.0 / jnp.maximum(jnp.abs(den), jnp.exp(-(b_col + m_row)))
    h_ref[:, head * dv:(head + 1) * dv] = (num * jnp.concatenate([inv] * (dv // LANES), axis=1)).astype(h_ref.dtype)

    m_new = g + jnp.maximum(m_prev, jnp.max(u, axis=-1, keepdims=True))
    decay = jnp.exp(g + m_prev - m_new)
    ks_t = (k.astype(F32).T * jnp.exp(g + u - m_new)).astype(BF16)
    c_ref[slot] = decay * c_ref[slot] + _dot(ks_t, v)
    n_ref[slot] = decay * n_ref[slot] + _dot(ks_t, ones)
    m_ref[slot] = jnp.broadcast_to(m_new, (1, LANES))


def _mlstm_kernel(qf, kf, vf, lif, lff, qb, kb, vb, lib, lfb, hf_ref, hb_ref, c_ref, n_ref, m_ref):
    @pl.when(pl.program_id(1) == 0)
    def _():
        c_ref[...] = jnp.zeros_like(c_ref)
        n_ref[...] = jnp.zeros_like(n_ref)
        m_ref[...] = jnp.full_like(m_ref, -jnp.inf)
    heads = lif.shape[0]
    for h in range(heads):
        _mlstm_chunk(qf, kf, vf, lif, lff, hf_ref, c_ref, n_ref, m_ref, h, h, False)
        _mlstm_chunk(qb, kb, vb, lib, lfb, hb_ref, c_ref, n_ref, m_ref, heads + h, h, True)


def mlstm_scan(proj, gates, batch, seq, d_model):
    heads, dk = MLSTM_HEADS, MLSTM_DK
    dv = d_model // heads
    L = min(MLSTM_CHUNK, seq)
    nc = seq // L
    t = batch * seq
    qw = heads * dk

    def fwd(b, j):
        return b * nc + j

    def bwd(b, j):
        return b * nc + (nc - 1 - j)

    def qkv_specs(pos):
        return [pl.BlockSpec((L, qw), lambda b, j: (pos(b, j), 0)),
                pl.BlockSpec((L, qw), lambda b, j: (pos(b, j), 1)),
                pl.BlockSpec((L, d_model), lambda b, j: (pos(b, j), 2 * qw // d_model))]

    def gate_spec(kind, pos):
        return pl.BlockSpec((heads, 1, L), lambda b, j: (kind, 0, pos(b, j)))

    return pl.pallas_call(
        _mlstm_kernel,
        out_shape=(jax.ShapeDtypeStruct((t, d_model), BF16), jax.ShapeDtypeStruct((t, d_model), BF16)),
        grid=(batch, nc),
        in_specs=(qkv_specs(fwd) + [gate_spec(0, fwd), gate_spec(1, fwd)]
                  + qkv_specs(bwd) + [gate_spec(2, bwd), gate_spec(3, bwd)]),
        out_specs=(pl.BlockSpec((L, d_model), lambda b, j: (fwd(b, j), 0)),
                   pl.BlockSpec((L, d_model), lambda b, j: (bwd(b, j), 0))),
        scratch_shapes=[pltpu.VMEM((2 * heads, dk, dv), F32), pltpu.VMEM((2 * heads, dk, LANES), F32),
                        pltpu.VMEM((2 * heads, 1, LANES), F32)],
        compiler_params=_params("arbitrary", "arbitrary"),
        name="mlstm_scan",
    )(proj, proj, proj, gates, gates, proj, proj, proj, gates, gates)


def _na_fused_kernel(q_ref, k0_ref, k1_ref, k2_ref, v0_ref, v1_ref, v2_ref, bias_ref, wo_ref, x_ref, o_ref):
    hd = NA_HEAD_DIM
    gw = NA_HEADS_PER_GROUP * hd
    for g in range(q_ref.shape[1] // gw):
        outs = []
        for h in range(g * NA_HEADS_PER_GROUP, (g + 1) * NA_HEADS_PER_GROUP):
            cols = slice(h * hd, (h + 1) * hd)
            k = jnp.concatenate([k0_ref[:, cols], k1_ref[:, cols], k2_ref[:, cols]], axis=0)
            v = jnp.concatenate([v0_ref[:, cols], v1_ref[:, cols], v2_ref[:, cols]], axis=0)
            s = _dot_nt(q_ref[:, cols], k) + bias_ref[0, h].astype(F32)
            p = jnp.exp(s - jnp.max(s, axis=-1, keepdims=True))
            o = _dot(p.astype(BF16), v) * (1.0 / jnp.sum(p, axis=-1, keepdims=True))
            outs.append(o.astype(BF16))
        part = _dot(jnp.concatenate(outs, axis=1), wo_ref[g * gw:(g + 1) * gw, :])
        if g == 0:
            o_ref[...] = x_ref[...] + part
        else:
            o_ref[...] += part


def _na_bias_tables(rpb, rows):
    rb, w = NA_ROWS_PER_BLOCK, GRID_W
    nb = rows // rb
    heads, n_dr, n_dc = rpb.shape
    period = 2 * w
    left = (w - 1) - (NA_KW - 1)
    vec = jnp.pad(rpb.astype(BF16), ((0, 0), (0, 0), (left, period - left - n_dc)))
    toep = jnp.tile(vec, (1, 1, w))[:, :, :w * (period - 1)].reshape(heads, n_dr, w, period - 1)[..., w - 1:]
    zero = jnp.zeros((heads, w, w), BF16)
    tables = []
    for blk in (0, 1, nb - 1):
        ws = min(max(blk - 1, 0), nb - 3) * rb
        r = blk * rb + np.arange(rb)[:, None, None, None]
        c = np.arange(w)[None, :, None, None]
        a = ws + np.arange(3 * rb)[None, None, :, None]
        kc = np.arange(w)[None, None, None, :]
        rs = np.clip(r - NA_KH // 2, 0, rows - NA_KH)
        cs = np.clip(c - NA_KW // 2, 0, w - NA_KW)
        ok = (a >= rs) & (a < rs + NA_KH) & (kc >= cs) & (kc < cs + NA_KW)
        ok = np.broadcast_to(ok, (rb, w, 3 * rb, w)).reshape(rb * w, 3 * rb * w)
        parts = []
        for qr in range(rb):
            drs = [ws + ar - (blk * rb + qr) + NA_KH - 1 for ar in range(3 * rb)]
            parts.append(jnp.stack([toep[:, dr] if 0 <= dr < n_dr else zero for dr in drs], axis=2))
        vals = jnp.stack(parts, axis=1).reshape(heads, rb * w, 3 * rb * w)
        tables.append(jnp.where(ok[None], vals, NEG_BIG))
    return jnp.stack(tables)


def na_attention_out(qkv, rpb, w_out, x, batch, seq):
    rb, w = NA_ROWS_PER_BLOCK, GRID_W
    t, d = x.shape
    rows = seq // w
    nb = rows // rb
    tq = rb * w
    bias = _na_bias_tables(rpb, rows)
    once = pl.Buffered(1)

    def kv_spec(part, off):
        return pl.BlockSpec((tq, d), lambda b, i: (b * nb + jnp.clip(i - 1, 0, nb - 3) + off, part))

    def bias_type(i):
        return jnp.where(i == 0, 0, jnp.where(i == nb - 1, 2, 1))

    return pl.pallas_call(
        _na_fused_kernel,
        out_shape=jax.ShapeDtypeStruct((t, d), F32),
        grid=(batch, nb),
        in_specs=([pl.BlockSpec((tq, d), lambda b, i: (b * nb + i, 0))]
                  + [kv_spec(1, off) for off in range(3)] + [kv_spec(2, off) for off in range(3)]
                  + [pl.BlockSpec((1, NA_HEADS, tq, 3 * tq), lambda b, i: (bias_type(i), 0, 0, 0), pipeline_mode=once),
                     pl.BlockSpec((d, d), lambda b, i: (0, 0), pipeline_mode=once),
                     pl.BlockSpec((tq, d), lambda b, i: (b * nb + i, 0))]),
        out_specs=pl.BlockSpec((tq, d), lambda b, i: (b * nb + i, 0)),
        compiler_params=_params("arbitrary", "arbitrary"),
        name="na_attention_out",
    )(qkv, qkv, qkv, qkv, qkv, qkv, qkv, bias, w_out, x)


def _router_kernel(x_ref, g_ref, wr_ref, xp_ref, meta_ref, cnt_ref, carry_ref):
    @pl.when(pl.program_id(0) == 0)
    def _():
        carry_ref[...] = jnp.zeros_like(carry_ref)

    tt = x_ref.shape[0]
    xn = _rms_rows(x_ref[...], g_ref[...])
    xh = xn.astype(BF16)
    xhf = xh.astype(F32)
    xl = (xn - xhf).astype(BF16)
    wr = wr_ref[...]
    wh = wr.astype(BF16)
    wl = (wr - wh.astype(F32)).astype(BF16)
    logits = _dot(xh, wh) + (_dot(xh, wl) + _dot(xl, wh))
    col = lax.broadcasted_iota(jnp.int32, logits.shape, 1).astype(F32)
    lg = jnp.where(col < N_EXPERTS, logits, -jnp.inf)
    m1 = jnp.max(lg, axis=-1, keepdims=True)
    i1 = jnp.min(jnp.where(lg == m1, col, float(LANES)), axis=-1, keepdims=True)
    lg2 = jnp.where(col == i1, -jnp.inf, lg)
    m2 = jnp.max(lg2, axis=-1, keepdims=True)
    i2 = jnp.min(jnp.where(lg2 == m2, col, float(LANES)), axis=-1, keepdims=True)
    e2 = jnp.exp(m2 - m1)
    w1 = 1.0 / (1.0 + e2)
    w2 = e2 * w1

    hit1, hit2 = col == i1, col == i2
    cnt = jnp.where(hit1 | hit2, 1.0, 0.0)
    r_i = lax.broadcasted_iota(jnp.int32, (tt, tt), 0)
    c_i = lax.broadcasted_iota(jnp.int32, (tt, tt), 1)
    before = jnp.where(c_i < r_i, 1.0, 0.0).astype(BF16)
    prefix = _dot(before, cnt.astype(BF16)) + carry_ref[...]
    rank1 = jnp.sum(jnp.where(hit1, prefix, 0.0), axis=-1, keepdims=True)
    rank2 = jnp.sum(jnp.where(hit2, prefix, 0.0), axis=-1, keepdims=True)
    carry_ref[...] += jnp.sum(cnt, axis=0, keepdims=True)
    cnt_ref[...] = carry_ref[...]

    meta = jnp.zeros(logits.shape, F32)
    for lane, val in enumerate((i1, i2, w1, w2, rank1, rank2)):
        meta = jnp.where(col == lane, val, meta)
    meta_ref[...] = meta
    half = xhf.shape[1] // 2
    xp_ref[...] = _pack_bf16_pairs(xhf[:, :half], xhf[:, half:])


def moe_router(x, gain, w_router):
    t, d = x.shape
    tt = min(ROUTER_TT, t)
    wr = jnp.zeros((d, LANES), F32).at[:, :N_EXPERTS].set(w_router)
    return pl.pallas_call(
        _router_kernel,
        out_shape=(jax.ShapeDtypeStruct((t, d // 2), U32), jax.ShapeDtypeStruct((t, LANES), F32),
                   jax.ShapeDtypeStruct((1, LANES), F32)),
        grid=(t // tt,),
        in_specs=[pl.BlockSpec((tt, d), lambda i: (i, 0)),
                  pl.BlockSpec((1, d), lambda i: (0, 0)),
                  pl.BlockSpec((d, LANES), lambda i: (0, 0))],
        out_specs=(pl.BlockSpec((tt, d // 2), lambda i: (i, 0)),
                   pl.BlockSpec((tt, LANES), lambda i: (i, 0)),
                   pl.BlockSpec((1, LANES), lambda i: (0, 0))),
        scratch_shapes=[pltpu.VMEM((1, LANES), F32)],
        compiler_params=_params("arbitrary"),
        name="moe_router",
    )(x, gain.reshape(1, d), wr)


def _scatter_kernel(pad_lo_ref, pad_hi_ref, tail_ref, p0_ref, p1_ref, src_ref, dst_ref, sem):
    tt = p0_ref.shape[2]

    def row_copy(src_row, dst_row):
        return pltpu.make_async_copy(src_ref.at[pl.ds(src_row, 1)], dst_ref.at[pl.ds(dst_row, 1)], sem)

    def block_copy(dst_block):
        return pltpu.make_async_copy(src_ref, dst_ref.at[pl.ds(pl.multiple_of(dst_block * tt, tt), tt)], sem)

    @pl.when(pl.program_id(0) == 0)
    def _():
        for e in range(N_EXPERTS):
            def pad_body(r, carry):
                row_copy(0, r).start()
                return carry
            lax.fori_loop(pad_lo_ref[e], pad_hi_ref[e], pad_body, 0)

            def pad_wait(r, carry):
                row_copy(0, 0).wait()
                return carry
            lax.fori_loop(pad_lo_ref[e], pad_hi_ref[e], pad_wait, 0)

        def tail_body(blk, carry):
            block_copy(blk).start()
            block_copy(blk).wait()
            return carry
        lax.fori_loop(tail_ref[0], tail_ref[1], tail_body, 0)

    def start_body(t, carry):
        row_copy(t, p0_ref[0, 0, t]).start()
        row_copy(t, p1_ref[0, 0, t]).start()
        return carry
    lax.fori_loop(0, tt, start_body, 0, unroll=8)
    block_copy(0).wait()
    block_copy(0).wait()


def moe_scatter_rows(xp, pos0, pos1, pad_lo, pad_hi, tail_blocks, n_rows):
    t, half = xp.shape
    tt = min(SCATTER_TT, t)
    nblk = t // tt
    smem_rows = pl.BlockSpec((1, 1, tt), lambda i, lo, hi, tl: (i, 0, 0), memory_space=pltpu.SMEM)
    return pl.pallas_call(
        _scatter_kernel,
        out_shape=jax.ShapeDtypeStruct((n_rows, half), xp.dtype),
        grid_spec=pltpu.PrefetchScalarGridSpec(
            num_scalar_prefetch=3,
            grid=(nblk,),
            in_specs=[smem_rows, smem_rows, pl.BlockSpec((tt, half), lambda i, lo, hi, tl: (i, 0))],
            out_specs=pl.BlockSpec(memory_space=pl.ANY),
            scratch_shapes=[pltpu.SemaphoreType.DMA]),
        compiler_params=pltpu.CompilerParams(dimension_semantics=("arbitrary",), has_side_effects=True),
        name="moe_scatter_rows",
    )(pad_lo, pad_hi, tail_blocks, pos0.reshape(nblk, 1, tt), pos1.reshape(nblk, 1, tt), xp)


def _combine_kernel(p0_ref, p1_ref, x_ref, meta_ref, y_ref, o_ref, buf_ref, sem):
    tt = x_ref.shape[0]

    def row_copy(k, src_row, t):
        return pltpu.make_async_copy(y_ref.at[pl.ds(src_row, 1)], buf_ref.at[k, pl.ds(t, 1)], sem)

    def start_body(t, carry):
        row_copy(0, p0_ref[0, 0, t], t).start(priority=0)
        row_copy(1, p1_ref[0, 0, t], t).start(priority=1)
        return carry
    lax.fori_loop(0, tt, start_body, 0, unroll=8)
    for k in range(TOP_K):
        pltpu.make_async_copy(y_ref.at[pl.ds(0, tt)], buf_ref.at[k], sem).wait()

    meta = meta_ref[...]
    o_ref[...] = x_ref[...] + (meta[:, 2:3] * buf_ref[0] + meta[:, 3:4] * buf_ref[1])


def moe_combine(x, meta, pos0, pos1, y):
    t, d = x.shape
    tt = min(COMBINE_TT, t)
    nblk = t // tt
    smem_rows = pl.BlockSpec((1, 1, tt), lambda i: (i, 0, 0), memory_space=pltpu.SMEM)
    return pl.pallas_call(
        _combine_kernel,
        out_shape=jax.ShapeDtypeStruct((t, d), F32),
        grid=(nblk,),
        in_specs=[smem_rows, smem_rows,
                  pl.BlockSpec((tt, d), lambda i: (i, 0)),
                  pl.BlockSpec((tt, LANES), lambda i: (i, 0)),
                  pl.BlockSpec(memory_space=pl.ANY)],
        out_specs=pl.BlockSpec((tt, d), lambda i: (i, 0)),
        scratch_shapes=[pltpu.VMEM((2, tt, d), F32), pltpu.SemaphoreType.DMA],
        compiler_params=_params("arbitrary"),
        name="moe_combine",
    )(pos0.reshape(nblk, 1, tt), pos1.reshape(nblk, 1, tt), x, meta, y)


def mlstm_layer(x, batch, seq, mix_norm, w_in, b_gates, head_norm, w_out):
    t, d = x.shape
    n_main = w_in.shape[1] - 4 * MLSTM_HEADS
    proj = norm_matmul(x, mix_norm, w_in.astype(BF16), BF16, n_main)
    gates = mlstm_gates(x, mix_norm, w_in[:, n_main:].T, b_gates)
    hf, hb = mlstm_scan(proj, gates.reshape(4 * MLSTM_HEADS, 1, t), batch, seq, d)
    return mlstm_out_proj(hf, hb, proj, head_norm, w_out.astype(BF16), x)


def na_layer(x, batch, seq, mix_norm, w_qkv, q_norm, k_norm, rpb, w_out):
    qkv = qkv_matmul(x, mix_norm, w_qkv.astype(BF16), q_norm, k_norm)
    return na_attention_out(qkv, rpb, w_out.astype(BF16), x, batch, seq)


def moe_layer(x, ffn_norm, w_router, w_gate, w_up, w_down):
    t, d = x.shape
    tm = MOE_TM
    xp, meta, counts = moe_router(x, ffn_norm, w_router)
    counts = counts[0, :N_EXPERTS].astype(jnp.int32)
    tiles = (counts + tm - 1) // tm
    tile_end = jnp.cumsum(tiles)
    row_start = (tile_end - tiles) * tm
    n_tiles = TOP_K * t // tm + N_EXPERTS
    tile_expert = jnp.minimum(jnp.sum(jnp.arange(n_tiles)[:, None] >= tile_end[None, :], axis=1),
                              N_EXPERTS - 1).astype(jnp.int32)
    n_used = tile_end[-1:].astype(jnp.int32)
    first_tile = (tile_end - tiles)[tile_expert]
    tile_rows = jnp.clip(counts[tile_expert] - (jnp.arange(n_tiles) - first_tile) * tm, 0, tm).astype(jnp.int32)
    e0, e1 = meta[:, 0].astype(jnp.int32), meta[:, 1].astype(jnp.int32)
    pos0 = row_start[e0] + meta[:, 4].astype(jnp.int32)
    pos1 = row_start[e1] + meta[:, 5].astype(jnp.int32)
    tt = min(SCATTER_TT, t)
    tail_blocks = jnp.stack([tile_end[-1] * tm // tt, n_tiles * tm // tt]).astype(jnp.int32)
    xs = moe_scatter_rows(xp, pos0, pos1, (row_start + counts).astype(jnp.int32),
                          (tile_end * tm).astype(jnp.int32), tail_blocks, n_tiles * tm)
    y = moe_ffn(tile_expert, tile_rows, n_used, xs, w_gate, w_up, w_down)
    return moe_combine(x, meta, pos0, pos1, y)


def kernel(x, l0_mix_norm, l0_mlstm_w_in, l0_mlstm_b_gates, l0_mlstm_head_norm, l0_mlstm_w_out, l0_ffn_norm, l0_ffn_w_gate, l0_ffn_w_up, l0_ffn_w_down, l1_mix_norm, l1_na_w_qkv, l1_na_q_norm, l1_na_k_norm, l1_na_rpb, l1_na_w_out, l1_ffn_norm, l1_moe_w_router, l1_moe_w_gate, l1_moe_w_up, l1_moe_w_down, l2_mix_norm, l2_mlstm_w_in, l2_mlstm_b_gates, l2_mlstm_head_norm, l2_mlstm_w_out, l2_ffn_norm, l2_ffn_w_gate, l2_ffn_w_up, l2_ffn_w_down, l3_mix_norm, l3_na_w_qkv, l3_na_q_norm, l3_na_k_norm, l3_na_rpb, l3_na_w_out, l3_ffn_norm, l3_moe_w_router, l3_moe_w_gate, l3_moe_w_up, l3_moe_w_down):
    batch, seq, d = x.shape
    h = x.reshape(batch * seq, d)
    h = mlstm_layer(h, batch, seq, l0_mix_norm, l0_mlstm_w_in, l0_mlstm_b_gates, l0_mlstm_head_norm, l0_mlstm_w_out)
    h = dense_ffn(h, l0_ffn_norm, l0_ffn_w_gate, l0_ffn_w_up, l0_ffn_w_down)
    h = na_layer(h, batch, seq, l1_mix_norm, l1_na_w_qkv, l1_na_q_norm, l1_na_k_norm, l1_na_rpb, l1_na_w_out)
    h = moe_layer(h, l1_ffn_norm, l1_moe_w_router, l1_moe_w_gate, l1_moe_w_up, l1_moe_w_down)
    h = mlstm_layer(h, batch, seq, l2_mix_norm, l2_mlstm_w_in, l2_mlstm_b_gates, l2_mlstm_head_norm, l2_mlstm_w_out)
    h = dense_ffn(h, l2_ffn_norm, l2_ffn_w_gate, l2_ffn_w_up, l2_ffn_w_down)
    h = na_layer(h, batch, seq, l3_mix_norm, l3_na_w_qkv, l3_na_q_norm, l3_na_k_norm, l3_na_rpb, l3_na_w_out)
    h = moe_layer(h, l3_ffn_norm, l3_moe_w_router, l3_moe_w_gate, l3_moe_w_up, l3_moe_w_down)
    return h.reshape(batch, seq, d)
```

```python
import functools

import jax
import jax.numpy as jnp
import numpy as np
from jax import lax
from jax.experimental import pallas as pl
from jax.experimental.pallas import tpu as pltpu

F32 = jnp.float32
BF16 = jnp.bfloat16
U32 = jnp.uint32

NORM_EPS = 1e-6
LANES = 128
VMEM_LIMIT_BYTES = 56 << 20
NEG_BIG = -1e30

MLSTM_HEADS = 4
MLSTM_DK = 256
GATE_SOFTCAP = 15.0
MLSTM_CHUNK = 256
GRID_W = 64
NA_HEADS = 16
NA_HEAD_DIM = 128
NA_KH = 8
NA_KW = 16
NA_ROWS_PER_BLOCK = 4
NA_HEADS_PER_GROUP = 4
N_EXPERTS = 8
TOP_K = 2

MM_TM, MM_TN = 1024, 1024
MLSTM_OUT_TM = 512
FFN_TM, FFN_TF = 1024, 512
MOE_TM, MOE_TF = 1024, 512
MOE_SUBTILES = 2
ROUTER_TT = 512
SCATTER_TT = 1024
COMBINE_TT = 256
NORM_ROWS = 128


def _params(*semantics):
    return pltpu.CompilerParams(dimension_semantics=semantics, vmem_limit_bytes=VMEM_LIMIT_BYTES)


def _dot(a, b):
    return jnp.dot(a, b, preferred_element_type=F32)


def _dot_nt(a, b):
    return lax.dot_general(a, b, (((1,), (1,)), ((), ())), preferred_element_type=F32)


def _split3(x):
    hi = x.astype(BF16)
    r1 = x - hi.astype(F32)
    mid = r1.astype(BF16)
    lo = (r1 - mid.astype(F32)).astype(BF16)
    return hi, mid, lo


def _rms_rows(x, gain):
    ms = jnp.mean(x * x, axis=-1, keepdims=True)
    return x * lax.rsqrt(ms + NORM_EPS) * gain


def _fill_rmsnorm(x_ref, g_ref, a_ref):
    def body(c, carry):
        r = pl.multiple_of(c * NORM_ROWS, NORM_ROWS)
        a_ref[pl.ds(r, NORM_ROWS), :] = _rms_rows(x_ref[pl.ds(r, NORM_ROWS), :], g_ref[...]).astype(BF16)
        return carry
    lax.fori_loop(0, x_ref.shape[0] // NORM_ROWS, body, 0)


def _norm_matmul_kernel(x_ref, g_ref, w_ref, o_ref, a_ref):
    @pl.when(pl.program_id(1) == 0)
    def _():
        _fill_rmsnorm(x_ref, g_ref, a_ref)
    o_ref[...] = _dot(a_ref[...], w_ref[...]).astype(o_ref.dtype)


def norm_matmul(x, gain, w, out_dtype, n_cols):
    m, k = x.shape
    n = n_cols
    tm, tn = min(MM_TM, m), min(MM_TN, n)
    return pl.pallas_call(
        _norm_matmul_kernel,
        out_shape=jax.ShapeDtypeStruct((m, n), out_dtype),
        grid=(m // tm, n // tn),
        in_specs=[pl.BlockSpec((tm, k), lambda i, j: (i, 0)),
                  pl.BlockSpec((1, k), lambda i, j: (0, 0)),
                  pl.BlockSpec((k, tn), lambda i, j: (0, j))],
        out_specs=pl.BlockSpec((tm, tn), lambda i, j: (i, j)),
        scratch_shapes=[pltpu.VMEM((tm, k), BF16)],
        compiler_params=_params("arbitrary", "arbitrary"),
        name="norm_matmul",
    )(x, gain.reshape(1, k), w)


def _qkv_matmul_kernel(x_ref, g_ref, w_ref, hg_ref, o_ref, a_ref, *, n_norm_tiles):
    j = pl.program_id(1)

    @pl.when(j == 0)
    def _():
        _fill_rmsnorm(x_ref, g_ref, a_ref)
    acc = _dot(a_ref[...], w_ref[...])

    @pl.when(j < n_norm_tiles)
    def _():
        hd = hg_ref.shape[2]
        mean_mat = jnp.full((hd, hd), 1.0 / hd, BF16)
        for g in range(o_ref.shape[1] // hd):
            cols = slice(g * hd, (g + 1) * hd)
            blk = acc[:, cols]
            ms = _dot((blk * blk).astype(BF16), mean_mat)
            o_ref[:, cols] = (blk * lax.rsqrt(ms + NORM_EPS) * hg_ref[0]).astype(o_ref.dtype)

    @pl.when(j >= n_norm_tiles)
    def _():
        o_ref[...] = acc.astype(o_ref.dtype)


def qkv_matmul(x, gain, w, q_norm, k_norm):
    m, k = x.shape
    n = w.shape[1]
    d = n // 3
    hd = NA_HEAD_DIM
    tm, tn = min(MM_TM, m), min(MM_TN, d)
    per = d // tn
    head_gain = jnp.concatenate([jnp.tile((q_norm * hd ** -0.5)[None], (per, 1)), jnp.tile(k_norm[None], (per, 1)),
                                 jnp.ones((per, hd), F32)]).reshape(3 * per, 1, hd)
    return pl.pallas_call(
        functools.partial(_qkv_matmul_kernel, n_norm_tiles=2 * per),
        out_shape=jax.ShapeDtypeStruct((m, n), BF16),
        grid=(m // tm, n // tn),
        in_specs=[pl.BlockSpec((tm, k), lambda i, j: (i, 0)),
                  pl.BlockSpec((1, k), lambda i, j: (0, 0)),
                  pl.BlockSpec((k, tn), lambda i, j: (0, j)),
                  pl.BlockSpec((1, 1, hd), lambda i, j: (j, 0, 0))],
        out_specs=pl.BlockSpec((tm, tn), lambda i, j: (i, j)),
        scratch_shapes=[pltpu.VMEM((tm, k), BF16)],
        compiler_params=_params("arbitrary", "arbitrary"),
        name="qkv_matmul",
    )(x, gain.reshape(1, k), w, head_gain)


def _mlstm_out_kernel(hf_ref, hb_ref, og_ref, g_ref, w_ref, r_ref, o_ref, a_ref, *, heads):
    @pl.when(pl.program_id(1) == 0)
    def _():
        dv = a_ref.shape[1] // heads

        def body(c, carry):
            r = pl.multiple_of(c * NORM_ROWS, NORM_ROWS)
            rows = pl.ds(r, NORM_ROWS)
            for h in range(heads):
                cols = slice(h * dv, (h + 1) * dv)
                hh = hf_ref[rows, cols].astype(F32) + hb_ref[rows, cols].astype(F32)
                y = _rms_rows(hh, g_ref[:, cols]) * jax.nn.sigmoid(og_ref[rows, cols].astype(F32))
                a_ref[rows, cols] = y.astype(BF16)
            return carry
        lax.fori_loop(0, a_ref.shape[0] // NORM_ROWS, body, 0)
    o_ref[...] = r_ref[...] + _dot(a_ref[...], w_ref[...])


def mlstm_out_proj(hf, hb, proj, head_norm, w, res):
    m, d = hf.shape
    n = w.shape[1]
    tm, tn = min(MLSTM_OUT_TM, m), n
    o_col_block = (proj.shape[1] - d) // d
    return pl.pallas_call(
        functools.partial(_mlstm_out_kernel, heads=MLSTM_HEADS),
        out_shape=jax.ShapeDtypeStruct((m, n), F32),
        grid=(m // tm, n // tn),
        in_specs=[pl.BlockSpec((tm, d), lambda i, j: (i, 0)),
                  pl.BlockSpec((tm, d), lambda i, j: (i, 0)),
                  pl.BlockSpec((tm, d), lambda i, j: (i, o_col_block)),
                  pl.BlockSpec((1, d), lambda i, j: (0, 0)),
                  pl.BlockSpec((d, tn), lambda i, j: (0, j), pipeline_mode=pl.Buffered(1)),
                  pl.BlockSpec((tm, tn), lambda i, j: (i, j))],
        out_specs=pl.BlockSpec((tm, tn), lambda i, j: (i, j)),
        scratch_shapes=[pltpu.VMEM((tm, d), BF16)],
        compiler_params=_params("arbitrary", "arbitrary"),
        name="mlstm_out_proj",
    )(hf, hb, proj, head_norm.reshape(1, d), w, res)


def _swiglu(a, wg, wu, wd):
    gate = _dot(a, wg)
    up = _dot(a, wu)
    hidden = (gate * jax.nn.sigmoid(gate) * up).astype(BF16)
    return _dot(hidden, wd)


def _ffn_kernel(x_ref, g_ref, wg_ref, wu_ref, wd_ref, o_ref, a_ref):
    @pl.when(pl.program_id(1) == 0)
    def _():
        _fill_rmsnorm(x_ref, g_ref, a_ref)
        o_ref[...] = x_ref[...]
    o_ref[...] += _swiglu(a_ref[...], wg_ref[...].astype(BF16), wu_ref[...].astype(BF16), wd_ref[...].astype(BF16))


def dense_ffn(x, gain, wg, wu, wd):
    m, d = x.shape
    f = wg.shape[1]
    tm, tf = min(FFN_TM, m), min(FFN_TF, f)
    return pl.pallas_call(
        _ffn_kernel,
        out_shape=jax.ShapeDtypeStruct((m, d), F32),
        grid=(m // tm, f // tf),
        in_specs=[pl.BlockSpec((tm, d), lambda i, j: (i, 0), pipeline_mode=pl.Buffered(1)),
                  pl.BlockSpec((1, d), lambda i, j: (0, 0)),
                  pl.BlockSpec((d, tf), lambda i, j: (0, j)),
                  pl.BlockSpec((d, tf), lambda i, j: (0, j)),
                  pl.BlockSpec((tf, d), lambda i, j: (j, 0))],
        out_specs=pl.BlockSpec((tm, d), lambda i, j: (i, 0), pipeline_mode=pl.Buffered(1)),
        scratch_shapes=[pltpu.VMEM((tm, d), BF16)],
        compiler_params=_params("arbitrary", "arbitrary"),
        name="dense_ffn",
    )(x, gain.reshape(1, d), wg, wu, wd)


def _unpack_bf16_pairs(words):
    hi = lax.bitcast_convert_type(words & jnp.uint32(0xFFFF0000), F32).astype(BF16)
    lo = lax.bitcast_convert_type(words << 16, F32).astype(BF16)
    return hi, lo


def _pack_bf16_pairs(hi, lo):
    return lax.bitcast_convert_type(hi, U32) | (lax.bitcast_convert_type(lo, U32) >> 16)


def _moe_ffn_kernel(te_ref, tr_ref, nu_ref, xs_ref, wg_ref, wu_ref, wd_ref, o_ref, a_ref):
    i, j = pl.program_id(0), pl.program_id(1)

    @pl.when(i < nu_ref[0])
    def _():
        @pl.when(j == 0)
        def _():
            half = a_ref.shape[1] // 2

            def body(c, carry):
                r = pl.multiple_of(c * NORM_ROWS, NORM_ROWS)
                hi, lo = _unpack_bf16_pairs(xs_ref[pl.ds(r, NORM_ROWS), :])
                a_ref[pl.ds(r, NORM_ROWS), :half] = hi
                a_ref[pl.ds(r, NORM_ROWS), half:] = lo
                return carry
            lax.fori_loop(0, a_ref.shape[0] // NORM_ROWS, body, 0)
            o_ref[...] = jnp.zeros_like(o_ref)

        wg, wu, wd = wg_ref[...].astype(BF16), wu_ref[...].astype(BF16), wd_ref[...].astype(BF16)
        sub = a_ref.shape[0] // MOE_SUBTILES
        rows_used = tr_ref[jnp.minimum(i, nu_ref[0] - 1)]
        for sb in range(MOE_SUBTILES):
            def sub_step(rows=slice(sb * sub, (sb + 1) * sub)):
                o_ref[rows, :] += _swiglu(a_ref[rows, :], wg, wu, wd)
            if sb == 0:
                sub_step()
            else:
                pl.when(rows_used > sb * sub)(sub_step)

    @pl.when((i >= nu_ref[0]) & (j == 0))
    def _():
        o_ref[...] = jnp.zeros_like(o_ref)


def moe_ffn(tile_expert, tile_rows, n_used, xs, wg, wu, wd):
    p, half = xs.shape
    d = 2 * half
    f = wg.shape[2]
    tm, tf = MOE_TM, min(MOE_TF, f)
    nf = f // tf

    def row_map(i, j, te, tr, nu):
        return (jnp.minimum(i, nu[0] - 1), 0)

    def f_idx(i, j, nu):
        return jnp.where(i < nu[0], j, nf - 1)

    def te_idx(i, te, nu):
        return te[jnp.minimum(i, nu[0] - 1)]

    return pl.pallas_call(
        _moe_ffn_kernel,
        out_shape=jax.ShapeDtypeStruct((p, d), F32),
        grid_spec=pltpu.PrefetchScalarGridSpec(
            num_scalar_prefetch=3,
            grid=(p // tm, nf),
            in_specs=[pl.BlockSpec((tm, half), row_map, pipeline_mode=pl.Buffered(1)),
                      pl.BlockSpec((None, d, tf), lambda i, j, te, tr, nu: (te_idx(i, te, nu), 0, f_idx(i, j, nu))),
                      pl.BlockSpec((None, d, tf), lambda i, j, te, tr, nu: (te_idx(i, te, nu), 0, f_idx(i, j, nu))),
                      pl.BlockSpec((None, tf, d), lambda i, j, te, tr, nu: (te_idx(i, te, nu), f_idx(i, j, nu), 0))],
            out_specs=pl.BlockSpec((tm, d), lambda i, j, te, tr, nu: (i, 0), pipeline_mode=pl.Buffered(1)),
            scratch_shapes=[pltpu.VMEM((tm, d), BF16)]),
        compiler_params=_params("arbitrary", "arbitrary"),
        name="moe_ffn",
    )(tile_expert, tile_rows, n_used, xs, wg, wu, wd)


def _mlstm_gates_kernel(x_ref, g_ref, wt_ref, b_ref, o_ref):
    xn = _rms_rows(x_ref[...], g_ref[...])
    xh = xn.astype(BF16)
    xl = (xn - xh.astype(F32)).astype(BF16)
    w = wt_ref[...]
    wh = w.astype(BF16)
    wl = (w - wh.astype(F32)).astype(BF16)
    pre = _dot_nt(wh, xh) + (_dot_nt(wh, xl) + _dot_nt(wl, xh)) + b_ref[...]
    g = GATE_SOFTCAP * jnp.tanh(pre / GATE_SOFTCAP)
    log_sig = jnp.minimum(g, 0.0) - jnp.log1p(jnp.exp(-jnp.abs(g)))
    row = lax.broadcasted_iota(jnp.int32, g.shape, 0)
    is_forget = ((row >= MLSTM_HEADS) & (row < 2 * MLSTM_HEADS)) | (row >= 3 * MLSTM_HEADS)
    o_ref[...] = jnp.where(is_forget, log_sig, g)


def mlstm_gates(x, gain, w_gates_t, b_gates):
    t, d = x.shape
    ng = w_gates_t.shape[0]
    tt = min(ROUTER_TT, t)
    return pl.pallas_call(
        _mlstm_gates_kernel,
        out_shape=jax.ShapeDtypeStruct((ng, t), F32),
        grid=(t // tt,),
        in_specs=[pl.BlockSpec((tt, d), lambda i: (i, 0)),
                  pl.BlockSpec((1, d), lambda i: (0, 0)),
                  pl.BlockSpec((ng, d), lambda i: (0, 0)),
                  pl.BlockSpec((ng, 1), lambda i: (0, 0))],
        out_specs=pl.BlockSpec((ng, tt), lambda i: (0, i)),
        compiler_params=_params("arbitrary"),
        name="mlstm_gates",
    )(x, gain.reshape(1, d), w_gates_t, b_gates.reshape(ng, 1))


def _mlstm_chunk(q_ref, k_ref, v_ref, li_ref, lf_ref, h_ref, c_ref, n_ref, m_ref, slot, head, reverse):
    L = q_ref.shape[0]
    dk = c_ref.shape[1]
    dv = c_ref.shape[2]
    q = q_ref[:, head * dk:(head + 1) * dk] * jnp.asarray(dk ** -0.5, BF16)
    k = k_ref[:, head * dk:(head + 1) * dk]
    v = v_ref[:, head * dv:(head + 1) * dv]
    li = li_ref[head]
    lf = lf_ref[head]
    r_i = lax.broadcasted_iota(jnp.int32, (L, L), 0)
    c_i = lax.broadcasted_iota(jnp.int32, (L, L), 1)
    visible = (c_i >= r_i) if reverse else (c_i <= r_i)
    cum = (r_i >= c_i) if reverse else (r_i <= c_i)

    hi, mid, lo = (p.astype(F32) for p in _split3(lf))
    prow = lax.broadcasted_iota(jnp.int32, (16, L), 0)
    pieces = jnp.where(prow == 0, hi, jnp.where(prow == 1, mid, jnp.where(prow == 2, lo, 0.0)))
    b = jnp.sum(_dot(pieces.astype(BF16), jnp.where(cum, 1.0, 0.0).astype(BF16)), axis=0, keepdims=True)
    g = jnp.sum(lf, axis=-1, keepdims=True)
    u = li - b

    m_prev = m_ref[slot][:, :1]
    a_mat = jnp.where(visible, jnp.broadcast_to(u, (L, L)), -jnp.inf)
    m_row = jnp.maximum(jnp.max(a_mat, axis=-1, keepdims=True), m_prev)
    w_intra = (jnp.exp(a_mat - m_row) * _dot_nt(q, k)).astype(BF16)
    w_inter = jnp.exp(m_prev - m_row)

    ones = jnp.ones((L, LANES), BF16)
    num = w_inter * _dot(q, c_ref[slot].astype(BF16)) + _dot(w_intra, v)
    den = w_inter * _dot(q, n_ref[slot].astype(BF16)) + _dot(w_intra, ones)
    b_col = jnp.sum(jnp.where(r_i == c_i, jnp.broadcast_to(b, (L, L)), 0.0), axis=-1, keepdims=True)
    inv = 1.0 / jnp.maximum(jnp.abs(den), jnp.exp(-(b_col + m_row)))
    h_ref[:, head * dv:(head + 1) * dv] = (num * jnp.concatenate([inv] * (dv // LANES), axis=1)).astype(h_ref.dtype)

    m_new = g + jnp.maximum(m_prev, jnp.max(u, axis=-1, keepdims=True))
    decay = jnp.exp(g + m_prev - m_new)
    ks_t = (k.astype(F32).T * jnp.exp(g + u - m_new)).astype(BF16)
    c_ref[slot] = decay * c_ref[slot] + _dot(ks_t, v)
    n_ref[slot] = decay * n_ref[slot] + _dot(ks_t, ones)
    m_ref[slot] = jnp.broadcast_to(m_new, (1, LANES))


def _mlstm_kernel(qf, kf, vf, lif, lff, qb, kb, vb, lib, lfb, hf_ref, hb_ref, c_ref, n_ref, m_ref):
    @pl.when(pl.program_id(1) == 0)
    def _():
        c_ref[...] = jnp.zeros_like(c_ref)
        n_ref[...] = jnp.zeros_like(n_ref)
        m_ref[...] = jnp.full_like(m_ref, -jnp.inf)
    heads = lif.shape[0]
    for h in range(heads):
        _mlstm_chunk(qf, kf, vf, lif, lff, hf_ref, c_ref, n_ref, m_ref, h, h, False)
        _mlstm_chunk(qb, kb, vb, lib, lfb, hb_ref, c_ref, n_ref, m_ref, heads + h, h, True)


def mlstm_scan(proj, gates, batch, seq, d_model):
    heads, dk = MLSTM_HEADS, MLSTM_DK
    dv = d_model // heads
    L = min(MLSTM_CHUNK, seq)
    nc = seq // L
    t = batch * seq
    qw = heads * dk

    def fwd(b, j):
        return b * nc + j

    def bwd(b, j):
        return b * nc + (nc - 1 - j)

    def qkv_specs(pos):
        return [pl.BlockSpec((L, qw), lambda b, j: (pos(b, j), 0)),
                pl.BlockSpec((L, qw), lambda b, j: (pos(b, j), 1)),
                pl.BlockSpec((L, d_model), lambda b, j: (pos(b, j), 2 * qw // d_model))]

    def gate_spec(kind, pos):
        return pl.BlockSpec((heads, 1, L), lambda b, j: (kind, 0, pos(b, j)))

    return pl.pallas_call(
        _mlstm_kernel,
        out_shape=(jax.ShapeDtypeStruct((t, d_model), BF16), jax.ShapeDtypeStruct((t, d_model), BF16)),
        grid=(batch, nc),
        in_specs=(qkv_specs(fwd) + [gate_spec(0, fwd), gate_spec(1, fwd)]
                  + qkv_specs(bwd) + [gate_spec(2, bwd), gate_spec(3, bwd)]),
        out_specs=(pl.BlockSpec((L, d_model), lambda b, j: (fwd(b, j), 0)),
                   pl.BlockSpec((L, d_model), lambda b, j: (bwd(b, j), 0))),
        scratch_shapes=[pltpu.VMEM((2 * heads, dk, dv), F32), pltpu.VMEM((2 * heads, dk, LANES), F32),
                        pltpu.VMEM((2 * heads, 1, LANES), F32)],
        compiler_params=_params("arbitrary", "arbitrary"),
        name="mlstm_scan",
    )(proj, proj, proj, gates, gates, proj, proj, proj, gates, gates)


def _na_fused_kernel(q_ref, k0_ref, k1_ref, k2_ref, v0_ref, v1_ref, v2_ref, bias_ref, wo_ref, x_ref, o_ref):
    hd = NA_HEAD_DIM
    gw = NA_HEADS_PER_GROUP * hd
    for g in range(q_ref.shape[1] // gw):
        outs = []
        for h in range(g * NA_HEADS_PER_GROUP, (g + 1) * NA_HEADS_PER_GROUP):
            cols = slice(h * hd, (h + 1) * hd)
            k = jnp.concatenate([k0_ref[:, cols], k1_ref[:, cols], k2_ref[:, cols]], axis=0)
            v = jnp.concatenate([v0_ref[:, cols], v1_ref[:, cols], v2_ref[:, cols]], axis=0)
            s = _dot_nt(q_ref[:, cols], k) + bias_ref[0, h].astype(F32)
            p = jnp.exp(s - jnp.max(s, axis=-1, keepdims=True))
            o = _dot(p.astype(BF16), v) * (1.0 / jnp.sum(p, axis=-1, keepdims=True))
            outs.append(o.astype(BF16))
        part = _dot(jnp.concatenate(outs, axis=1), wo_ref[g * gw:(g + 1) * gw, :])
        if g == 0:
            o_ref[...] = x_ref[...] + part
        else:
            o_ref[...] += part


def _na_bias_tables(rpb, rows):
    rb, w = NA_ROWS_PER_BLOCK, GRID_W
    nb = rows // rb
    heads, n_dr, n_dc = rpb.shape
    period = 2 * w
    left = (w - 1) - (NA_KW - 1)
    vec = jnp.pad(rpb.astype(BF16), ((0, 0), (0, 0), (left, period - left - n_dc)))
    toep = jnp.tile(vec, (1, 1, w))[:, :, :w * (period - 1)].reshape(heads, n_dr, w, period - 1)[..., w - 1:]
    zero = jnp.zeros((heads, w, w), BF16)
    tables = []
    for blk in (0, 1, nb - 1):
        ws = min(max(blk - 1, 0), nb - 3) * rb
        r = blk * rb + np.arange(rb)[:, None, None, None]
        c = np.arange(w)[None, :, None, None]
        a = ws + np.arange(3 * rb)[None, None, :, None]
        kc = np.arange(w)[None, None, None, :]
        rs = np.clip(r - NA_KH // 2, 0, rows - NA_KH)
        cs = np.clip(c - NA_KW // 2, 0, w - NA_KW)
        ok = (a >= rs) & (a < rs + NA_KH) & (kc >= cs) & (kc < cs + NA_KW)
        ok = np.broadcast_to(ok, (rb, w, 3 * rb, w)).reshape(rb * w, 3 * rb * w)
        parts = []
        for qr in range(rb):
            drs = [ws + ar - (blk * rb + qr) + NA_KH - 1 for ar in range(3 * rb)]
            parts.append(jnp.stack([toep[:, dr] if 0 <= dr < n_dr else zero for dr in drs], axis=2))
        vals = jnp.stack(parts, axis=1).reshape(heads, rb * w, 3 * rb * w)
        tables.append(jnp.where(ok[None], vals, NEG_BIG))
    return jnp.stack(tables)


def na_attention_out(qkv, rpb, w_out, x, batch, seq):
    rb, w = NA_ROWS_PER_BLOCK, GRID_W
    t, d = x.shape
    rows = seq // w
    nb = rows // rb
    tq = rb * w
    bias = _na_bias_tables(rpb, rows)
    once = pl.Buffered(1)

    def kv_spec(part, off):
        return pl.BlockSpec((tq, d), lambda b, i: (b * nb + jnp.clip(i - 1, 0, nb - 3) + off, part))

    def bias_type(i):
        return jnp.where(i == 0, 0, jnp.where(i == nb - 1, 2, 1))

    return pl.pallas_call(
        _na_fused_kernel,
        out_shape=jax.ShapeDtypeStruct((t, d), F32),
        grid=(batch, nb),
        in_specs=([pl.BlockSpec((tq, d), lambda b, i: (b * nb + i, 0))]
                  + [kv_spec(1, off) for off in range(3)] + [kv_spec(2, off) for off in range(3)]
                  + [pl.BlockSpec((1, NA_HEADS, tq, 3 * tq), lambda b, i: (bias_type(i), 0, 0, 0), pipeline_mode=once),
                     pl.BlockSpec((d, d), lambda b, i: (0, 0), pipeline_mode=once),
                     pl.BlockSpec((tq, d), lambda b, i: (b * nb + i, 0))]),
        out_specs=pl.BlockSpec((tq, d), lambda b, i: (b * nb + i, 0)),
        compiler_params=_params("arbitrary", "arbitrary"),
        name="na_attention_out",
    )(qkv, qkv, qkv, qkv, qkv, qkv, qkv, bias, w_out, x)


def _router_kernel(x_ref, g_ref, wr_ref, xp_ref, meta_ref, cnt_ref, carry_ref):
    @pl.when(pl.program_id(0) == 0)
    def _():
        carry_ref[...] = jnp.zeros_like(carry_ref)

    tt = x_ref.shape[0]
    xn = _rms_rows(x_ref[...], g_ref[...])
    xh = xn.astype(BF16)
    xhf = xh.astype(F32)
    xl = (xn - xhf).astype(BF16)
    wr = wr_ref[...]
    wh = wr.astype(BF16)
    wl = (wr - wh.astype(F32)).astype(BF16)
    logits = _dot(xh, wh) + (_dot(xh, wl) + _dot(xl, wh))
    col = lax.broadcasted_iota(jnp.int32, logits.shape, 1).astype(F32)
    lg = jnp.where(col < N_EXPERTS, logits, -jnp.inf)
    m1 = jnp.max(lg, axis=-1, keepdims=True)
    i1 = jnp.min(jnp.where(lg == m1, col, float(LANES)), axis=-1, keepdims=True)
    lg2 = jnp.where(col == i1, -jnp.inf, lg)
    m2 = jnp.max(lg2, axis=-1, keepdims=True)
    i2 = jnp.min(jnp.where(lg2 == m2, col, float(LANES)), axis=-1, keepdims=True)
    e2 = jnp.exp(m2 - m1)
    w1 = 1.0 / (1.0 + e2)
    w2 = e2 * w1

    hit1, hit2 = col == i1, col == i2
    cnt = jnp.where(hit1 | hit2, 1.0, 0.0)
    r_i = lax.broadcasted_iota(jnp.int32, (tt, tt), 0)
    c_i = lax.broadcasted_iota(jnp.int32, (tt, tt), 1)
    before = jnp.where(c_i < r_i, 1.0, 0.0).astype(BF16)
    prefix = _dot(before, cnt.astype(BF16)) + carry_ref[...]
    rank1 = jnp.sum(jnp.where(hit1, prefix, 0.0), axis=-1, keepdims=True)
    rank2 = jnp.sum(jnp.where(hit2, prefix, 0.0), axis=-1, keepdims=True)
    carry_ref[...] += jnp.sum(cnt, axis=0, keepdims=True)
    cnt_ref[...] = carry_ref[...]

    meta = jnp.zeros(logits.shape, F32)
    for lane, val in enumerate((i1, i2, w1, w2, rank1, rank2)):
        meta = jnp.where(col == lane, val, meta)
    meta_ref[...] = meta
    half = xhf.shape[1] // 2
    xp_ref[...] = _pack_bf16_pairs(xhf[:, :half], xhf[:, half:])


def moe_router(x, gain, w_router):
    t, d = x.shape
    tt = min(ROUTER_TT, t)
    wr = jnp.zeros((d, LANES), F32).at[:, :N_EXPERTS].set(w_router)
    return pl.pallas_call(
        _router_kernel,
        out_shape=(jax.ShapeDtypeStruct((t, d // 2), U32), jax.ShapeDtypeStruct((t, LANES), F32),
                   jax.ShapeDtypeStruct((1, LANES), F32)),
        grid=(t // tt,),
        in_specs=[pl.BlockSpec((tt, d), lambda i: (i, 0)),
                  pl.BlockSpec((1, d), lambda i: (0, 0)),
                  pl.BlockSpec((d, LANES), lambda i: (0, 0))],
        out_specs=(pl.BlockSpec((tt, d // 2), lambda i: (i, 0)),
                   pl.BlockSpec((tt, LANES), lambda i: (i, 0)),
                   pl.BlockSpec((1, LANES), lambda i: (0, 0))),
        scratch_shapes=[pltpu.VMEM((1, LANES), F32)],
        compiler_params=_params("arbitrary"),
        name="moe_router",
    )(x, gain.reshape(1, d), wr)


def _scatter_kernel(pad_lo_ref, pad_hi_ref, tail_ref, p0_ref, p1_ref, src_ref, dst_ref, sem):
    tt = p0_ref.shape[2]

    def row_copy(src_row, dst_row):
        return pltpu.make_async_copy(src_ref.at[pl.ds(src_row, 1)], dst_ref.at[pl.ds(dst_row, 1)], sem)

    def block_copy(dst_block):
        return pltpu.make_async_copy(src_ref, dst_ref.at[pl.ds(pl.multiple_of(dst_block * tt, tt), tt)], sem)

    @pl.when(pl.program_id(0) == 0)
    def _():
        for e in range(N_EXPERTS):
            def pad_body(r, carry):
                row_copy(0, r).start()
                return carry
            lax.fori_loop(pad_lo_ref[e], pad_hi_ref[e], pad_body, 0)

            def pad_wait(r, carry):
                row_copy(0, 0).wait()
                return carry
            lax.fori_loop(pad_lo_ref[e], pad_hi_ref[e], pad_wait, 0)

        def tail_body(blk, carry):
            block_copy(blk).start()
            block_copy(blk).wait()
            return carry
        lax.fori_loop(tail_ref[0], tail_ref[1], tail_body, 0)

    def start_body(t, carry):
        row_copy(t, p0_ref[0, 0, t]).start()
        row_copy(t, p1_ref[0, 0, t]).start()
        return carry
    lax.fori_loop(0, tt, start_body, 0, unroll=8)
    block_copy(0).wait()
    block_copy(0).wait()


def moe_scatter_rows(xp, pos0, pos1, pad_lo, pad_hi, tail_blocks, n_rows):
    t, half = xp.shape
    tt = min(SCATTER_TT, t)
    nblk = t // tt
    smem_rows = pl.BlockSpec((1, 1, tt), lambda i, lo, hi, tl: (i, 0, 0), memory_space=pltpu.SMEM)
    return pl.pallas_call(
        _scatter_kernel,
        out_shape=jax.ShapeDtypeStruct((n_rows, half), xp.dtype),
        grid_spec=pltpu.PrefetchScalarGridSpec(
            num_scalar_prefetch=3,
            grid=(nblk,),
            in_specs=[smem_rows, smem_rows, pl.BlockSpec((tt, half), lambda i, lo, hi, tl: (i, 0))],
            out_specs=pl.BlockSpec(memory_space=pl.ANY),
            scratch_shapes=[pltpu.SemaphoreType.DMA]),
        compiler_params=pltpu.CompilerParams(dimension_semantics=("arbitrary",), has_side_effects=True),
        name="moe_scatter_rows",
    )(pad_lo, pad_hi, tail_blocks, pos0.reshape(nblk, 1, tt), pos1.reshape(nblk, 1, tt), xp)


def _combine_kernel(p0_ref, p1_ref, x_ref, meta_ref, y_ref, o_ref, buf_ref, sem):
    tt = x_ref.shape[0]

    def row_copy(k, src_row, t):
        return pltpu.make_async_copy(y_ref.at[pl.ds(src_row, 1)], buf_ref.at[k, pl.ds(t, 1)], sem)

    def start_body(t, carry):
        row_copy(0, p0_ref[0, 0, t], t).start(priority=0)
        row_copy(1, p1_ref[0, 0, t], t).start(priority=1)
        return carry
    lax.fori_loop(0, tt, start_body, 0, unroll=8)
    for k in range(TOP_K):
        pltpu.make_async_copy(y_ref.at[pl.ds(0, tt)], buf_ref.at[k], sem).wait()

    meta = meta_ref[...]
    o_ref[...] = x_ref[...] + (meta[:, 2:3] * buf_ref[0] + meta[:, 3:4] * buf_ref[1])


def moe_combine(x, meta, pos0, pos1, y):
    t, d = x.shape
    tt = min(COMBINE_TT, t)
    nblk = t // tt
    smem_rows = pl.BlockSpec((1, 1, tt), lambda i: (i, 0, 0), memory_space=pltpu.SMEM)
    return pl.pallas_call(
        _combine_kernel,
        out_shape=jax.ShapeDtypeStruct((t, d), F32),
        grid=(nblk,),
        in_specs=[smem_rows, smem_rows,
                  pl.BlockSpec((tt, d), lambda i: (i, 0)),
                  pl.BlockSpec((tt, LANES), lambda i: (i, 0)),
                  pl.BlockSpec(memory_space=pl.ANY)],
        out_specs=pl.BlockSpec((tt, d), lambda i: (i, 0)),
        scratch_shapes=[pltpu.VMEM((2, tt, d), F32), pltpu.SemaphoreType.DMA],
        compiler_params=_params("arbitrary"),
        name="moe_combine",
    )(pos0.reshape(nblk, 1, tt), pos1.reshape(nblk, 1, tt), x, meta, y)


def mlstm_layer(x, batch, seq, mix_norm, w_in, b_gates, head_norm, w_out):
    t, d = x.shape
    n_main = w_in.shape[1] - 4 * MLSTM_HEADS
    proj = norm_matmul(x, mix_norm, w_in.astype(BF16), BF16, n_main)
    gates = mlstm_gates(x, mix_norm, w_in[:, n_main:].T, b_gates)
    hf, hb = mlstm_scan(proj, gates.reshape(4 * MLSTM_HEADS, 1, t), batch, seq, d)
    return mlstm_out_proj(hf, hb, proj, head_norm, w_out.astype(BF16), x)


def na_layer(x, batch, seq, mix_norm, w_qkv, q_norm, k_norm, rpb, w_out):
    qkv = qkv_matmul(x, mix_norm, w_qkv.astype(BF16), q_norm, k_norm)
    return na_attention_out(qkv, rpb, w_out.astype(BF16), x, batch, seq)


def moe_layer(x, ffn_norm, w_router, w_gate, w_up, w_down):
    t, d = x.shape
    tm = MOE_TM
    xp, meta, counts = moe_router(x, ffn_norm, w_router)
    counts = counts[0, :N_EXPERTS].astype(jnp.int32)
    tiles = (counts + tm - 1) // tm
    tile_end = jnp.cumsum(tiles)
    row_start = (tile_end - tiles) * tm
    n_tiles = TOP_K * t // tm + N_EXPERTS
    tile_expert = jnp.minimum(jnp.sum(jnp.arange(n_tiles)[:, None] >= tile_end[None, :], axis=1),
                              N_EXPERTS - 1).astype(jnp.int32)
    n_used = tile_end[-1:].astype(jnp.int32)
    first_tile = (tile_end - tiles)[tile_expert]
    tile_rows = jnp.clip(counts[tile_expert] - (jnp.arange(n_tiles) - first_tile) * tm, 0, tm).astype(jnp.int32)
    e0, e1 = meta[:, 0].astype(jnp.int32), meta[:, 1].astype(jnp.int32)
    pos0 = row_start[e0] + meta[:, 4].astype(jnp.int32)
    pos1 = row_start[e1] + meta[:, 5].astype(jnp.int32)
    tt = min(SCATTER_TT, t)
    tail_blocks = jnp.stack([tile_end[-1] * tm // tt, n_tiles * tm // tt]).astype(jnp.int32)
    xs = moe_scatter_rows(xp, pos0, pos1, (row_start + counts).astype(jnp.int32),
                          (tile_end * tm).astype(jnp.int32), tail_blocks, n_tiles * tm)
    y = moe_ffn(tile_expert, tile_rows, n_used, xs, w_gate, w_up, w_down)
    return moe_combine(x, meta, pos0, pos1, y)


def kernel(x, l0_mix_norm, l0_mlstm_w_in, l0_mlstm_b_gates, l0_mlstm_head_norm, l0_mlstm_w_out, l0_ffn_norm, l0_ffn_w_gate, l0_ffn_w_up, l0_ffn_w_down, l1_mix_norm, l1_na_w_qkv, l1_na_q_norm, l1_na_k_norm, l1_na_rpb, l1_na_w_out, l1_ffn_norm, l1_moe_w_router, l1_moe_w_gate, l1_moe_w_up, l1_moe_w_down, l2_mix_norm, l2_mlstm_w_in, l2_mlstm_b_gates, l2_mlstm_head_norm, l2_mlstm_w_out, l2_ffn_norm, l2_ffn_w_gate, l2_ffn_w_up, l2_ffn_w_down, l3_mix_norm, l3_na_w_qkv, l3_na_q_norm, l3_na_k_norm, l3_na_rpb, l3_na_w_out, l3_ffn_norm, l3_moe_w_router, l3_moe_w_gate, l3_moe_w_up, l3_moe_w_down):
    batch, seq, d = x.shape
    h = x.reshape(batch * seq, d)
    h = mlstm_layer(h, batch, seq, l0_mix_norm, l0_mlstm_w_in, l0_mlstm_b_gates, l0_mlstm_head_norm, l0_mlstm_w_out)
    h = dense_ffn(h, l0_ffn_norm, l0_ffn_w_gate, l0_ffn_w_up, l0_ffn_w_down)
    h = na_layer(h, batch, seq, l1_mix_norm, l1_na_w_qkv, l1_na_q_norm, l1_na_k_norm, l1_na_rpb, l1_na_w_out)
    h = moe_layer(h, l1_ffn_norm, l1_moe_w_router, l1_moe_w_gate, l1_moe_w_up, l1_moe_w_down)
    h = mlstm_layer(h, batch, seq, l2_mix_norm, l2_mlstm_w_in, l2_mlstm_b_gates, l2_mlstm_head_norm, l2_mlstm_w_out)
    h = dense_ffn(h, l2_ffn_norm, l2_ffn_w_gate, l2_ffn_w_up, l2_ffn_w_down)
    h = na_layer(h, batch, seq, l3_mix_norm, l3_na_w_qkv, l3_na_q_norm, l3_na_k_norm, l3_na_rpb, l3_na_w_out)
    h = moe_layer(h, l3_ffn_norm, l3_moe_w_router, l3_moe_w_gate, l3_moe_w_up, l3_moe_w_down)
    return h.reshape(batch, seq, d)
```

```python
import functools

import jax
import jax.numpy as jnp
import numpy as np
from jax import lax
from jax.experimental import pallas as pl
from jax.experimental.pallas import tpu as pltpu

F32 = jnp.float32
BF16 = jnp.bfloat16
U32 = jnp.uint32

NORM_EPS = 1e-6
LANES = 128
VMEM_LIMIT_BYTES = 56 << 20
NEG_BIG = -1e30

MLSTM_HEADS = 4
MLSTM_DK = 256
GATE_SOFTCAP = 15.0
MLSTM_CHUNK = 256
GRID_W = 64
NA_HEADS = 16
NA_HEAD_DIM = 128
NA_KH = 8
NA_KW = 16
NA_ROWS_PER_BLOCK = 4
NA_HEADS_PER_GROUP = 4
N_EXPERTS = 8
TOP_K = 2

MM_TM, MM_TN = 1024, 1024
MLSTM_OUT_TM = 512
FFN_TM, FFN_TF = 1024, 512
MOE_TM, MOE_TF = 1024, 512
MOE_SUBTILES = 2
ROUTER_TT = 512
SCATTER_TT = 1024
COMBINE_TT = 512
NORM_ROWS = 128


def _params(*semantics):
    return pltpu.CompilerParams(dimension_semantics=semantics, vmem_limit_bytes=VMEM_LIMIT_BYTES)


def _dot(a, b):
    return jnp.dot(a, b, preferred_element_type=F32)


def _dot_nt(a, b):
    return lax.dot_general(a, b, (((1,), (1,)), ((), ())), preferred_element_type=F32)


def _split3(x):
    hi = x.astype(BF16)
    r1 = x - hi.astype(F32)
    mid = r1.astype(BF16)
    lo = (r1 - mid.astype(F32)).astype(BF16)
    return hi, mid, lo


def _rms_rows(x, gain):
    ms = jnp.mean(x * x, axis=-1, keepdims=True)
    return x * lax.rsqrt(ms + NORM_EPS) * gain


def _fill_rmsnorm(x_ref, g_ref, a_ref):
    def body(c, carry):
        r = pl.multiple_of(c * NORM_ROWS, NORM_ROWS)
        a_ref[pl.ds(r, NORM_ROWS), :] = _rms_rows(x_ref[pl.ds(r, NORM_ROWS), :], g_ref[...]).astype(BF16)
        return carry
    lax.fori_loop(0, x_ref.shape[0] // NORM_ROWS, body, 0)


def _qkv_matmul_kernel(x_ref, g_ref, w_ref, hg_ref, o_ref, a_ref, *, n_norm_tiles):
    j = pl.program_id(1)

    @pl.when(j == 0)
    def _():
        _fill_rmsnorm(x_ref, g_ref, a_ref)
    acc = _dot(a_ref[...], w_ref[...])

    @pl.when(j < n_norm_tiles)
    def _():
        hd = hg_ref.shape[2]
        mean_mat = jnp.full((hd, hd), 1.0 / hd, BF16)
        for g in range(o_ref.shape[1] // hd):
            cols = slice(g * hd, (g + 1) * hd)
            blk = acc[:, cols]
            ms = _dot((blk * blk).astype(BF16), mean_mat)
            o_ref[:, cols] = (blk * lax.rsqrt(ms + NORM_EPS) * hg_ref[0]).astype(o_ref.dtype)

    @pl.when(j >= n_norm_tiles)
    def _():
        o_ref[...] = acc.astype(o_ref.dtype)


def qkv_matmul(x, gain, w, q_norm, k_norm):
    m, k = x.shape
    n = w.shape[1]
    d = n // 3
    hd = NA_HEAD_DIM
    tm, tn = min(MM_TM, m), min(MM_TN, d)
    per = d // tn
    head_gain = jnp.concatenate([jnp.tile((q_norm * hd ** -0.5)[None], (per, 1)), jnp.tile(k_norm[None], (per, 1)),
                                 jnp.ones((per, hd), F32)]).reshape(3 * per, 1, hd)
    return pl.pallas_call(
        functools.partial(_qkv_matmul_kernel, n_norm_tiles=2 * per),
        out_shape=jax.ShapeDtypeStruct((m, n), BF16),
        grid=(m // tm, n // tn),
        in_specs=[pl.BlockSpec((tm, k), lambda i, j: (i, 0)),
                  pl.BlockSpec((1, k), lambda i, j: (0, 0)),
                  pl.BlockSpec((k, tn), lambda i, j: (0, j)),
                  pl.BlockSpec((1, 1, hd), lambda i, j: (j, 0, 0))],
        out_specs=pl.BlockSpec((tm, tn), lambda i, j: (i, j)),
        scratch_shapes=[pltpu.VMEM((tm, k), BF16)],
        compiler_params=_params("arbitrary", "arbitrary"),
        name="qkv_matmul",
    )(x, gain.reshape(1, k), w, head_gain)


def _mlstm_out_kernel(hf_ref, hb_ref, og_ref, g_ref, w_ref, r_ref, o_ref, a_ref, *, heads):
    @pl.when(pl.program_id(1) == 0)
    def _():
        dv = a_ref.shape[1] // heads

        def body(c, carry):
            r = pl.multiple_of(c * NORM_ROWS, NORM_ROWS)
            rows = pl.ds(r, NORM_ROWS)
            for h in range(heads):
                cols = slice(h * dv, (h + 1) * dv)
                hh = hf_ref[rows, cols].astype(F32) + hb_ref[rows, cols].astype(F32)
                y = _rms_rows(hh, g_ref[:, cols]) * jax.nn.sigmoid(og_ref[rows, cols].astype(F32))
                a_ref[rows, cols] = y.astype(BF16)
            return carry
        lax.fori_loop(0, a_ref.shape[0] // NORM_ROWS, body, 0)
    o_ref[...] = r_ref[...] + _dot(a_ref[...], w_ref[...])


def mlstm_out_proj(hf, hb, proj, head_norm, w, res):
    m, d = hf.shape
    n = w.shape[1]
    tm, tn = min(MLSTM_OUT_TM, m), n
    o_col_block = (proj.shape[1] - d) // d
    return pl.pallas_call(
        functools.partial(_mlstm_out_kernel, heads=MLSTM_HEADS),
        out_shape=jax.ShapeDtypeStruct((m, n), F32),
        grid=(m // tm, n // tn),
        in_specs=[pl.BlockSpec((tm, d), lambda i, j: (i, 0)),
                  pl.BlockSpec((tm, d), lambda i, j: (i, 0)),
                  pl.BlockSpec((tm, d), lambda i, j: (i, o_col_block)),
                  pl.BlockSpec((1, d), lambda i, j: (0, 0)),
                  pl.BlockSpec((d, tn), lambda i, j: (0, j), pipeline_mode=pl.Buffered(1)),
                  pl.BlockSpec((tm, tn), lambda i, j: (i, j))],
        out_specs=pl.BlockSpec((tm, tn), lambda i, j: (i, j)),
        scratch_shapes=[pltpu.VMEM((tm, d), BF16)],
        compiler_params=_params("arbitrary", "arbitrary"),
        name="mlstm_out_proj",
    )(hf, hb, proj, head_norm.reshape(1, d), w, res)


def _swiglu(a, wg, wu, wd):
    gate = _dot(a, wg)
    up = _dot(a, wu)
    hidden = (gate * jax.nn.sigmoid(gate) * up).astype(BF16)
    return _dot(hidden, wd)


def _ffn_kernel(x_ref, g_ref, wg_ref, wu_ref, wd_ref, o_ref, a_ref):
    @pl.when(pl.program_id(1) == 0)
    def _():
        _fill_rmsnorm(x_ref, g_ref, a_ref)
        o_ref[...] = x_ref[...]
    o_ref[...] += _swiglu(a_ref[...], wg_ref[...].astype(BF16), wu_ref[...].astype(BF16), wd_ref[...].astype(BF16))


def dense_ffn(x, gain, wg, wu, wd):
    m, d = x.shape
    f = wg.shape[1]
    tm, tf = min(FFN_TM, m), min(FFN_TF, f)
    return pl.pallas_call(
        _ffn_kernel,
        out_shape=jax.ShapeDtypeStruct((m, d), F32),
        grid=(m // tm, f // tf),
        in_specs=[pl.BlockSpec((tm, d), lambda i, j: (i, 0), pipeline_mode=pl.Buffered(1)),
                  pl.BlockSpec((1, d), lambda i, j: (0, 0)),
                  pl.BlockSpec((d, tf), lambda i, j: (0, j)),
                  pl.BlockSpec((d, tf), lambda i, j: (0, j)),
                  pl.BlockSpec((tf, d), lambda i, j: (j, 0))],
        out_specs=pl.BlockSpec((tm, d), lambda i, j: (i, 0), pipeline_mode=pl.Buffered(1)),
        scratch_shapes=[pltpu.VMEM((tm, d), BF16)],
        compiler_params=_params("arbitrary", "arbitrary"),
        name="dense_ffn",
    )(x, gain.reshape(1, d), wg, wu, wd)


def _unpack_bf16_pairs(words):
    hi = lax.bitcast_convert_type(words & jnp.uint32(0xFFFF0000), F32).astype(BF16)
    lo = lax.bitcast_convert_type(words << 16, F32).astype(BF16)
    return hi, lo


def _pack_bf16_pairs(hi, lo):
    return lax.bitcast_convert_type(hi, U32) | (lax.bitcast_convert_type(lo, U32) >> 16)


def _moe_ffn_kernel(te_ref, tr_ref, nu_ref, xs_ref, wg_ref, wu_ref, wd_ref, o_ref, a_ref):
    i, j = pl.program_id(0), pl.program_id(1)

    @pl.when(i < nu_ref[0])
    def _():
        @pl.when(j == 0)
        def _():
            half = a_ref.shape[1] // 2

            def body(c, carry):
                r = pl.multiple_of(c * NORM_ROWS, NORM_ROWS)
                hi, lo = _unpack_bf16_pairs(xs_ref[pl.ds(r, NORM_ROWS), :])
                a_ref[pl.ds(r, NORM_ROWS), :half] = hi
                a_ref[pl.ds(r, NORM_ROWS), half:] = lo
                return carry
            lax.fori_loop(0, a_ref.shape[0] // NORM_ROWS, body, 0)
            o_ref[...] = jnp.zeros_like(o_ref)

        wg, wu, wd = wg_ref[...].astype(BF16), wu_ref[...].astype(BF16), wd_ref[...].astype(BF16)
        sub = a_ref.shape[0] // MOE_SUBTILES
        rows_used = tr_ref[jnp.minimum(i, nu_ref[0] - 1)]
        for sb in range(MOE_SUBTILES):
            def sub_step(rows=slice(sb * sub, (sb + 1) * sub)):
                o_ref[rows, :] += _swiglu(a_ref[rows, :], wg, wu, wd)
            if sb == 0:
                sub_step()
            else:
                pl.when(rows_used > sb * sub)(sub_step)

    @pl.when((i >= nu_ref[0]) & (j == 0))
    def _():
        o_ref[...] = jnp.zeros_like(o_ref)


def moe_ffn(tile_expert, tile_rows, n_used, xs, wg, wu, wd):
    p, half = xs.shape
    d = 2 * half
    f = wg.shape[2]
    tm, tf = MOE_TM, min(MOE_TF, f)
    nf = f // tf

    def row_map(i, j, te, tr, nu):
        return (jnp.minimum(i, nu[0] - 1), 0)

    def f_idx(i, j, nu):
        return jnp.where(i < nu[0], j, nf - 1)

    def te_idx(i, te, nu):
        return te[jnp.minimum(i, nu[0] - 1)]

    return pl.pallas_call(
        _moe_ffn_kernel,
        out_shape=jax.ShapeDtypeStruct((p, d), F32),
        grid_spec=pltpu.PrefetchScalarGridSpec(
            num_scalar_prefetch=3,
            grid=(p // tm, nf),
            in_specs=[pl.BlockSpec((tm, half), row_map, pipeline_mode=pl.Buffered(1)),
                      pl.BlockSpec((None, d, tf), lambda i, j, te, tr, nu: (te_idx(i, te, nu), 0, f_idx(i, j, nu))),
                      pl.BlockSpec((None, d, tf), lambda i, j, te, tr, nu: (te_idx(i, te, nu), 0, f_idx(i, j, nu))),
                      pl.BlockSpec((None, tf, d), lambda i, j, te, tr, nu: (te_idx(i, te, nu), f_idx(i, j, nu), 0))],
            out_specs=pl.BlockSpec((tm, d), lambda i, j, te, tr, nu: (i, 0), pipeline_mode=pl.Buffered(1)),
            scratch_shapes=[pltpu.VMEM((tm, d), BF16)]),
        compiler_params=_params("arbitrary", "arbitrary"),
        name="moe_ffn",
    )(tile_expert, tile_rows, n_used, xs, wg, wu, wd)


def _gate_rows(xn, wt_ref, b_ref):
    xh = xn.astype(BF16)
    xl = (xn - xh.astype(F32)).astype(BF16)
    w = wt_ref[...]
    wh = w.astype(BF16)
    wl = (w - wh.astype(F32)).astype(BF16)
    pre = _dot_nt(wh, xh) + (_dot_nt(wh, xl) + _dot_nt(wl, xh)) + b_ref[...]
    g = GATE_SOFTCAP * jnp.tanh(pre / GATE_SOFTCAP)
    log_sig = jnp.minimum(g, 0.0) - jnp.log1p(jnp.exp(-jnp.abs(g)))
    row = lax.broadcasted_iota(jnp.int32, g.shape, 0)
    is_forget = ((row >= MLSTM_HEADS) & (row < 2 * MLSTM_HEADS)) | (row >= 3 * MLSTM_HEADS)
    return jnp.where(is_forget, log_sig, g)


def _mlstm_in_kernel(x_ref, g_ref, w_ref, wt_ref, b_ref, o_ref, gates_ref, a_ref):
    @pl.when(pl.program_id(1) == 0)
    def _():
        for c in range(x_ref.shape[0] // NORM_ROWS):
            rows = slice(c * NORM_ROWS, (c + 1) * NORM_ROWS)
            xn = _rms_rows(x_ref[rows, :], g_ref[...])
            a_ref[rows, :] = xn.astype(BF16)
            gates_ref[:, rows] = _gate_rows(xn, wt_ref, b_ref)
    o_ref[...] = _dot(a_ref[...], w_ref[...]).astype(o_ref.dtype)


def mlstm_in_proj(x, gain, w, n_cols, w_gates_t, b_gates):
    m, k = x.shape
    ng = w_gates_t.shape[0]
    tm, tn = min(MM_TM, m), min(MM_TN, n_cols)
    return pl.pallas_call(
        _mlstm_in_kernel,
        out_shape=(jax.ShapeDtypeStruct((m, n_cols), BF16), jax.ShapeDtypeStruct((ng, m), F32)),
        grid=(m // tm, n_cols // tn),
        in_specs=[pl.BlockSpec((tm, k), lambda i, j: (i, 0)),
                  pl.BlockSpec((1, k), lambda i, j: (0, 0)),
                  pl.BlockSpec((k, tn), lambda i, j: (0, j)),
                  pl.BlockSpec((ng, k), lambda i, j: (0, 0)),
                  pl.BlockSpec((ng, 1), lambda i, j: (0, 0))],
        out_specs=(pl.BlockSpec((tm, tn), lambda i, j: (i, j)),
                   pl.BlockSpec((ng, tm), lambda i, j: (0, i))),
        scratch_shapes=[pltpu.VMEM((tm, k), BF16)],
        compiler_params=_params("arbitrary", "arbitrary"),
        name="mlstm_in_proj",
    )(x, gain.reshape(1, k), w, w_gates_t, b_gates.reshape(ng, 1))


def _mlstm_chunk(q_ref, k_ref, v_ref, li_ref, lf_ref, h_ref, c_ref, n_ref, m_ref, slot, head, reverse):
    L = q_ref.shape[0]
    dk = c_ref.shape[1]
    dv = c_ref.shape[2]
    q = q_ref[:, head * dk:(head + 1) * dk] * jnp.asarray(dk ** -0.5, BF16)
    k = k_ref[:, head * dk:(head + 1) * dk]
    v = v_ref[:, head * dv:(head + 1) * dv]
    li = li_ref[head]
    lf = lf_ref[head]
    r_i = lax.broadcasted_iota(jnp.int32, (L, L), 0)
    c_i = lax.broadcasted_iota(jnp.int32, (L, L), 1)
    visible = (c_i >= r_i) if reverse else (c_i <= r_i)
    cum = (r_i >= c_i) if reverse else (r_i <= c_i)

    hi, mid, lo = (p.astype(F32) for p in _split3(lf))
    prow = lax.broadcasted_iota(jnp.int32, (16, L), 0)
    pieces = jnp.where(prow == 0, hi, jnp.where(prow == 1, mid, jnp.where(prow == 2, lo, 0.0)))
    b = jnp.sum(_dot(pieces.astype(BF16), jnp.where(cum, 1.0, 0.0).astype(BF16)), axis=0, keepdims=True)
    g = jnp.sum(lf, axis=-1, keepdims=True)
    u = li - b

    m_prev = m_ref[slot][:, :1]
    a_mat = jnp.where(visible, jnp.broadcast_to(u, (L, L)), -jnp.inf)
    m_row = jnp.maximum(jnp.max(a_mat, axis=-1, keepdims=True), m_prev)
    w_intra = (jnp.exp(a_mat - m_row) * _dot_nt(q, k)).astype(BF16)
    w_inter = jnp.exp(m_prev - m_row)

    ones = jnp.ones((L, LANES), BF16)
    num = w_inter * _dot(q, c_ref[slot].astype(BF16)) + _dot(w_intra, v)
    den = w_inter * _dot(q, n_ref[slot].astype(BF16)) + _dot(w_intra, ones)
    b_col = jnp.sum(jnp.where(r_i == c_i, jnp.broadcast_to(b, (L, L)), 0.0), axis=-1, keepdims=True)
    inv = 1.0 / jnp.maximum(jnp.abs(den), jnp.exp(-(b_col + m_row)))
    h_ref[:, head * dv:(head + 1) * dv] = (num * jnp.concatenate([inv] * (dv // LANES), axis=1)).astype(h_ref.dtype)

    m_new = g + jnp.maximum(m_prev, jnp.max(u, axis=-1, keepdims=True))
    decay = jnp.exp(g + m_prev - m_new)
    ks_t = (k.astype(F32).T * jnp.exp(g + u - m_new)).astype(BF16)
    c_ref[slot] = decay * c_ref[slot] + _dot(ks_t, v)
    n_ref[slot] = decay * n_ref[slot] + _dot(ks_t, ones)
    m_ref[slot] = jnp.broadcast_to(m_new, (1, LANES))


def _mlstm_kernel(qf, kf, vf, lif, lff, qb, kb, vb, lib, lfb, hf_ref, hb_ref, c_ref, n_ref, m_ref):
    @pl.when(pl.program_id(1) == 0)
    def _():
        c_ref[...] = jnp.zeros_like(c_ref)
        n_ref[...] = jnp.zeros_like(n_ref)
        m_ref[...] = jnp.full_like(m_ref, -jnp.inf)
    heads = lif.shape[0]
    for h in range(heads):
        _mlstm_chunk(qf, kf, vf, lif, lff, hf_ref, c_ref, n_ref, m_ref, h, h, False)
        _mlstm_chunk(qb, kb, vb, lib, lfb, hb_ref, c_ref, n_ref, m_ref, heads + h, h, True)


def mlstm_scan(proj, gates, batch, seq, d_model):
    heads, dk = MLSTM_HEADS, MLSTM_DK
    dv = d_model // heads
    L = min(MLSTM_CHUNK, seq)
    nc = seq // L
    t = batch * seq
    qw = heads * dk

    def fwd(b, j):
        return b * nc + j

    def bwd(b, j):
        return b * nc + (nc - 1 - j)

    def qkv_specs(pos):
        return [pl.BlockSpec((L, qw), lambda b, j: (pos(b, j), 0)),
                pl.BlockSpec((L, qw), lambda b, j: (pos(b, j), 1)),
                pl.BlockSpec((L, d_model), lambda b, j: (pos(b, j), 2 * qw // d_model))]

    def gate_spec(kind, pos):
        return pl.BlockSpec((heads, 1, L), lambda b, j: (kind, 0, pos(b, j)))

    return pl.pallas_call(
        _mlstm_kernel,
        out_shape=(jax.ShapeDtypeStruct((t, d_model), BF16), jax.ShapeDtypeStruct((t, d_model), BF16)),
        grid=(batch, nc),
        in_specs=(qkv_specs(fwd) + [gate_spec(0, fwd), gate_spec(1, fwd)]
                  + qkv_specs(bwd) + [gate_spec(2, bwd), gate_spec(3, bwd)]),
        out_specs=(pl.BlockSpec((L, d_model), lambda b, j: (fwd(b, j), 0)),
                   pl.BlockSpec((L, d_model), lambda b, j: (bwd(b, j), 0))),
        scratch_shapes=[pltpu.VMEM((2 * heads, dk, dv), F32), pltpu.VMEM((2 * heads, dk, LANES), F32),
                        pltpu.VMEM((2 * heads, 1, LANES), F32)],
        compiler_params=_params("arbitrary", "arbitrary"),
        name="mlstm_scan",
    )(proj, proj, proj, gates, gates, proj, proj, proj, gates, gates)


def _na_fused_kernel(q_ref, k0_ref, k1_ref, k2_ref, v0_ref, v1_ref, v2_ref, bias_ref, wo_ref, x_ref, o_ref):
    hd = NA_HEAD_DIM
    gw = NA_HEADS_PER_GROUP * hd
    for g in range(q_ref.shape[1] // gw):
        outs = []
        for h in range(g * NA_HEADS_PER_GROUP, (g + 1) * NA_HEADS_PER_GROUP):
            cols = slice(h * hd, (h + 1) * hd)
            k = jnp.concatenate([k0_ref[:, cols], k1_ref[:, cols], k2_ref[:, cols]], axis=0)
            v = jnp.concatenate([v0_ref[:, cols], v1_ref[:, cols], v2_ref[:, cols]], axis=0)
            s = _dot_nt(q_ref[:, cols], k) + bias_ref[0, h].astype(F32)
            p = jnp.exp(s - jnp.max(s, axis=-1, keepdims=True))
            o = _dot(p.astype(BF16), v) * (1.0 / jnp.sum(p, axis=-1, keepdims=True))
            outs.append(o.astype(BF16))
        part = _dot(jnp.concatenate(outs, axis=1), wo_ref[g * gw:(g + 1) * gw, :])
        if g == 0:
            o_ref[...] = x_ref[...] + part
        else:
            o_ref[...] += part


def _na_bias_tables(rpb, rows):
    rb, w = NA_ROWS_PER_BLOCK, GRID_W
    nb = rows // rb
    heads, n_dr, n_dc = rpb.shape
    period = 2 * w
    left = (w - 1) - (NA_KW - 1)
    vec = jnp.pad(rpb.astype(BF16), ((0, 0), (0, 0), (left, period - left - n_dc)))
    toep = jnp.tile(vec, (1, 1, w))[:, :, :w * (period - 1)].reshape(heads, n_dr, w, period - 1)[..., w - 1:]
    zero = jnp.zeros((heads, w, w), BF16)
    tables = []
    for blk in (0, 1, nb - 1):
        ws = min(max(blk - 1, 0), nb - 3) * rb
        r = blk * rb + np.arange(rb)[:, None, None, None]
        c = np.arange(w)[None, :, None, None]
        a = ws + np.arange(3 * rb)[None, None, :, None]
        kc = np.arange(w)[None, None, None, :]
        rs = np.clip(r - NA_KH // 2, 0, rows - NA_KH)
        cs = np.clip(c - NA_KW // 2, 0, w - NA_KW)
        ok = (a >= rs) & (a < rs + NA_KH) & (kc >= cs) & (kc < cs + NA_KW)
        ok = np.broadcast_to(ok, (rb, w, 3 * rb, w)).reshape(rb * w, 3 * rb * w)
        parts = []
        for qr in range(rb):
            drs = [ws + ar - (blk * rb + qr) + NA_KH - 1 for ar in range(3 * rb)]
            parts.append(jnp.stack([toep[:, dr] if 0 <= dr < n_dr else zero for dr in drs], axis=2))
        vals = jnp.stack(parts, axis=1).reshape(heads, rb * w, 3 * rb * w)
        tables.append(jnp.where(ok[None], vals, NEG_BIG))
    return jnp.stack(tables)


def na_attention_out(qkv, rpb, w_out, x, batch, seq):
    rb, w = NA_ROWS_PER_BLOCK, GRID_W
    t, d = x.shape
    rows = seq // w
    nb = rows // rb
    tq = rb * w
    bias = _na_bias_tables(rpb, rows)
    once = pl.Buffered(1)

    def kv_spec(part, off):
        return pl.BlockSpec((tq, d), lambda b, i: (b * nb + jnp.clip(i - 1, 0, nb - 3) + off, part))

    def bias_type(i):
        return jnp.where(i == 0, 0, jnp.where(i == nb - 1, 2, 1))

    return pl.pallas_call(
        _na_fused_kernel,
        out_shape=jax.ShapeDtypeStruct((t, d), F32),
        grid=(batch, nb),
        in_specs=([pl.BlockSpec((tq, d), lambda b, i: (b * nb + i, 0))]
                  + [kv_spec(1, off) for off in range(3)] + [kv_spec(2, off) for off in range(3)]
                  + [pl.BlockSpec((1, NA_HEADS, tq, 3 * tq), lambda b, i: (bias_type(i), 0, 0, 0), pipeline_mode=once),
                     pl.BlockSpec((d, d), lambda b, i: (0, 0), pipeline_mode=once),
                     pl.BlockSpec((tq, d), lambda b, i: (b * nb + i, 0))]),
        out_specs=pl.BlockSpec((tq, d), lambda b, i: (b * nb + i, 0)),
        compiler_params=_params("arbitrary", "arbitrary"),
        name="na_attention_out",
    )(qkv, qkv, qkv, qkv, qkv, qkv, qkv, bias, w_out, x)


def _router_kernel(x_ref, g_ref, wr_ref, xp_ref, meta_ref, cnt_ref, carry_ref):
    @pl.when(pl.program_id(0) == 0)
    def _():
        carry_ref[...] = jnp.zeros_like(carry_ref)

    tt = x_ref.shape[0]
    xn = _rms_rows(x_ref[...], g_ref[...])
    xh = xn.astype(BF16)
    xhf = xh.astype(F32)
    xl = (xn - xhf).astype(BF16)
    wr = wr_ref[...]
    wh = wr.astype(BF16)
    wl = (wr - wh.astype(F32)).astype(BF16)
    logits = _dot(xh, wh) + (_dot(xh, wl) + _dot(xl, wh))
    col = lax.broadcasted_iota(jnp.int32, logits.shape, 1).astype(F32)
    lg = jnp.where(col < N_EXPERTS, logits, -jnp.inf)
    m1 = jnp.max(lg, axis=-1, keepdims=True)
    i1 = jnp.min(jnp.where(lg == m1, col, float(LANES)), axis=-1, keepdims=True)
    lg2 = jnp.where(col == i1, -jnp.inf, lg)
    m2 = jnp.max(lg2, axis=-1, keepdims=True)
    i2 = jnp.min(jnp.where(lg2 == m2, col, float(LANES)), axis=-1, keepdims=True)
    e2 = jnp.exp(m2 - m1)
    w1 = 1.0 / (1.0 + e2)
    w2 = e2 * w1

    hit1, hit2 = col == i1, col == i2
    cnt = jnp.where(hit1 | hit2, 1.0, 0.0)
    r_i = lax.broadcasted_iota(jnp.int32, (tt, tt), 0)
    c_i = lax.broadcasted_iota(jnp.int32, (tt, tt), 1)
    before = jnp.where(c_i < r_i, 1.0, 0.0).astype(BF16)
    prefix = _dot(before, cnt.astype(BF16)) + carry_ref[...]
    rank1 = jnp.sum(jnp.where(hit1, prefix, 0.0), axis=-1, keepdims=True)
    rank2 = jnp.sum(jnp.where(hit2, prefix, 0.0), axis=-1, keepdims=True)
    carry_ref[...] += jnp.sum(cnt, axis=0, keepdims=True)
    cnt_ref[...] = carry_ref[...]

    meta = jnp.zeros(logits.shape, F32)
    for lane, val in enumerate((i1, i2, w1, w2, rank1, rank2)):
        meta = jnp.where(col == lane, val, meta)
    meta_ref[...] = meta
    half = xhf.shape[1] // 2
    xp_ref[...] = _pack_bf16_pairs(xhf[:, :half], xhf[:, half:])


def moe_router(x, gain, w_router):
    t, d = x.shape
    tt = min(ROUTER_TT, t)
    wr = jnp.zeros((d, LANES), F32).at[:, :N_EXPERTS].set(w_router)
    return pl.pallas_call(
        _router_kernel,
        out_shape=(jax.ShapeDtypeStruct((t, d // 2), U32), jax.ShapeDtypeStruct((t, LANES), F32),
                   jax.ShapeDtypeStruct((1, LANES), F32)),
        grid=(t // tt,),
        in_specs=[pl.BlockSpec((tt, d), lambda i: (i, 0)),
                  pl.BlockSpec((1, d), lambda i: (0, 0)),
                  pl.BlockSpec((d, LANES), lambda i: (0, 0))],
        out_specs=(pl.BlockSpec((tt, d // 2), lambda i: (i, 0)),
                   pl.BlockSpec((tt, LANES), lambda i: (i, 0)),
                   pl.BlockSpec((1, LANES), lambda i: (0, 0))),
        scratch_shapes=[pltpu.VMEM((1, LANES), F32)],
        compiler_params=_params("arbitrary"),
        name="moe_router",
    )(x, gain.reshape(1, d), wr)


def _scatter_kernel(pad_lo_ref, pad_hi_ref, tail_ref, p0_ref, p1_ref, src_ref, dst_ref, sem):
    tt = p0_ref.shape[2]

    def row_copy(src_row, dst_row):
        return pltpu.make_async_copy(src_ref.at[pl.ds(src_row, 1)], dst_ref.at[pl.ds(dst_row, 1)], sem)

    def block_copy(dst_block):
        return pltpu.make_async_copy(src_ref, dst_ref.at[pl.ds(pl.multiple_of(dst_block * tt, tt), tt)], sem)

    @pl.when(pl.program_id(0) == 0)
    def _():
        for e in range(N_EXPERTS):
            def pad_body(r, carry):
                row_copy(0, r).start()
                return carry
            lax.fori_loop(pad_lo_ref[e], pad_hi_ref[e], pad_body, 0)

            def pad_wait(r, carry):
                row_copy(0, 0).wait()
                return carry
            lax.fori_loop(pad_lo_ref[e], pad_hi_ref[e], pad_wait, 0)

        def tail_body(blk, carry):
            block_copy(blk).start()
            block_copy(blk).wait()
            return carry
        lax.fori_loop(tail_ref[0], tail_ref[1], tail_body, 0)

    def start_body(t, carry):
        row_copy(t, p0_ref[0, 0, t]).start()
        row_copy(t, p1_ref[0, 0, t]).start()
        return carry
    lax.fori_loop(0, tt, start_body, 0, unroll=8)
    block_copy(0).wait()
    block_copy(0).wait()


def moe_scatter_rows(xp, pos0, pos1, pad_lo, pad_hi, tail_blocks, n_rows):
    t, half = xp.shape
    tt = min(SCATTER_TT, t)
    nblk = t // tt
    smem_rows = pl.BlockSpec((1, 1, tt), lambda i, lo, hi, tl: (i, 0, 0), memory_space=pltpu.SMEM)
    return pl.pallas_call(
        _scatter_kernel,
        out_shape=jax.ShapeDtypeStruct((n_rows, half), xp.dtype),
        grid_spec=pltpu.PrefetchScalarGridSpec(
            num_scalar_prefetch=3,
            grid=(nblk,),
            in_specs=[smem_rows, smem_rows, pl.BlockSpec((tt, half), lambda i, lo, hi, tl: (i, 0))],
            out_specs=pl.BlockSpec(memory_space=pl.ANY),
            scratch_shapes=[pltpu.SemaphoreType.DMA]),
        compiler_params=pltpu.CompilerParams(dimension_semantics=("arbitrary",), has_side_effects=True),
        name="moe_scatter_rows",
    )(pad_lo, pad_hi, tail_blocks, pos0.reshape(nblk, 1, tt), pos1.reshape(nblk, 1, tt), xp)


def _combine_kernel(p0_ref, p1_ref, x_ref, meta_ref, y_ref, o_ref, buf_ref, sem):
    tt = x_ref.shape[0]

    def row_copy(k, src_row, t):
        return pltpu.make_async_copy(y_ref.at[pl.ds(src_row, 1)], buf_ref.at[k, pl.ds(t, 1)], sem)

    def start_body(t, carry):
        row_copy(0, p0_ref[0, 0, t], t).start(priority=0)
        row_copy(1, p1_ref[0, 0, t], t).start(priority=1)
        return carry
    lax.fori_loop(0, tt, start_body, 0, unroll=8)
    for k in range(TOP_K):
        pltpu.make_async_copy(y_ref.at[pl.ds(0, tt)], buf_ref.at[k], sem).wait()

    meta = meta_ref[...]
    o_ref[...] = x_ref[...] + (meta[:, 2:3] * buf_ref[0] + meta[:, 3:4] * buf_ref[1])


def moe_combine(x, meta, pos0, pos1, y):
    t, d = x.shape
    tt = min(COMBINE_TT, t)
    nblk = t // tt
    smem_rows = pl.BlockSpec((1, 1, tt), lambda i: (i, 0, 0), memory_space=pltpu.SMEM)
    return pl.pallas_call(
        _combine_kernel,
        out_shape=jax.ShapeDtypeStruct((t, d), F32),
        grid=(nblk,),
        in_specs=[smem_rows, smem_rows,
                  pl.BlockSpec((tt, d), lambda i: (i, 0)),
                  pl.BlockSpec((tt, LANES), lambda i: (i, 0)),
                  pl.BlockSpec(memory_space=pl.ANY)],
        out_specs=pl.BlockSpec((tt, d), lambda i: (i, 0)),
        scratch_shapes=[pltpu.VMEM((2, tt, d), F32), pltpu.SemaphoreType.DMA],
        compiler_params=_params("arbitrary"),
        name="moe_combine",
    )(pos0.reshape(nblk, 1, tt), pos1.reshape(nblk, 1, tt), x, meta, y)


def mlstm_layer(x, batch, seq, mix_norm, w_in, b_gates, head_norm, w_out):
    t, d = x.shape
    n_main = w_in.shape[1] - 4 * MLSTM_HEADS
    proj, gates = mlstm_in_proj(x, mix_norm, w_in.astype(BF16), n_main, w_in[:, n_main:].T, b_gates)
    hf, hb = mlstm_scan(proj, gates.reshape(4 * MLSTM_HEADS, 1, t), batch, seq, d)
    return mlstm_out_proj(hf, hb, proj, head_norm, w_out.astype(BF16), x)


def na_layer(x, batch, seq, mix_norm, w_qkv, q_norm, k_norm, rpb, w_out):
    qkv = qkv_matmul(x, mix_norm, w_qkv.astype(BF16), q_norm, k_norm)
    return na_attention_out(qkv, rpb, w_out.astype(BF16), x, batch, seq)


def moe_layer(x, ffn_norm, w_router, w_gate, w_up, w_down):
    t, d = x.shape
    tm = MOE_TM
    xp, meta, counts = moe_router(x, ffn_norm, w_router)
    counts = counts[0, :N_EXPERTS].astype(jnp.int32)
    tiles = (counts + tm - 1) // tm
    tile_end = jnp.cumsum(tiles)
    row_start = (tile_end - tiles) * tm
    n_tiles = TOP_K * t // tm + N_EXPERTS
    tile_expert = jnp.minimum(jnp.sum(jnp.arange(n_tiles)[:, None] >= tile_end[None, :], axis=1),
                              N_EXPERTS - 1).astype(jnp.int32)
    n_used = tile_end[-1:].astype(jnp.int32)
    first_tile = (tile_end - tiles)[tile_expert]
    tile_rows = jnp.clip(counts[tile_expert] - (jnp.arange(n_tiles) - first_tile) * tm, 0, tm).astype(jnp.int32)
    e0, e1 = meta[:, 0].astype(jnp.int32), meta[:, 1].astype(jnp.int32)
    pos0 = row_start[e0] + meta[:, 4].astype(jnp.int32)
    pos1 = row_start[e1] + meta[:, 5].astype(jnp.int32)
    tt = min(SCATTER_TT, t)
    tail_blocks = jnp.stack([tile_end[-1] * tm // tt, n_tiles * tm // tt]).astype(jnp.int32)
    xs = moe_scatter_rows(xp, pos0, pos1, (row_start + counts).astype(jnp.int32),
                          (tile_end * tm).astype(jnp.int32), tail_blocks, n_tiles * tm)
    y = moe_ffn(tile_expert, tile_rows, n_used, xs, w_gate, w_up, w_down)
    return moe_combine(x, meta, pos0, pos1, y)


def kernel(x, l0_mix_norm, l0_mlstm_w_in, l0_mlstm_b_gates, l0_mlstm_head_norm, l0_mlstm_w_out, l0_ffn_norm, l0_ffn_w_gate, l0_ffn_w_up, l0_ffn_w_down, l1_mix_norm, l1_na_w_qkv, l1_na_q_norm, l1_na_k_norm, l1_na_rpb, l1_na_w_out, l1_ffn_norm, l1_moe_w_router, l1_moe_w_gate, l1_moe_w_up, l1_moe_w_down, l2_mix_norm, l2_mlstm_w_in, l2_mlstm_b_gates, l2_mlstm_head_norm, l2_mlstm_w_out, l2_ffn_norm, l2_ffn_w_gate, l2_ffn_w_up, l2_ffn_w_down, l3_mix_norm, l3_na_w_qkv, l3_na_q_norm, l3_na_k_norm, l3_na_rpb, l3_na_w_out, l3_ffn_norm, l3_moe_w_router, l3_moe_w_gate, l3_moe_w_up, l3_moe_w_down):
    batch, seq, d = x.shape
    h = x.reshape(batch * seq, d)
    h = mlstm_layer(h, batch, seq, l0_mix_norm, l0_mlstm_w_in, l0_mlstm_b_gates, l0_mlstm_head_norm, l0_mlstm_w_out)
    h = dense_ffn(h, l0_ffn_norm, l0_ffn_w_gate, l0_ffn_w_up, l0_ffn_w_down)
    h = na_layer(h, batch, seq, l1_mix_norm, l1_na_w_qkv, l1_na_q_norm, l1_na_k_norm, l1_na_rpb, l1_na_w_out)
    h = moe_layer(h, l1_ffn_norm, l1_moe_w_router, l1_moe_w_gate, l1_moe_w_up, l1_moe_w_down)
    h = mlstm_layer(h, batch, seq, l2_mix_norm, l2_mlstm_w_in, l2_mlstm_b_gates, l2_mlstm_head_norm, l2_mlstm_w_out)
    h = dense_ffn(h, l2_ffn_norm, l2_ffn_w_gate, l2_ffn_w_up, l2_ffn_w_down)
    h = na_layer(h, batch, seq, l3_mix_norm, l3_na_w_qkv, l3_na_q_norm, l3_na_k_norm, l3_na_rpb, l3_na_w_out)
    h = moe_layer(h, l3_ffn_norm, l3_moe_w_router, l3_moe_w_gate, l3_moe_w_up, l3_moe_w_down)
    return h.reshape(batch, seq, d)
```

```python
import functools

import jax
import jax.numpy as jnp
import numpy as np
from jax import lax
from jax.experimental import pallas as pl
from jax.experimental.pallas import tpu as pltpu

F32 = jnp.float32
BF16 = jnp.bfloat16
U32 = jnp.uint32

NORM_EPS = 1e-6
LANES = 128
VMEM_LIMIT_BYTES = 56 << 20
NEG_BIG = -1e30

MLSTM_HEADS = 4
MLSTM_DK = 256
GATE_SOFTCAP = 15.0
MLSTM_CHUNK = 256
GRID_W = 64
NA_HEADS = 16
NA_HEAD_DIM = 128
NA_KH = 8
NA_KW = 16
NA_ROWS_PER_BLOCK = 4
NA_HEADS_PER_GROUP = 4
N_EXPERTS = 8
TOP_K = 2

MM_TM, MM_TN = 1024, 1024
MLSTM_OUT_TM = 512
FFN_TM, FFN_TF = 1024, 512
MOE_TM, MOE_TF = 1024, 512
MOE_SUBTILES = 2
ROUTER_TT = 512
SCATTER_TT = 1024
COMBINE_TT = 512
NORM_ROWS = 128


def _params(*semantics):
    return pltpu.CompilerParams(dimension_semantics=semantics, vmem_limit_bytes=VMEM_LIMIT_BYTES)


def _dot(a, b):
    return jnp.dot(a, b, preferred_element_type=F32)


def _dot_nt(a, b):
    return lax.dot_general(a, b, (((1,), (1,)), ((), ())), preferred_element_type=F32)


def _split3(x):
    hi = x.astype(BF16)
    r1 = x - hi.astype(F32)
    mid = r1.astype(BF16)
    lo = (r1 - mid.astype(F32)).astype(BF16)
    return hi, mid, lo


def _rms_rows(x, gain):
    ms = jnp.mean(x * x, axis=-1, keepdims=True)
    return x * lax.rsqrt(ms + NORM_EPS) * gain


def _fill_rmsnorm(x_ref, g_ref, a_ref):
    def body(c, carry):
        r = pl.multiple_of(c * NORM_ROWS, NORM_ROWS)
        a_ref[pl.ds(r, NORM_ROWS), :] = _rms_rows(x_ref[pl.ds(r, NORM_ROWS), :], g_ref[...]).astype(BF16)
        return carry
    lax.fori_loop(0, x_ref.shape[0] // NORM_ROWS, body, 0)


def _qkv_matmul_kernel(x_ref, g_ref, w_ref, hg_ref, o_ref, a_ref, *, n_norm_tiles):
    j = pl.program_id(1)

    @pl.when(j == 0)
    def _():
        _fill_rmsnorm(x_ref, g_ref, a_ref)
    acc = _dot(a_ref[...], w_ref[...])

    @pl.when(j < n_norm_tiles)
    def _():
        hd = hg_ref.shape[2]
        mean_mat = jnp.full((hd, hd), 1.0 / hd, BF16)
        for g in range(o_ref.shape[1] // hd):
            cols = slice(g * hd, (g + 1) * hd)
            blk = acc[:, cols]
            ms = _dot((blk * blk).astype(BF16), mean_mat)
            o_ref[:, cols] = (blk * lax.rsqrt(ms + NORM_EPS) * hg_ref[0]).astype(o_ref.dtype)

    @pl.when(j >= n_norm_tiles)
    def _():
        o_ref[...] = acc.astype(o_ref.dtype)


def qkv_matmul(x, gain, w, q_norm, k_norm):
    m, k = x.shape
    n = w.shape[1]
    d = n // 3
    hd = NA_HEAD_DIM
    tm, tn = min(MM_TM, m), min(MM_TN, d)
    per = d // tn
    head_gain = jnp.concatenate([jnp.tile((q_norm * hd ** -0.5)[None], (per, 1)), jnp.tile(k_norm[None], (per, 1)),
                                 jnp.ones((per, hd), F32)]).reshape(3 * per, 1, hd)
    return pl.pallas_call(
        functools.partial(_qkv_matmul_kernel, n_norm_tiles=2 * per),
        out_shape=jax.ShapeDtypeStruct((m, n), BF16),
        grid=(m // tm, n // tn),
        in_specs=[pl.BlockSpec((tm, k), lambda i, j: (i, 0)),
                  pl.BlockSpec((1, k), lambda i, j: (0, 0)),
                  pl.BlockSpec((k, tn), lambda i, j: (0, j)),
                  pl.BlockSpec((1, 1, hd), lambda i, j: (j, 0, 0))],
        out_specs=pl.BlockSpec((tm, tn), lambda i, j: (i, j)),
        scratch_shapes=[pltpu.VMEM((tm, k), BF16)],
        compiler_params=_params("arbitrary", "arbitrary"),
        name="qkv_matmul",
    )(x, gain.reshape(1, k), w, head_gain)


def _mlstm_out_kernel(hf_ref, hb_ref, og_ref, g_ref, w_ref, r_ref, o_ref, a_ref, *, heads):
    @pl.when(pl.program_id(1) == 0)
    def _():
        dv = a_ref.shape[1] // heads

        def body(c, carry):
            r = pl.multiple_of(c * NORM_ROWS, NORM_ROWS)
            rows = pl.ds(r, NORM_ROWS)
            for h in range(heads):
                cols = slice(h * dv, (h + 1) * dv)
                hh = hf_ref[rows, cols].astype(F32) + hb_ref[rows, cols].astype(F32)
                y = _rms_rows(hh, g_ref[:, cols]) * jax.nn.sigmoid(og_ref[rows, cols].astype(F32))
                a_ref[rows, cols] = y.astype(BF16)
            return carry
        lax.fori_loop(0, a_ref.shape[0] // NORM_ROWS, body, 0)
    o_ref[...] = r_ref[...] + _dot(a_ref[...], w_ref[...])


def mlstm_out_proj(hf, hb, proj, head_norm, w, res):
    m, d = hf.shape
    n = w.shape[1]
    tm, tn = min(MLSTM_OUT_TM, m), n
    o_col_block = (proj.shape[1] - d) // d
    return pl.pallas_call(
        functools.partial(_mlstm_out_kernel, heads=MLSTM_HEADS),
        out_shape=jax.ShapeDtypeStruct((m, n), F32),
        grid=(m // tm, n // tn),
        in_specs=[pl.BlockSpec((tm, d), lambda i, j: (i, 0)),
                  pl.BlockSpec((tm, d), lambda i, j: (i, 0)),
                  pl.BlockSpec((tm, d), lambda i, j: (i, o_col_block)),
                  pl.BlockSpec((1, d), lambda i, j: (0, 0)),
                  pl.BlockSpec((d, tn), lambda i, j: (0, j), pipeline_mode=pl.Buffered(1)),
                  pl.BlockSpec((tm, tn), lambda i, j: (i, j))],
        out_specs=pl.BlockSpec((tm, tn), lambda i, j: (i, j)),
        scratch_shapes=[pltpu.VMEM((tm, d), BF16)],
        compiler_params=_params("arbitrary", "arbitrary"),
        name="mlstm_out_proj",
    )(hf, hb, proj, head_norm.reshape(1, d), w, res)


def _swiglu(a, wg, wu, wd):
    gate = _dot(a, wg)
    up = _dot(a, wu)
    hidden = (gate * jax.nn.sigmoid(gate) * up).astype(BF16)
    return _dot(hidden, wd)


def _ffn_kernel(x_ref, g_ref, wg_ref, wu_ref, wd_ref, o_ref, a_ref):
    @pl.when(pl.program_id(1) == 0)
    def _():
        _fill_rmsnorm(x_ref, g_ref, a_ref)
        o_ref[...] = x_ref[...]
    o_ref[...] += _swiglu(a_ref[...], wg_ref[...].astype(BF16), wu_ref[...].astype(BF16), wd_ref[...].astype(BF16))


def dense_ffn(x, gain, wg, wu, wd):
    m, d = x.shape
    f = wg.shape[1]
    tm, tf = min(FFN_TM, m), min(FFN_TF, f)
    return pl.pallas_call(
        _ffn_kernel,
        out_shape=jax.ShapeDtypeStruct((m, d), F32),
        grid=(m // tm, f // tf),
        in_specs=[pl.BlockSpec((tm, d), lambda i, j: (i, 0), pipeline_mode=pl.Buffered(1)),
                  pl.BlockSpec((1, d), lambda i, j: (0, 0)),
                  pl.BlockSpec((d, tf), lambda i, j: (0, j)),
                  pl.BlockSpec((d, tf), lambda i, j: (0, j)),
                  pl.BlockSpec((tf, d), lambda i, j: (j, 0))],
        out_specs=pl.BlockSpec((tm, d), lambda i, j: (i, 0), pipeline_mode=pl.Buffered(1)),
        scratch_shapes=[pltpu.VMEM((tm, d), BF16)],
        compiler_params=_params("arbitrary", "arbitrary"),
        name="dense_ffn",
    )(x, gain.reshape(1, d), wg, wu, wd)


def _unpack_bf16_pairs(words):
    hi = lax.bitcast_convert_type(words & jnp.uint32(0xFFFF0000), F32).astype(BF16)
    lo = lax.bitcast_convert_type(words << 16, F32).astype(BF16)
    return hi, lo


def _pack_bf16_pairs(hi, lo):
    return lax.bitcast_convert_type(hi, U32) | (lax.bitcast_convert_type(lo, U32) >> 16)


def _moe_ffn_kernel(te_ref, tr_ref, nu_ref, xs_ref, wg_ref, wu_ref, wd_ref, o_ref, a_ref):
    i, j = pl.program_id(0), pl.program_id(1)

    @pl.when(i < nu_ref[0])
    def _():
        @pl.when(j == 0)
        def _():
            half = a_ref.shape[1] // 2

            def body(c, carry):
                r = pl.multiple_of(c * NORM_ROWS, NORM_ROWS)
                hi, lo = _unpack_bf16_pairs(xs_ref[pl.ds(r, NORM_ROWS), :])
                a_ref[pl.ds(r, NORM_ROWS), :half] = hi
                a_ref[pl.ds(r, NORM_ROWS), half:] = lo
                return carry
            lax.fori_loop(0, a_ref.shape[0] // NORM_ROWS, body, 0)
            o_ref[...] = jnp.zeros_like(o_ref)

        wg, wu, wd = wg_ref[...].astype(BF16), wu_ref[...].astype(BF16), wd_ref[...].astype(BF16)
        sub = a_ref.shape[0] // MOE_SUBTILES
        rows_used = tr_ref[jnp.minimum(i, nu_ref[0] - 1)]
        for sb in range(MOE_SUBTILES):
            def sub_step(rows=slice(sb * sub, (sb + 1) * sub)):
                o_ref[rows, :] += _swiglu(a_ref[rows, :], wg, wu, wd)
            if sb == 0:
                sub_step()
            else:
                pl.when(rows_used > sb * sub)(sub_step)

    @pl.when((i >= nu_ref[0]) & (j == 0))
    def _():
        o_ref[...] = jnp.zeros_like(o_ref)


def moe_ffn(tile_expert, tile_rows, n_used, xs, wg, wu, wd):
    p, half = xs.shape
    d = 2 * half
    f = wg.shape[2]
    tm, tf = MOE_TM, min(MOE_TF, f)
    nf = f // tf

    def row_map(i, j, te, tr, nu):
        return (jnp.minimum(i, nu[0] - 1), 0)

    def f_idx(i, j, nu):
        return jnp.where(i < nu[0], j, nf - 1)

    def te_idx(i, te, nu):
        return te[jnp.minimum(i, nu[0] - 1)]

    return pl.pallas_call(
        _moe_ffn_kernel,
        out_shape=jax.ShapeDtypeStruct((p, d), F32),
        grid_spec=pltpu.PrefetchScalarGridSpec(
            num_scalar_prefetch=3,
            grid=(p // tm, nf),
            in_specs=[pl.BlockSpec((tm, half), row_map, pipeline_mode=pl.Buffered(1)),
                      pl.BlockSpec((None, d, tf), lambda i, j, te, tr, nu: (te_idx(i, te, nu), 0, f_idx(i, j, nu))),
                      pl.BlockSpec((None, d, tf), lambda i, j, te, tr, nu: (te_idx(i, te, nu), 0, f_idx(i, j, nu))),
                      pl.BlockSpec((None, tf, d), lambda i, j, te, tr, nu: (te_idx(i, te, nu), f_idx(i, j, nu), 0))],
            out_specs=pl.BlockSpec((tm, d), lambda i, j, te, tr, nu: (i, 0), pipeline_mode=pl.Buffered(1)),
            scratch_shapes=[pltpu.VMEM((tm, d), BF16)]),
        compiler_params=_params("arbitrary", "arbitrary"),
        name="moe_ffn",
    )(tile_expert, tile_rows, n_used, xs, wg, wu, wd)


def _gate_rows(xn, wt_ref, b_ref):
    xh = xn.astype(BF16)
    xl = (xn - xh.astype(F32)).astype(BF16)
    w = wt_ref[...]
    wh = w.astype(BF16)
    wl = (w - wh.astype(F32)).astype(BF16)
    pre = _dot_nt(wh, xh) + (_dot_nt(wh, xl) + _dot_nt(wl, xh)) + b_ref[...]
    g = GATE_SOFTCAP * jnp.tanh(pre / GATE_SOFTCAP)
    log_sig = jnp.minimum(g, 0.0) - jnp.log1p(jnp.exp(-jnp.abs(g)))
    row = lax.broadcasted_iota(jnp.int32, g.shape, 0)
    is_forget = ((row >= MLSTM_HEADS) & (row < 2 * MLSTM_HEADS)) | (row >= 3 * MLSTM_HEADS)
    return jnp.where(is_forget, log_sig, g)


def _mlstm_in_kernel(x_ref, g_ref, w_ref, wt_ref, b_ref, o_ref, gates_ref, a_ref):
    @pl.when(pl.program_id(1) == 0)
    def _():
        for c in range(x_ref.shape[0] // NORM_ROWS):
            rows = slice(c * NORM_ROWS, (c + 1) * NORM_ROWS)
            xn = _rms_rows(x_ref[rows, :], g_ref[...])
            a_ref[rows, :] = xn.astype(BF16)
            gates_ref[:, rows] = _gate_rows(xn, wt_ref, b_ref)
    o_ref[...] = _dot(a_ref[...], w_ref[...]).astype(o_ref.dtype)


def mlstm_in_proj(x, gain, w, n_cols, w_gates_t, b_gates):
    m, k = x.shape
    ng = w_gates_t.shape[0]
    tm, tn = min(MM_TM, m), min(MM_TN, n_cols)
    return pl.pallas_call(
        _mlstm_in_kernel,
        out_shape=(jax.ShapeDtypeStruct((m, n_cols), BF16), jax.ShapeDtypeStruct((ng, m), F32)),
        grid=(m // tm, n_cols // tn),
        in_specs=[pl.BlockSpec((tm, k), lambda i, j: (i, 0)),
                  pl.BlockSpec((1, k), lambda i, j: (0, 0)),
                  pl.BlockSpec((k, tn), lambda i, j: (0, j)),
                  pl.BlockSpec((ng, k), lambda i, j: (0, 0)),
                  pl.BlockSpec((ng, 1), lambda i, j: (0, 0))],
        out_specs=(pl.BlockSpec((tm, tn), lambda i, j: (i, j)),
                   pl.BlockSpec((ng, tm), lambda i, j: (0, i))),
        scratch_shapes=[pltpu.VMEM((tm, k), BF16)],
        compiler_params=_params("arbitrary", "arbitrary"),
        name="mlstm_in_proj",
    )(x, gain.reshape(1, k), w, w_gates_t, b_gates.reshape(ng, 1))


def _mlstm_chunk(q_ref, k_ref, v_ref, li_ref, lf_ref, h_ref, c_ref, n_ref, m_ref, slot, head, reverse):
    L = q_ref.shape[0]
    dk = c_ref.shape[1]
    dv = c_ref.shape[2]
    q = q_ref[:, head * dk:(head + 1) * dk] * jnp.asarray(dk ** -0.5, BF16)
    k = k_ref[:, head * dk:(head + 1) * dk]
    v = v_ref[:, head * dv:(head + 1) * dv]
    li = li_ref[head]
    lf = lf_ref[head]
    r_i = lax.broadcasted_iota(jnp.int32, (L, L), 0)
    c_i = lax.broadcasted_iota(jnp.int32, (L, L), 1)
    visible = (c_i >= r_i) if reverse else (c_i <= r_i)
    cum = (r_i >= c_i) if reverse else (r_i <= c_i)

    hi, mid, lo = (p.astype(F32) for p in _split3(lf))
    prow = lax.broadcasted_iota(jnp.int32, (16, L), 0)
    pieces = jnp.where(prow == 0, hi, jnp.where(prow == 1, mid, jnp.where(prow == 2, lo, 0.0)))
    b = jnp.sum(_dot(pieces.astype(BF16), jnp.where(cum, 1.0, 0.0).astype(BF16)), axis=0, keepdims=True)
    g = jnp.sum(lf, axis=-1, keepdims=True)
    u = li - b

    m_prev = m_ref[slot][:, :1]
    a_mat = jnp.where(visible, jnp.broadcast_to(u, (L, L)), -jnp.inf)
    m_row = jnp.maximum(jnp.max(a_mat, axis=-1, keepdims=True), m_prev)
    w_intra = (jnp.exp(a_mat - m_row) * _dot_nt(q, k)).astype(BF16)
    w_inter = jnp.exp(m_prev - m_row)

    ones = jnp.ones((L, LANES), BF16)
    num = w_inter * _dot(q, c_ref[slot].astype(BF16)) + _dot(w_intra, v)
    den = w_inter * _dot(q, n_ref[slot].astype(BF16)) + _dot(w_intra, ones)
    b_col = jnp.sum(jnp.where(r_i == c_i, jnp.broadcast_to(b, (L, L)), 0.0), axis=-1, keepdims=True)
    inv = 1.0 / jnp.maximum(jnp.abs(den), jnp.exp(-(b_col + m_row)))
    h_ref[:, head * dv:(head + 1) * dv] = (num * jnp.concatenate([inv] * (dv // LANES), axis=1)).astype(h_ref.dtype)

    m_new = g + jnp.maximum(m_prev, jnp.max(u, axis=-1, keepdims=True))
    decay = jnp.exp(g + m_prev - m_new)
    ks_t = (k.astype(F32).T * jnp.exp(g + u - m_new)).astype(BF16)
    c_ref[slot] = decay * c_ref[slot] + _dot(ks_t, v)
    n_ref[slot] = decay * n_ref[slot] + _dot(ks_t, ones)
    m_ref[slot] = jnp.broadcast_to(m_new, (1, LANES))


def _mlstm_kernel(qf, kf, vf, lif, lff, qb, kb, vb, lib, lfb, hf_ref, hb_ref, c_ref, n_ref, m_ref):
    @pl.when(pl.program_id(1) == 0)
    def _():
        c_ref[...] = jnp.zeros_like(c_ref)
        n_ref[...] = jnp.zeros_like(n_ref)
        m_ref[...] = jnp.full_like(m_ref, -jnp.inf)
    heads = lif.shape[0]
    for h in range(heads):
        _mlstm_chunk(qf, kf, vf, lif, lff, hf_ref, c_ref, n_ref, m_ref, h, h, False)
        _mlstm_chunk(qb, kb, vb, lib, lfb, hb_ref, c_ref, n_ref, m_ref, heads + h, h, True)


def mlstm_scan(proj, gates, batch, seq, d_model):
    heads, dk = MLSTM_HEADS, MLSTM_DK
    dv = d_model // heads
    L = min(MLSTM_CHUNK, seq)
    nc = seq // L
    t = batch * seq
    qw = heads * dk

    def fwd(b, j):
        return b * nc + j

    def bwd(b, j):
        return b * nc + (nc - 1 - j)

    def qkv_specs(pos):
        return [pl.BlockSpec((L, qw), lambda b, j: (pos(b, j), 0)),
                pl.BlockSpec((L, qw), lambda b, j: (pos(b, j), 1)),
                pl.BlockSpec((L, d_model), lambda b, j: (pos(b, j), 2 * qw // d_model))]

    def gate_spec(kind, pos):
        return pl.BlockSpec((heads, 1, L), lambda b, j: (kind, 0, pos(b, j)))

    return pl.pallas_call(
        _mlstm_kernel,
        out_shape=(jax.ShapeDtypeStruct((t, d_model), BF16), jax.ShapeDtypeStruct((t, d_model), BF16)),
        grid=(batch, nc),
        in_specs=(qkv_specs(fwd) + [gate_spec(0, fwd), gate_spec(1, fwd)]
                  + qkv_specs(bwd) + [gate_spec(2, bwd), gate_spec(3, bwd)]),
        out_specs=(pl.BlockSpec((L, d_model), lambda b, j: (fwd(b, j), 0)),
                   pl.BlockSpec((L, d_model), lambda b, j: (bwd(b, j), 0))),
        scratch_shapes=[pltpu.VMEM((2 * heads, dk, dv), F32), pltpu.VMEM((2 * heads, dk, LANES), F32),
                        pltpu.VMEM((2 * heads, 1, LANES), F32)],
        compiler_params=_params("arbitrary", "arbitrary"),
        name="mlstm_scan",
    )(proj, proj, proj, gates, gates, proj, proj, proj, gates, gates)


def _na_fused_kernel(q_ref, k0_ref, k1_ref, k2_ref, v0_ref, v1_ref, v2_ref, bias_ref, wo_ref, x_ref, o_ref):
    hd = NA_HEAD_DIM
    gw = NA_HEADS_PER_GROUP * hd
    for g in range(q_ref.shape[1] // gw):
        outs = []
        for h in range(g * NA_HEADS_PER_GROUP, (g + 1) * NA_HEADS_PER_GROUP):
            cols = slice(h * hd, (h + 1) * hd)
            k = jnp.concatenate([k0_ref[:, cols], k1_ref[:, cols], k2_ref[:, cols]], axis=0)
            v = jnp.concatenate([v0_ref[:, cols], v1_ref[:, cols], v2_ref[:, cols]], axis=0)
            s = _dot_nt(q_ref[:, cols], k) + bias_ref[0, h].astype(F32)
            p = jnp.exp(s - jnp.max(s, axis=-1, keepdims=True))
            o = _dot(p.astype(BF16), v) * (1.0 / jnp.sum(p, axis=-1, keepdims=True))
            outs.append(o.astype(BF16))
        part = _dot(jnp.concatenate(outs, axis=1), wo_ref[g * gw:(g + 1) * gw, :])
        if g == 0:
            o_ref[...] = x_ref[...] + part
        else:
            o_ref[...] += part


def _na_bias_tables(rpb, rows):
    rb, w = NA_ROWS_PER_BLOCK, GRID_W
    nb = rows // rb
    heads, n_dr, n_dc = rpb.shape
    period = 2 * w
    left = (w - 1) - (NA_KW - 1)
    vec = jnp.pad(rpb.astype(BF16), ((0, 0), (0, 0), (left, period - left - n_dc)))
    toep = jnp.tile(vec, (1, 1, w))[:, :, :w * (period - 1)].reshape(heads, n_dr, w, period - 1)[..., w - 1:]
    zero = jnp.zeros((heads, w, w), BF16)
    tables = []
    for blk in (0, 1, nb - 1):
        ws = min(max(blk - 1, 0), nb - 3) * rb
        r = blk * rb + np.arange(rb)[:, None, None, None]
        c = np.arange(w)[None, :, None, None]
        a = ws + np.arange(3 * rb)[None, None, :, None]
        kc = np.arange(w)[None, None, None, :]
        rs = np.clip(r - NA_KH // 2, 0, rows - NA_KH)
        cs = np.clip(c - NA_KW // 2, 0, w - NA_KW)
        ok = (a >= rs) & (a < rs + NA_KH) & (kc >= cs) & (kc < cs + NA_KW)
        ok = np.broadcast_to(ok, (rb, w, 3 * rb, w)).reshape(rb * w, 3 * rb * w)
        parts = []
        for qr in range(rb):
            drs = [ws + ar - (blk * rb + qr) + NA_KH - 1 for ar in range(3 * rb)]
            parts.append(jnp.concatenate([toep[:, dr] if 0 <= dr < n_dr else zero for dr in drs], axis=-1))
        vals = jnp.stack(parts, axis=1).reshape(heads, rb * w, 3 * rb * w)
        tables.append(jnp.where(ok[None], vals, NEG_BIG))
    return jnp.stack(tables)


def na_attention_out(qkv, rpb, w_out, x, batch, seq):
    rb, w = NA_ROWS_PER_BLOCK, GRID_W
    t, d = x.shape
    rows = seq // w
    nb = rows // rb
    tq = rb * w
    bias = _na_bias_tables(rpb, rows)
    once = pl.Buffered(1)

    def kv_spec(part, off):
        return pl.BlockSpec((tq, d), lambda b, i: (b * nb + jnp.clip(i - 1, 0, nb - 3) + off, part))

    def bias_type(i):
        return jnp.where(i == 0, 0, jnp.where(i == nb - 1, 2, 1))

    return pl.pallas_call(
        _na_fused_kernel,
        out_shape=jax.ShapeDtypeStruct((t, d), F32),
        grid=(batch, nb),
        in_specs=([pl.BlockSpec((tq, d), lambda b, i: (b * nb + i, 0))]
                  + [kv_spec(1, off) for off in range(3)] + [kv_spec(2, off) for off in range(3)]
                  + [pl.BlockSpec((1, NA_HEADS, tq, 3 * tq), lambda b, i: (bias_type(i), 0, 0, 0), pipeline_mode=once),
                     pl.BlockSpec((d, d), lambda b, i: (0, 0), pipeline_mode=once),
                     pl.BlockSpec((tq, d), lambda b, i: (b * nb + i, 0))]),
        out_specs=pl.BlockSpec((tq, d), lambda b, i: (b * nb + i, 0)),
        compiler_params=_params("arbitrary", "arbitrary"),
        name="na_attention_out",
    )(qkv, qkv, qkv, qkv, qkv, qkv, qkv, bias, w_out, x)


def _router_kernel(x_ref, g_ref, wr_ref, xp_ref, meta_ref, cnt_ref, carry_ref):
    @pl.when(pl.program_id(0) == 0)
    def _():
        carry_ref[...] = jnp.zeros_like(carry_ref)

    tt = x_ref.shape[0]
    xn = _rms_rows(x_ref[...], g_ref[...])
    xh = xn.astype(BF16)
    xhf = xh.astype(F32)
    xl = (xn - xhf).astype(BF16)
    wr = wr_ref[...]
    wh = wr.astype(BF16)
    wl = (wr - wh.astype(F32)).astype(BF16)
    logits = _dot(xh, wh) + (_dot(xh, wl) + _dot(xl, wh))
    col = lax.broadcasted_iota(jnp.int32, logits.shape, 1).astype(F32)
    lg = jnp.where(col < N_EXPERTS, logits, -jnp.inf)
    m1 = jnp.max(lg, axis=-1, keepdims=True)
    i1 = jnp.min(jnp.where(lg == m1, col, float(LANES)), axis=-1, keepdims=True)
    lg2 = jnp.where(col == i1, -jnp.inf, lg)
    m2 = jnp.max(lg2, axis=-1, keepdims=True)
    i2 = jnp.min(jnp.where(lg2 == m2, col, float(LANES)), axis=-1, keepdims=True)
    e2 = jnp.exp(m2 - m1)
    w1 = 1.0 / (1.0 + e2)
    w2 = e2 * w1

    hit1, hit2 = col == i1, col == i2
    cnt = jnp.where(hit1 | hit2, 1.0, 0.0)
    r_i = lax.broadcasted_iota(jnp.int32, (tt, tt), 0)
    c_i = lax.broadcasted_iota(jnp.int32, (tt, tt), 1)
    before = jnp.where(c_i < r_i, 1.0, 0.0).astype(BF16)
    prefix = _dot(before, cnt.astype(BF16)) + carry_ref[...]
    rank1 = jnp.sum(jnp.where(hit1, prefix, 0.0), axis=-1, keepdims=True)
    rank2 = jnp.sum(jnp.where(hit2, prefix, 0.0), axis=-1, keepdims=True)
    carry_ref[...] += jnp.sum(cnt, axis=0, keepdims=True)
    cnt_ref[...] = carry_ref[...]

    meta = jnp.zeros(logits.shape, F32)
    for lane, val in enumerate((i1, i2, w1, w2, rank1, rank2)):
        meta = jnp.where(col == lane, val, meta)
    meta_ref[...] = meta
    half = xhf.shape[1] // 2
    xp_ref[...] = _pack_bf16_pairs(xhf[:, :half], xhf[:, half:])


def moe_router(x, gain, w_router):
    t, d = x.shape
    tt = min(ROUTER_TT, t)
    wr = jnp.zeros((d, LANES), F32).at[:, :N_EXPERTS].set(w_router)
    return pl.pallas_call(
        _router_kernel,
        out_shape=(jax.ShapeDtypeStruct((t, d // 2), U32), jax.ShapeDtypeStruct((t, LANES), F32),
                   jax.ShapeDtypeStruct((1, LANES), F32)),
        grid=(t // tt,),
        in_specs=[pl.BlockSpec((tt, d), lambda i: (i, 0)),
                  pl.BlockSpec((1, d), lambda i: (0, 0)),
                  pl.BlockSpec((d, LANES), lambda i: (0, 0))],
        out_specs=(pl.BlockSpec((tt, d // 2), lambda i: (i, 0)),
                   pl.BlockSpec((tt, LANES), lambda i: (i, 0)),
                   pl.BlockSpec((1, LANES), lambda i: (0, 0))),
        scratch_shapes=[pltpu.VMEM((1, LANES), F32)],
        compiler_params=_params("arbitrary"),
        name="moe_router",
    )(x, gain.reshape(1, d), wr)


def _scatter_kernel(pad_lo_ref, pad_hi_ref, tail_ref, p0_ref, p1_ref, src_ref, dst_ref, sem):
    tt = p0_ref.shape[2]

    def row_copy(src_row, dst_row):
        return pltpu.make_async_copy(src_ref.at[pl.ds(src_row, 1)], dst_ref.at[pl.ds(dst_row, 1)], sem)

    def block_copy(dst_block):
        return pltpu.make_async_copy(src_ref, dst_ref.at[pl.ds(pl.multiple_of(dst_block * tt, tt), tt)], sem)

    @pl.when(pl.program_id(0) == 0)
    def _():
        for e in range(N_EXPERTS):
            def pad_body(r, carry):
                row_copy(0, r).start()
                return carry
            lax.fori_loop(pad_lo_ref[e], pad_hi_ref[e], pad_body, 0)

            def pad_wait(r, carry):
                row_copy(0, 0).wait()
                return carry
            lax.fori_loop(pad_lo_ref[e], pad_hi_ref[e], pad_wait, 0)

        def tail_body(blk, carry):
            block_copy(blk).start()
            block_copy(blk).wait()
            return carry
        lax.fori_loop(tail_ref[0], tail_ref[1], tail_body, 0)

    def start_body(t, carry):
        row_copy(t, p0_ref[0, 0, t]).start()
        row_copy(t, p1_ref[0, 0, t]).start()
        return carry
    lax.fori_loop(0, tt, start_body, 0, unroll=8)
    block_copy(0).wait()
    block_copy(0).wait()


def moe_scatter_rows(xp, pos0, pos1, pad_lo, pad_hi, tail_blocks, n_rows):
    t, half = xp.shape
    tt = min(SCATTER_TT, t)
    nblk = t // tt
    smem_rows = pl.BlockSpec((1, 1, tt), lambda i, lo, hi, tl: (i, 0, 0), memory_space=pltpu.SMEM)
    return pl.pallas_call(
        _scatter_kernel,
        out_shape=jax.ShapeDtypeStruct((n_rows, half), xp.dtype),
        grid_spec=pltpu.PrefetchScalarGridSpec(
            num_scalar_prefetch=3,
            grid=(nblk,),
            in_specs=[smem_rows, smem_rows, pl.BlockSpec((tt, half), lambda i, lo, hi, tl: (i, 0))],
            out_specs=pl.BlockSpec(memory_space=pl.ANY),
            scratch_shapes=[pltpu.SemaphoreType.DMA]),
        compiler_params=pltpu.CompilerParams(dimension_semantics=("arbitrary",), has_side_effects=True),
        name="moe_scatter_rows",
    )(pad_lo, pad_hi, tail_blocks, pos0.reshape(nblk, 1, tt), pos1.reshape(nblk, 1, tt), xp)


def _combine_kernel(p0_ref, p1_ref, x_ref, meta_ref, y_ref, o_ref, buf_ref, sem):
    tt = x_ref.shape[0]

    def row_copy(k, src_row, t):
        return pltpu.make_async_copy(y_ref.at[pl.ds(src_row, 1)], buf_ref.at[k, pl.ds(t, 1)], sem)

    def start_body(t, carry):
        row_copy(0, p0_ref[0, 0, t], t).start(priority=0)
        row_copy(1, p1_ref[0, 0, t], t).start(priority=1)
        return carry
    lax.fori_loop(0, tt, start_body, 0, unroll=8)
    for k in range(TOP_K):
        pltpu.make_async_copy(y_ref.at[pl.ds(0, tt)], buf_ref.at[k], sem).wait()

    meta = meta_ref[...]
    o_ref[...] = x_ref[...] + (meta[:, 2:3] * buf_ref[0] + meta[:, 3:4] * buf_ref[1])


def moe_combine(x, meta, pos0, pos1, y):
    t, d = x.shape
    tt = min(COMBINE_TT, t)
    nblk = t // tt
    smem_rows = pl.BlockSpec((1, 1, tt), lambda i: (i, 0, 0), memory_space=pltpu.SMEM)
    return pl.pallas_call(
        _combine_kernel,
        out_shape=jax.ShapeDtypeStruct((t, d), F32),
        grid=(nblk,),
        in_specs=[smem_rows, smem_rows,
                  pl.BlockSpec((tt, d), lambda i: (i, 0)),
                  pl.BlockSpec((tt, LANES), lambda i: (i, 0)),
                  pl.BlockSpec(memory_space=pl.ANY)],
        out_specs=pl.BlockSpec((tt, d), lambda i: (i, 0)),
        scratch_shapes=[pltpu.VMEM((2, tt, d), F32), pltpu.SemaphoreType.DMA],
        compiler_params=_params("arbitrary"),
        name="moe_combine",
    )(pos0.reshape(nblk, 1, tt), pos1.reshape(nblk, 1, tt), x, meta, y)


def mlstm_layer(x, batch, seq, mix_norm, w_in, b_gates, head_norm, w_out):
    t, d = x.shape
    n_main = w_in.shape[1] - 4 * MLSTM_HEADS
    proj, gates = mlstm_in_proj(x, mix_norm, w_in.astype(BF16), n_main, w_in[:, n_main:].T, b_gates)
    hf, hb = mlstm_scan(proj, gates.reshape(4 * MLSTM_HEADS, 1, t), batch, seq, d)
    return mlstm_out_proj(hf, hb, proj, head_norm, w_out.astype(BF16), x)


def na_layer(x, batch, seq, mix_norm, w_qkv, q_norm, k_norm, rpb, w_out):
    qkv = qkv_matmul(x, mix_norm, w_qkv.astype(BF16), q_norm, k_norm)
    return na_attention_out(qkv, rpb, w_out.astype(BF16), x, batch, seq)


def moe_layer(x, ffn_norm, w_router, w_gate, w_up, w_down):
    t, d = x.shape
    tm = MOE_TM
    xp, meta, counts = moe_router(x, ffn_norm, w_router)
    counts = counts[0, :N_EXPERTS].astype(jnp.int32)
    tiles = (counts + tm - 1) // tm
    tile_end = jnp.cumsum(tiles)
    row_start = (tile_end - tiles) * tm
    n_tiles = TOP_K * t // tm + N_EXPERTS
    tile_expert = jnp.minimum(jnp.sum(jnp.arange(n_tiles)[:, None] >= tile_end[None, :], axis=1),
                              N_EXPERTS - 1).astype(jnp.int32)
    n_used = tile_end[-1:].astype(jnp.int32)
    first_tile = (tile_end - tiles)[tile_expert]
    tile_rows = jnp.clip(counts[tile_expert] - (jnp.arange(n_tiles) - first_tile) * tm, 0, tm).astype(jnp.int32)
    e0, e1 = meta[:, 0].astype(jnp.int32), meta[:, 1].astype(jnp.int32)
    pos0 = row_start[e0] + meta[:, 4].astype(jnp.int32)
    pos1 = row_start[e1] + meta[:, 5].astype(jnp.int32)
    tt = min(SCATTER_TT, t)
    tail_blocks = jnp.stack([tile_end[-1] * tm // tt, n_tiles * tm // tt]).astype(jnp.int32)
    xs = moe_scatter_rows(xp, pos0, pos1, (row_start + counts).astype(jnp.int32),
                          (tile_end * tm).astype(jnp.int32), tail_blocks, n_tiles * tm)
    y = moe_ffn(tile_expert, tile_rows, n_used, xs, w_gate, w_up, w_down)
    return moe_combine(x, meta, pos0, pos1, y)


def kernel(x, l0_mix_norm, l0_mlstm_w_in, l0_mlstm_b_gates, l0_mlstm_head_norm, l0_mlstm_w_out, l0_ffn_norm, l0_ffn_w_gate, l0_ffn_w_up, l0_ffn_w_down, l1_mix_norm, l1_na_w_qkv, l1_na_q_norm, l1_na_k_norm, l1_na_rpb, l1_na_w_out, l1_ffn_norm, l1_moe_w_router, l1_moe_w_gate, l1_moe_w_up, l1_moe_w_down, l2_mix_norm, l2_mlstm_w_in, l2_mlstm_b_gates, l2_mlstm_head_norm, l2_mlstm_w_out, l2_ffn_norm, l2_ffn_w_gate, l2_ffn_w_up, l2_ffn_w_down, l3_mix_norm, l3_na_w_qkv, l3_na_q_norm, l3_na_k_norm, l3_na_rpb, l3_na_w_out, l3_ffn_norm, l3_moe_w_router, l3_moe_w_gate, l3_moe_w_up, l3_moe_w_down):
    batch, seq, d = x.shape
    h = x.reshape(batch * seq, d)
    h = mlstm_layer(h, batch, seq, l0_mix_norm, l0_mlstm_w_in, l0_mlstm_b_gates, l0_mlstm_head_norm, l0_mlstm_w_out)
    h = dense_ffn(h, l0_ffn_norm, l0_ffn_w_gate, l0_ffn_w_up, l0_ffn_w_down)
    h = na_layer(h, batch, seq, l1_mix_norm, l1_na_w_qkv, l1_na_q_norm, l1_na_k_norm, l1_na_rpb, l1_na_w_out)
    h = moe_layer(h, l1_ffn_norm, l1_moe_w_router, l1_moe_w_gate, l1_moe_w_up, l1_moe_w_down)
    h = mlstm_layer(h, batch, seq, l2_mix_norm, l2_mlstm_w_in, l2_mlstm_b_gates, l2_mlstm_head_norm, l2_mlstm_w_out)
    h = dense_ffn(h, l2_ffn_norm, l2_ffn_w_gate, l2_ffn_w_up, l2_ffn_w_down)
    h = na_layer(h, batch, seq, l3_mix_norm, l3_na_w_qkv, l3_na_q_norm, l3_na_k_norm, l3_na_rpb, l3_na_w_out)
    h = moe_layer(h, l3_ffn_norm, l3_moe_w_router, l3_moe_w_gate, l3_moe_w_up, l3_moe_w_down)
    return h.reshape(batch, seq, d)
```

```python
import functools

import jax
import jax.numpy as jnp
import numpy as np
from jax import lax
from jax.experimental import pallas as pl
from jax.experimental.pallas import tpu as pltpu

F32 = jnp.float32
BF16 = jnp.bfloat16
U32 = jnp.uint32

NORM_EPS = 1e-6
LANES = 128
BF16_SUBLANES = 16
VMEM_LIMIT_BYTES = 56 << 20
NEG_BIG = -1e30

MLSTM_HEADS = 4
MLSTM_DK = 256
GATE_SOFTCAP = 15.0
MLSTM_CHUNK = 256
GRID_W = 64
NA_HEADS = 16
NA_HEAD_DIM = 128
NA_KH = 8
NA_KW = 16
NA_ROWS_PER_BLOCK = 4
NA_HEADS_PER_GROUP = 4
N_EXPERTS = 8
TOP_K = 2

MM_TM, MM_TN = 1024, 1024
MLSTM_OUT_TM = 512
FFN_TM, FFN_TF = 1024, 512
MOE_TM, MOE_TF = 1024, 512
MOE_SUBTILES = 2
ROUTER_TT = 512
SCATTER_TT = 1024
COMBINE_TT = 512
NORM_ROWS = 128


def _params(*semantics):
    return pltpu.CompilerParams(dimension_semantics=semantics, vmem_limit_bytes=VMEM_LIMIT_BYTES)


def _dot(a, b):
    return jnp.dot(a, b, preferred_element_type=F32)


def _dot_nt(a, b):
    return lax.dot_general(a, b, (((1,), (1,)), ((), ())), preferred_element_type=F32)


def _split3(x):
    hi = x.astype(BF16)
    r1 = x - hi.astype(F32)
    mid = r1.astype(BF16)
    lo = (r1 - mid.astype(F32)).astype(BF16)
    return hi, mid, lo


def _rms_rows(x, gain):
    ms = jnp.mean(x * x, axis=-1, keepdims=True)
    return x * lax.rsqrt(ms + NORM_EPS) * gain


def _fill_rmsnorm(x_ref, g_ref, a_ref):
    def body(c, carry):
        r = pl.multiple_of(c * NORM_ROWS, NORM_ROWS)
        a_ref[pl.ds(r, NORM_ROWS), :] = _rms_rows(x_ref[pl.ds(r, NORM_ROWS), :], g_ref[...]).astype(BF16)
        return carry
    lax.fori_loop(0, x_ref.shape[0] // NORM_ROWS, body, 0)


def _qkv_matmul_kernel(x_ref, g_ref, w_ref, hg_ref, o_ref, a_ref, *, n_norm_tiles):
    j = pl.program_id(1)

    @pl.when(j == 0)
    def _():
        _fill_rmsnorm(x_ref, g_ref, a_ref)
    acc = _dot(a_ref[...], w_ref[...])

    @pl.when(j < n_norm_tiles)
    def _():
        hd = hg_ref.shape[2]
        mean_mat = jnp.full((hd, hd), 1.0 / hd, BF16)
        for g in range(o_ref.shape[1] // hd):
            cols = slice(g * hd, (g + 1) * hd)
            blk = acc[:, cols]
            ms = _dot((blk * blk).astype(BF16), mean_mat)
            o_ref[:, cols] = (blk * lax.rsqrt(ms + NORM_EPS) * hg_ref[0]).astype(o_ref.dtype)

    @pl.when(j >= n_norm_tiles)
    def _():
        o_ref[...] = acc.astype(o_ref.dtype)


def qkv_matmul(x, gain, w, q_norm, k_norm):
    m, k = x.shape
    n = w.shape[1]
    d = n // 3
    hd = NA_HEAD_DIM
    tm, tn = min(MM_TM, m), min(MM_TN, d)
    per = d // tn
    head_gain = jnp.concatenate([jnp.tile((q_norm * hd ** -0.5)[None], (per, 1)), jnp.tile(k_norm[None], (per, 1)),
                                 jnp.ones((per, hd), F32)]).reshape(3 * per, 1, hd)
    return pl.pallas_call(
        functools.partial(_qkv_matmul_kernel, n_norm_tiles=2 * per),
        out_shape=jax.ShapeDtypeStruct((m, n), BF16),
        grid=(m // tm, n // tn),
        in_specs=[pl.BlockSpec((tm, k), lambda i, j: (i, 0)),
                  pl.BlockSpec((1, k), lambda i, j: (0, 0)),
                  pl.BlockSpec((k, tn), lambda i, j: (0, j)),
                  pl.BlockSpec((1, 1, hd), lambda i, j: (j, 0, 0))],
        out_specs=pl.BlockSpec((tm, tn), lambda i, j: (i, j)),
        scratch_shapes=[pltpu.VMEM((tm, k), BF16)],
        compiler_params=_params("arbitrary", "arbitrary"),
        name="qkv_matmul",
    )(x, gain.reshape(1, k), w, head_gain)


def _mlstm_out_kernel(hf_ref, hb_ref, og_ref, g_ref, w_ref, r_ref, o_ref, a_ref, *, heads):
    @pl.when(pl.program_id(1) == 0)
    def _():
        dv = a_ref.shape[1] // heads

        def body(c, carry):
            r = pl.multiple_of(c * NORM_ROWS, NORM_ROWS)
            rows = pl.ds(r, NORM_ROWS)
            for h in range(heads):
                cols = slice(h * dv, (h + 1) * dv)
                hh = hf_ref[rows, cols].astype(F32) + hb_ref[rows, cols].astype(F32)
                y = _rms_rows(hh, g_ref[:, cols]) * jax.nn.sigmoid(og_ref[rows, cols].astype(F32))
                a_ref[rows, cols] = y.astype(BF16)
            return carry
        lax.fori_loop(0, a_ref.shape[0] // NORM_ROWS, body, 0)
    o_ref[...] = r_ref[...] + _dot(a_ref[...], w_ref[...])


def mlstm_out_proj(hf, hb, proj, head_norm, w, res):
    m, d = hf.shape
    n = w.shape[1]
    tm, tn = min(MLSTM_OUT_TM, m), n
    o_col_block = (proj.shape[1] - d) // d
    return pl.pallas_call(
        functools.partial(_mlstm_out_kernel, heads=MLSTM_HEADS),
        out_shape=jax.ShapeDtypeStruct((m, n), F32),
        grid=(m // tm, n // tn),
        in_specs=[pl.BlockSpec((tm, d), lambda i, j: (i, 0)),
                  pl.BlockSpec((tm, d), lambda i, j: (i, 0)),
                  pl.BlockSpec((tm, d), lambda i, j: (i, o_col_block)),
                  pl.BlockSpec((1, d), lambda i, j: (0, 0)),
                  pl.BlockSpec((d, tn), lambda i, j: (0, j), pipeline_mode=pl.Buffered(1)),
                  pl.BlockSpec((tm, tn), lambda i, j: (i, j))],
        out_specs=pl.BlockSpec((tm, tn), lambda i, j: (i, j)),
        scratch_shapes=[pltpu.VMEM((tm, d), BF16)],
        compiler_params=_params("arbitrary", "arbitrary"),
        name="mlstm_out_proj",
    )(hf, hb, proj, head_norm.reshape(1, d), w, res)


def _swiglu(a, wg, wu, wd):
    gate = _dot(a, wg)
    up = _dot(a, wu)
    hidden = (gate * jax.nn.sigmoid(gate) * up).astype(BF16)
    return _dot(hidden, wd)


def _ffn_kernel(x_ref, g_ref, wg_ref, wu_ref, wd_ref, o_ref, a_ref):
    @pl.when(pl.program_id(1) == 0)
    def _():
        _fill_rmsnorm(x_ref, g_ref, a_ref)
        o_ref[...] = x_ref[...]
    o_ref[...] += _swiglu(a_ref[...], wg_ref[...].astype(BF16), wu_ref[...].astype(BF16), wd_ref[...].astype(BF16))


def dense_ffn(x, gain, wg, wu, wd):
    m, d = x.shape
    f = wg.shape[1]
    tm, tf = min(FFN_TM, m), min(FFN_TF, f)
    return pl.pallas_call(
        _ffn_kernel,
        out_shape=jax.ShapeDtypeStruct((m, d), F32),
        grid=(m // tm, f // tf),
        in_specs=[pl.BlockSpec((tm, d), lambda i, j: (i, 0), pipeline_mode=pl.Buffered(1)),
                  pl.BlockSpec((1, d), lambda i, j: (0, 0)),
                  pl.BlockSpec((d, tf), lambda i, j: (0, j)),
                  pl.BlockSpec((d, tf), lambda i, j: (0, j)),
                  pl.BlockSpec((tf, d), lambda i, j: (j, 0))],
        out_specs=pl.BlockSpec((tm, d), lambda i, j: (i, 0), pipeline_mode=pl.Buffered(1)),
        scratch_shapes=[pltpu.VMEM((tm, d), BF16)],
        compiler_params=_params("arbitrary", "arbitrary"),
        name="dense_ffn",
    )(x, gain.reshape(1, d), wg, wu, wd)


def _unpack_bf16_pairs(words):
    hi = lax.bitcast_convert_type(words & jnp.uint32(0xFFFF0000), F32).astype(BF16)
    lo = lax.bitcast_convert_type(words << 16, F32).astype(BF16)
    return hi, lo


def _pack_bf16_pairs(hi, lo):
    return lax.bitcast_convert_type(hi, U32) | (lax.bitcast_convert_type(lo, U32) >> 16)


def _moe_ffn_kernel(te_ref, tr_ref, nu_ref, xs_ref, wg_ref, wu_ref, wd_ref, o_ref, a_ref):
    i, j = pl.program_id(0), pl.program_id(1)

    @pl.when(i < nu_ref[0])
    def _():
        @pl.when(j == 0)
        def _():
            half = a_ref.shape[1] // 2

            def body(c, carry):
                r = pl.multiple_of(c * NORM_ROWS, NORM_ROWS)
                hi, lo = _unpack_bf16_pairs(xs_ref[pl.ds(r, NORM_ROWS), :])
                a_ref[pl.ds(r, NORM_ROWS), :half] = hi
                a_ref[pl.ds(r, NORM_ROWS), half:] = lo
                return carry
            lax.fori_loop(0, a_ref.shape[0] // NORM_ROWS, body, 0)
            o_ref[...] = jnp.zeros_like(o_ref)

        wg, wu, wd = wg_ref[...].astype(BF16), wu_ref[...].astype(BF16), wd_ref[...].astype(BF16)
        sub = a_ref.shape[0] // MOE_SUBTILES
        rows_used = tr_ref[jnp.minimum(i, nu_ref[0] - 1)]
        for sb in range(MOE_SUBTILES):
            def sub_step(rows=slice(sb * sub, (sb + 1) * sub)):
                o_ref[rows, :] += _swiglu(a_ref[rows, :], wg, wu, wd)
            if sb == 0:
                sub_step()
            else:
                pl.when(rows_used > sb * sub)(sub_step)

    @pl.when((i >= nu_ref[0]) & (j == 0))
    def _():
        o_ref[...] = jnp.zeros_like(o_ref)


def moe_ffn(tile_expert, tile_rows, n_used, xs, wg, wu, wd):
    p, half = xs.shape
    d = 2 * half
    f = wg.shape[2]
    tm, tf = MOE_TM, min(MOE_TF, f)
    nf = f // tf

    def row_map(i, j, te, tr, nu):
        return (jnp.minimum(i, nu[0] - 1), 0)

    def f_idx(i, j, nu):
        return jnp.where(i < nu[0], j, nf - 1)

    def te_idx(i, te, nu):
        return te[jnp.minimum(i, nu[0] - 1)]

    return pl.pallas_call(
        _moe_ffn_kernel,
        out_shape=jax.ShapeDtypeStruct((p, d), F32),
        grid_spec=pltpu.PrefetchScalarGridSpec(
            num_scalar_prefetch=3,
            grid=(p // tm, nf),
            in_specs=[pl.BlockSpec((tm, half), row_map, pipeline_mode=pl.Buffered(1)),
                      pl.BlockSpec((None, d, tf), lambda i, j, te, tr, nu: (te_idx(i, te, nu), 0, f_idx(i, j, nu))),
                      pl.BlockSpec((None, d, tf), lambda i, j, te, tr, nu: (te_idx(i, te, nu), 0, f_idx(i, j, nu))),
                      pl.BlockSpec((None, tf, d), lambda i, j, te, tr, nu: (te_idx(i, te, nu), f_idx(i, j, nu), 0))],
            out_specs=pl.BlockSpec((tm, d), lambda i, j, te, tr, nu: (i, 0), pipeline_mode=pl.Buffered(1)),
            scratch_shapes=[pltpu.VMEM((tm, d), BF16)]),
        compiler_params=_params("arbitrary", "arbitrary"),
        name="moe_ffn",
    )(tile_expert, tile_rows, n_used, xs, wg, wu, wd)


def _gate_rows(xn, wt_ref, b_ref):
    xh = xn.astype(BF16)
    xl = (xn - xh.astype(F32)).astype(BF16)
    w = wt_ref[...]
    wh = w.astype(BF16)
    wl = (w - wh.astype(F32)).astype(BF16)
    pre = _dot_nt(wh, xh) + (_dot_nt(wh, xl) + _dot_nt(wl, xh)) + b_ref[...]
    g = GATE_SOFTCAP * jnp.tanh(pre / GATE_SOFTCAP)
    log_sig = jnp.minimum(g, 0.0) - jnp.log1p(jnp.exp(-jnp.abs(g)))
    row = lax.broadcasted_iota(jnp.int32, g.shape, 0)
    is_forget = ((row >= MLSTM_HEADS) & (row < 2 * MLSTM_HEADS)) | (row >= 3 * MLSTM_HEADS)
    return jnp.where(is_forget, log_sig, g)


def _mlstm_in_kernel(x_ref, g_ref, w_ref, wt_ref, b_ref, o_ref, gates_ref, a_ref):
    @pl.when(pl.program_id(1) == 0)
    def _():
        for c in range(x_ref.shape[0] // NORM_ROWS):
            rows = slice(c * NORM_ROWS, (c + 1) * NORM_ROWS)
            xn = _rms_rows(x_ref[rows, :], g_ref[...])
            a_ref[rows, :] = xn.astype(BF16)
            gates_ref[:, rows] = _gate_rows(xn, wt_ref, b_ref)
    o_ref[...] = _dot(a_ref[...], w_ref[...]).astype(o_ref.dtype)


def mlstm_in_proj(x, gain, w, n_cols, w_gates_t, b_gates):
    m, k = x.shape
    ng = w_gates_t.shape[0]
    tm, tn = min(MM_TM, m), min(MM_TN, n_cols)
    return pl.pallas_call(
        _mlstm_in_kernel,
        out_shape=(jax.ShapeDtypeStruct((m, n_cols), BF16), jax.ShapeDtypeStruct((ng, m), F32)),
        grid=(m // tm, n_cols // tn),
        in_specs=[pl.BlockSpec((tm, k), lambda i, j: (i, 0)),
                  pl.BlockSpec((1, k), lambda i, j: (0, 0)),
                  pl.BlockSpec((k, tn), lambda i, j: (0, j)),
                  pl.BlockSpec((ng, k), lambda i, j: (0, 0)),
                  pl.BlockSpec((ng, 1), lambda i, j: (0, 0))],
        out_specs=(pl.BlockSpec((tm, tn), lambda i, j: (i, j)),
                   pl.BlockSpec((ng, tm), lambda i, j: (0, i))),
        scratch_shapes=[pltpu.VMEM((tm, k), BF16)],
        compiler_params=_params("arbitrary", "arbitrary"),
        name="mlstm_in_proj",
    )(x, gain.reshape(1, k), w, w_gates_t, b_gates.reshape(ng, 1))


def _mlstm_chunk(q_ref, k_ref, v_ref, li_ref, lf_ref, h_ref, c_ref, n_ref, m_ref, slot, head, reverse):
    L = q_ref.shape[0]
    dk = c_ref.shape[1]
    dv = c_ref.shape[2]
    q = q_ref[:, head * dk:(head + 1) * dk] * jnp.asarray(dk ** -0.5, BF16)
    k = k_ref[:, head * dk:(head + 1) * dk]
    v = v_ref[:, head * dv:(head + 1) * dv]
    li = li_ref[head]
    lf = lf_ref[head]
    r_i = lax.broadcasted_iota(jnp.int32, (L, L), 0)
    c_i = lax.broadcasted_iota(jnp.int32, (L, L), 1)
    visible = (c_i >= r_i) if reverse else (c_i <= r_i)
    cum = (r_i >= c_i) if reverse else (r_i <= c_i)

    hi, mid, lo = (p.astype(F32) for p in _split3(lf))
    prow = lax.broadcasted_iota(jnp.int32, (BF16_SUBLANES, L), 0)
    pieces = jnp.where(prow == 0, hi, jnp.where(prow == 1, mid, jnp.where(prow == 2, lo, 0.0)))
    b = jnp.sum(_dot(pieces.astype(BF16), jnp.where(cum, 1.0, 0.0).astype(BF16)), axis=0, keepdims=True)
    g = jnp.sum(lf, axis=-1, keepdims=True)
    u = li - b

    m_prev = m_ref[slot][:, :1]
    a_mat = jnp.where(visible, jnp.broadcast_to(u, (L, L)), -jnp.inf)
    m_row = jnp.maximum(jnp.max(a_mat, axis=-1, keepdims=True), m_prev)
    w_intra = (jnp.exp(a_mat - m_row) * _dot_nt(q, k)).astype(BF16)
    w_inter = jnp.exp(m_prev - m_row)

    ones = jnp.ones((L, LANES), BF16)
    num = w_inter * _dot(q, c_ref[slot].astype(BF16)) + _dot(w_intra, v)
    den = w_inter * _dot(q, n_ref[slot].astype(BF16)) + _dot(w_intra, ones)
    b_col = jnp.sum(jnp.where(r_i == c_i, jnp.broadcast_to(b, (L, L)), 0.0), axis=-1, keepdims=True)
    inv = 1.0 / jnp.maximum(jnp.abs(den), jnp.exp(-(b_col + m_row)))
    h_ref[:, head * dv:(head + 1) * dv] = (num * jnp.concatenate([inv] * (dv // LANES), axis=1)).astype(h_ref.dtype)

    m_new = g + jnp.maximum(m_prev, jnp.max(u, axis=-1, keepdims=True))
    decay = jnp.exp(g + m_prev - m_new)
    ks_t = (k.astype(F32).T * jnp.exp(g + u - m_new)).astype(BF16)
    c_ref[slot] = decay * c_ref[slot] + _dot(ks_t, v)
    n_ref[slot] = decay * n_ref[slot] + _dot(ks_t, ones)
    m_ref[slot] = jnp.broadcast_to(m_new, (1, LANES))


def _mlstm_kernel(qf, kf, vf, lif, lff, qb, kb, vb, lib, lfb, hf_ref, hb_ref, c_ref, n_ref, m_ref):
    @pl.when(pl.program_id(1) == 0)
    def _():
        c_ref[...] = jnp.zeros_like(c_ref)
        n_ref[...] = jnp.zeros_like(n_ref)
        m_ref[...] = jnp.full_like(m_ref, -jnp.inf)
    heads = lif.shape[0]
    for h in range(heads):
        _mlstm_chunk(qf, kf, vf, lif, lff, hf_ref, c_ref, n_ref, m_ref, h, h, False)
        _mlstm_chunk(qb, kb, vb, lib, lfb, hb_ref, c_ref, n_ref, m_ref, heads + h, h, True)


def mlstm_scan(proj, gates, batch, seq, d_model):
    heads, dk = MLSTM_HEADS, MLSTM_DK
    dv = d_model // heads
    L = min(MLSTM_CHUNK, seq)
    nc = seq // L
    t = batch * seq
    qw = heads * dk

    def fwd(b, j):
        return b * nc + j

    def bwd(b, j):
        return b * nc + (nc - 1 - j)

    def qkv_specs(pos):
        return [pl.BlockSpec((L, qw), lambda b, j: (pos(b, j), 0)),
                pl.BlockSpec((L, qw), lambda b, j: (pos(b, j), 1)),
                pl.BlockSpec((L, d_model), lambda b, j: (pos(b, j), 2 * qw // d_model))]

    def gate_spec(kind, pos):
        return pl.BlockSpec((heads, 1, L), lambda b, j: (kind, 0, pos(b, j)))

    return pl.pallas_call(
        _mlstm_kernel,
        out_shape=(jax.ShapeDtypeStruct((t, d_model), BF16), jax.ShapeDtypeStruct((t, d_model), BF16)),
        grid=(batch, nc),
        in_specs=(qkv_specs(fwd) + [gate_spec(0, fwd), gate_spec(1, fwd)]
                  + qkv_specs(bwd) + [gate_spec(2, bwd), gate_spec(3, bwd)]),
        out_specs=(pl.BlockSpec((L, d_model), lambda b, j: (fwd(b, j), 0)),
                   pl.BlockSpec((L, d_model), lambda b, j: (bwd(b, j), 0))),
        scratch_shapes=[pltpu.VMEM((2 * heads, dk, dv), F32), pltpu.VMEM((2 * heads, dk, LANES), F32),
                        pltpu.VMEM((2 * heads, 1, LANES), F32)],
        compiler_params=_params("arbitrary", "arbitrary"),
        name="mlstm_scan",
    )(proj, proj, proj, gates, gates, proj, proj, proj, gates, gates)


def _na_fused_kernel(q_ref, k0_ref, k1_ref, k2_ref, v0_ref, v1_ref, v2_ref, bias_ref, wo_ref, x_ref, o_ref):
    hd = NA_HEAD_DIM
    gw = NA_HEADS_PER_GROUP * hd
    for g in range(q_ref.shape[1] // gw):
        outs = []
        for h in range(g * NA_HEADS_PER_GROUP, (g + 1) * NA_HEADS_PER_GROUP):
            cols = slice(h * hd, (h + 1) * hd)
            k = jnp.concatenate([k0_ref[:, cols], k1_ref[:, cols], k2_ref[:, cols]], axis=0)
            v = jnp.concatenate([v0_ref[:, cols], v1_ref[:, cols], v2_ref[:, cols]], axis=0)
            s = _dot_nt(q_ref[:, cols], k) + bias_ref[0, h].astype(F32)
            p = jnp.exp(s - jnp.max(s, axis=-1, keepdims=True))
            o = _dot(p.astype(BF16), v) * (1.0 / jnp.sum(p, axis=-1, keepdims=True))
            outs.append(o.astype(BF16))
        part = _dot(jnp.concatenate(outs, axis=1), wo_ref[g * gw:(g + 1) * gw, :])
        if g == 0:
            o_ref[...] = x_ref[...] + part
        else:
            o_ref[...] += part


def _na_bias_tables(rpb, rows):
    rb, w = NA_ROWS_PER_BLOCK, GRID_W
    nb = rows // rb
    heads, n_dr, n_dc = rpb.shape
    period = 2 * w
    left = (w - 1) - (NA_KW - 1)
    vec = jnp.pad(rpb.astype(BF16), ((0, 0), (0, 0), (left, period - left - n_dc)))
    toep = jnp.tile(vec, (1, 1, w))[:, :, :w * (period - 1)].reshape(heads, n_dr, w, period - 1)[..., w - 1:]
    zero = jnp.zeros((heads, w, w), BF16)
    tables = []
    for blk in (0, 1, nb - 1):
        ws = min(max(blk - 1, 0), nb - 3) * rb
        r = blk * rb + np.arange(rb)[:, None, None, None]
        c = np.arange(w)[None, :, None, None]
        a = ws + np.arange(3 * rb)[None, None, :, None]
        kc = np.arange(w)[None, None, None, :]
        rs = np.clip(r - NA_KH // 2, 0, rows - NA_KH)
        cs = np.clip(c - NA_KW // 2, 0, w - NA_KW)
        ok = (a >= rs) & (a < rs + NA_KH) & (kc >= cs) & (kc < cs + NA_KW)
        ok = np.broadcast_to(ok, (rb, w, 3 * rb, w)).reshape(rb * w, 3 * rb * w)
        parts = []
        for qr in range(rb):
            drs = [ws + ar - (blk * rb + qr) + NA_KH - 1 for ar in range(3 * rb)]
            parts.append(jnp.concatenate([toep[:, dr] if 0 <= dr < n_dr else zero for dr in drs], axis=-1))
        vals = jnp.stack(parts, axis=1).reshape(heads, rb * w, 3 * rb * w)
        tables.append(jnp.where(ok[None], vals, NEG_BIG))
    return jnp.stack(tables)


def na_attention_out(qkv, rpb, w_out, x, batch, seq):
    rb, w = NA_ROWS_PER_BLOCK, GRID_W
    t, d = x.shape
    rows = seq // w
    nb = rows // rb
    tq = rb * w
    bias = _na_bias_tables(rpb, rows)
    once = pl.Buffered(1)

    def kv_spec(part, off):
        return pl.BlockSpec((tq, d), lambda b, i: (b * nb + jnp.clip(i - 1, 0, nb - 3) + off, part))

    def bias_type(i):
        return jnp.where(i == 0, 0, jnp.where(i == nb - 1, 2, 1))

    return pl.pallas_call(
        _na_fused_kernel,
        out_shape=jax.ShapeDtypeStruct((t, d), F32),
        grid=(batch, nb),
        in_specs=([pl.BlockSpec((tq, d), lambda b, i: (b * nb + i, 0))]
                  + [kv_spec(1, off) for off in range(3)] + [kv_spec(2, off) for off in range(3)]
                  + [pl.BlockSpec((1, NA_HEADS, tq, 3 * tq), lambda b, i: (bias_type(i), 0, 0, 0), pipeline_mode=once),
                     pl.BlockSpec((d, d), lambda b, i: (0, 0), pipeline_mode=once),
                     pl.BlockSpec((tq, d), lambda b, i: (b * nb + i, 0))]),
        out_specs=pl.BlockSpec((tq, d), lambda b, i: (b * nb + i, 0)),
        compiler_params=_params("arbitrary", "arbitrary"),
        name="na_attention_out",
    )(qkv, qkv, qkv, qkv, qkv, qkv, qkv, bias, w_out, x)


def _router_kernel(x_ref, g_ref, wr_ref, xp_ref, meta_ref, cnt_ref, carry_ref):
    @pl.when(pl.program_id(0) == 0)
    def _():
        carry_ref[...] = jnp.zeros_like(carry_ref)

    tt = x_ref.shape[0]
    xn = _rms_rows(x_ref[...], g_ref[...])
    xh = xn.astype(BF16)
    xhf = xh.astype(F32)
    xl = (xn - xhf).astype(BF16)
    wr = wr_ref[...]
    wh = wr.astype(BF16)
    wl = (wr - wh.astype(F32)).astype(BF16)
    logits = _dot(xh, wh) + (_dot(xh, wl) + _dot(xl, wh))
    col = lax.broadcasted_iota(jnp.int32, logits.shape, 1).astype(F32)
    lg = jnp.where(col < N_EXPERTS, logits, -jnp.inf)
    m1 = jnp.max(lg, axis=-1, keepdims=True)
    i1 = jnp.min(jnp.where(lg == m1, col, float(LANES)), axis=-1, keepdims=True)
    lg2 = jnp.where(col == i1, -jnp.inf, lg)
    m2 = jnp.max(lg2, axis=-1, keepdims=True)
    i2 = jnp.min(jnp.where(lg2 == m2, col, float(LANES)), axis=-1, keepdims=True)
    e2 = jnp.exp(m2 - m1)
    w1 = 1.0 / (1.0 + e2)
    w2 = e2 * w1

    hit1, hit2 = col == i1, col == i2
    cnt = jnp.where(hit1 | hit2, 1.0, 0.0)
    r_i = lax.broadcasted_iota(jnp.int32, (tt, tt), 0)
    c_i = lax.broadcasted_iota(jnp.int32, (tt, tt), 1)
    before = jnp.where(c_i < r_i, 1.0, 0.0).astype(BF16)
    prefix = _dot(before, cnt.astype(BF16)) + carry_ref[...]
    rank1 = jnp.sum(jnp.where(hit1, prefix, 0.0), axis=-1, keepdims=True)
    rank2 = jnp.sum(jnp.where(hit2, prefix, 0.0), axis=-1, keepdims=True)
    carry_ref[...] += jnp.sum(cnt, axis=0, keepdims=True)
    cnt_ref[...] = carry_ref[...]

    meta = jnp.zeros(logits.shape, F32)
    for lane, val in enumerate((i1, i2, w1, w2, rank1, rank2)):
        meta = jnp.where(col == lane, val, meta)
    meta_ref[...] = meta
    half = xhf.shape[1] // 2
    xp_ref[...] = _pack_bf16_pairs(xhf[:, :half], xhf[:, half:])


def moe_router(x, gain, w_router):
    t, d = x.shape
    tt = min(ROUTER_TT, t)
    wr = jnp.zeros((d, LANES), F32).at[:, :N_EXPERTS].set(w_router)
    return pl.pallas_call(
        _router_kernel,
        out_shape=(jax.ShapeDtypeStruct((t, d // 2), U32), jax.ShapeDtypeStruct((t, LANES), F32),
                   jax.ShapeDtypeStruct((1, LANES), F32)),
        grid=(t // tt,),
        in_specs=[pl.BlockSpec((tt, d), lambda i: (i, 0)),
                  pl.BlockSpec((1, d), lambda i: (0, 0)),
                  pl.BlockSpec((d, LANES), lambda i: (0, 0))],
        out_specs=(pl.BlockSpec((tt, d // 2), lambda i: (i, 0)),
                   pl.BlockSpec((tt, LANES), lambda i: (i, 0)),
                   pl.BlockSpec((1, LANES), lambda i: (0, 0))),
        scratch_shapes=[pltpu.VMEM((1, LANES), F32)],
        compiler_params=_params("arbitrary"),
        name="moe_router",
    )(x, gain.reshape(1, d), wr)


def _scatter_kernel(pad_lo_ref, pad_hi_ref, tail_ref, p0_ref, p1_ref, src_ref, dst_ref, sem):
    tt = p0_ref.shape[2]

    def row_copy(src_row, dst_row):
        return pltpu.make_async_copy(src_ref.at[pl.ds(src_row, 1)], dst_ref.at[pl.ds(dst_row, 1)], sem)

    def block_copy(dst_block):
        return pltpu.make_async_copy(src_ref, dst_ref.at[pl.ds(pl.multiple_of(dst_block * tt, tt), tt)], sem)

    @pl.when(pl.program_id(0) == 0)
    def _():
        for e in range(N_EXPERTS):
            def pad_body(r, carry):
                row_copy(0, r).start()
                return carry
            lax.fori_loop(pad_lo_ref[e], pad_hi_ref[e], pad_body, 0)

            def pad_wait(r, carry):
                row_copy(0, 0).wait()
                return carry
            lax.fori_loop(pad_lo_ref[e], pad_hi_ref[e], pad_wait, 0)

        def tail_body(blk, carry):
            block_copy(blk).start()
            block_copy(blk).wait()
            return carry
        lax.fori_loop(tail_ref[0], tail_ref[1], tail_body, 0)

    def start_body(t, carry):
        row_copy(t, p0_ref[0, 0, t]).start()
        row_copy(t, p1_ref[0, 0, t]).start()
        return carry
    lax.fori_loop(0, tt, start_body, 0, unroll=8)
    block_copy(0).wait()
    block_copy(0).wait()


def moe_scatter_rows(xp, pos0, pos1, pad_lo, pad_hi, tail_blocks, n_rows):
    t, half = xp.shape
    tt = min(SCATTER_TT, t)
    nblk = t // tt
    smem_rows = pl.BlockSpec((1, 1, tt), lambda i, lo, hi, tl: (i, 0, 0), memory_space=pltpu.SMEM)
    return pl.pallas_call(
        _scatter_kernel,
        out_shape=jax.ShapeDtypeStruct((n_rows, half), xp.dtype),
        grid_spec=pltpu.PrefetchScalarGridSpec(
            num_scalar_prefetch=3,
            grid=(nblk,),
            in_specs=[smem_rows, smem_rows, pl.BlockSpec((tt, half), lambda i, lo, hi, tl: (i, 0))],
            out_specs=pl.BlockSpec(memory_space=pl.ANY),
            scratch_shapes=[pltpu.SemaphoreType.DMA]),
        compiler_params=pltpu.CompilerParams(dimension_semantics=("arbitrary",), has_side_effects=True),
        name="moe_scatter_rows",
    )(pad_lo, pad_hi, tail_blocks, pos0.reshape(nblk, 1, tt), pos1.reshape(nblk, 1, tt), xp)


def _combine_kernel(p0_ref, p1_ref, x_ref, meta_ref, y_ref, o_ref, buf_ref, sem):
    tt = x_ref.shape[0]

    def row_copy(k, src_row, t):
        return pltpu.make_async_copy(y_ref.at[pl.ds(src_row, 1)], buf_ref.at[k, pl.ds(t, 1)], sem)

    def start_body(t, carry):
        row_copy(0, p0_ref[0, 0, t], t).start(priority=0)
        row_copy(1, p1_ref[0, 0, t], t).start(priority=1)
        return carry
    lax.fori_loop(0, tt, start_body, 0, unroll=8)
    for k in range(TOP_K):
        pltpu.make_async_copy(y_ref.at[pl.ds(0, tt)], buf_ref.at[k], sem).wait()

    meta = meta_ref[...]
    o_ref[...] = x_ref[...] + (meta[:, 2:3] * buf_ref[0] + meta[:, 3:4] * buf_ref[1])


def moe_combine(x, meta, pos0, pos1, y):
    t, d = x.shape
    tt = min(COMBINE_TT, t)
    nblk = t // tt
    smem_rows = pl.BlockSpec((1, 1, tt), lambda i: (i, 0, 0), memory_space=pltpu.SMEM)
    return pl.pallas_call(
        _combine_kernel,
        out_shape=jax.ShapeDtypeStruct((t, d), F32),
        grid=(nblk,),
        in_specs=[smem_rows, smem_rows,
                  pl.BlockSpec((tt, d), lambda i: (i, 0)),
                  pl.BlockSpec((tt, LANES), lambda i: (i, 0)),
                  pl.BlockSpec(memory_space=pl.ANY)],
        out_specs=pl.BlockSpec((tt, d), lambda i: (i, 0)),
        scratch_shapes=[pltpu.VMEM((2, tt, d), F32), pltpu.SemaphoreType.DMA],
        compiler_params=_params("arbitrary"),
        name="moe_combine",
    )(pos0.reshape(nblk, 1, tt), pos1.reshape(nblk, 1, tt), x, meta, y)


def mlstm_layer(x, batch, seq, mix_norm, w_in, b_gates, head_norm, w_out):
    t, d = x.shape
    n_main = w_in.shape[1] - 4 * MLSTM_HEADS
    proj, gates = mlstm_in_proj(x, mix_norm, w_in.astype(BF16), n_main, w_in[:, n_main:].T, b_gates)
    hf, hb = mlstm_scan(proj, gates.reshape(4 * MLSTM_HEADS, 1, t), batch, seq, d)
    return mlstm_out_proj(hf, hb, proj, head_norm, w_out.astype(BF16), x)


def na_layer(x, batch, seq, mix_norm, w_qkv, q_norm, k_norm, rpb, w_out):
    qkv = qkv_matmul(x, mix_norm, w_qkv.astype(BF16), q_norm, k_norm)
    return na_attention_out(qkv, rpb, w_out.astype(BF16), x, batch, seq)


def moe_layer(x, ffn_norm, w_router, w_gate, w_up, w_down):
    t, d = x.shape
    tm = MOE_TM
    xp, meta, counts = moe_router(x, ffn_norm, w_router)
    counts = counts[0, :N_EXPERTS].astype(jnp.int32)
    tiles = (counts + tm - 1) // tm
    tile_end = jnp.cumsum(tiles)
    row_start = (tile_end - tiles) * tm
    n_tiles = TOP_K * t // tm + N_EXPERTS
    tile_expert = jnp.minimum(jnp.sum(jnp.arange(n_tiles)[:, None] >= tile_end[None, :], axis=1),
                              N_EXPERTS - 1).astype(jnp.int32)
    n_used = tile_end[-1:].astype(jnp.int32)
    first_tile = (tile_end - tiles)[tile_expert]
    tile_rows = jnp.clip(counts[tile_expert] - (jnp.arange(n_tiles) - first_tile) * tm, 0, tm).astype(jnp.int32)
    e0, e1 = meta[:, 0].astype(jnp.int32), meta[:, 1].astype(jnp.int32)
    pos0 = row_start[e0] + meta[:, 4].astype(jnp.int32)
    pos1 = row_start[e1] + meta[:, 5].astype(jnp.int32)
    tt = min(SCATTER_TT, t)
    tail_blocks = jnp.stack([tile_end[-1] * tm // tt, n_tiles * tm // tt]).astype(jnp.int32)
    xs = moe_scatter_rows(xp, pos0, pos1, (row_start + counts).astype(jnp.int32),
                          (tile_end * tm).astype(jnp.int32), tail_blocks, n_tiles * tm)
    y = moe_ffn(tile_expert, tile_rows, n_used, xs, w_gate, w_up, w_down)
    return moe_combine(x, meta, pos0, pos1, y)


def kernel(x, l0_mix_norm, l0_mlstm_w_in, l0_mlstm_b_gates, l0_mlstm_head_norm, l0_mlstm_w_out, l0_ffn_norm, l0_ffn_w_gate, l0_ffn_w_up, l0_ffn_w_down, l1_mix_norm, l1_na_w_qkv, l1_na_q_norm, l1_na_k_norm, l1_na_rpb, l1_na_w_out, l1_ffn_norm, l1_moe_w_router, l1_moe_w_gate, l1_moe_w_up, l1_moe_w_down, l2_mix_norm, l2_mlstm_w_in, l2_mlstm_b_gates, l2_mlstm_head_norm, l2_mlstm_w_out, l2_ffn_norm, l2_ffn_w_gate, l2_ffn_w_up, l2_ffn_w_down, l3_mix_norm, l3_na_w_qkv, l3_na_q_norm, l3_na_k_norm, l3_na_rpb, l3_na_w_out, l3_ffn_norm, l3_moe_w_router, l3_moe_w_gate, l3_moe_w_up, l3_moe_w_down):
    batch, seq, d = x.shape
    h = x.reshape(batch * seq, d)
    h = mlstm_layer(h, batch, seq, l0_mix_norm, l0_mlstm_w_in, l0_mlstm_b_gates, l0_mlstm_head_norm, l0_mlstm_w_out)
    h = dense_ffn(h, l0_ffn_norm, l0_ffn_w_gate, l0_ffn_w_up, l0_ffn_w_down)
    h = na_layer(h, batch, seq, l1_mix_norm, l1_na_w_qkv, l1_na_q_norm, l1_na_k_norm, l1_na_rpb, l1_na_w_out)
    h = moe_layer(h, l1_ffn_norm, l1_moe_w_router, l1_moe_w_gate, l1_moe_w_up, l1_moe_w_down)
    h = mlstm_layer(h, batch, seq, l2_mix_norm, l2_mlstm_w_in, l2_mlstm_b_gates, l2_mlstm_head_norm, l2_mlstm_w_out)
    h = dense_ffn(h, l2_ffn_norm, l2_ffn_w_gate, l2_ffn_w_up, l2_ffn_w_down)
    h = na_layer(h, batch, seq, l3_mix_norm, l3_na_w_qkv, l3_na_q_norm, l3_na_k_norm, l3_na_rpb, l3_na_w_out)
    h = moe_layer(h, l3_ffn_norm, l3_moe_w_router, l3_moe_w_gate, l3_moe_w_up, l3_moe_w_down)
    return h.reshape(batch, seq, d)
```

```python
import functools

import jax
import jax.numpy as jnp
import numpy as np
from jax import lax
from jax.experimental import pallas as pl
from jax.experimental.pallas import tpu as pltpu

F32 = jnp.float32
BF16 = jnp.bfloat16
U32 = jnp.uint32

NORM_EPS = 1e-6
LANES = 128
BF16_SUBLANES = 16
VMEM_LIMIT_BYTES = 56 << 20
NEG_BIG = -1e30

MLSTM_HEADS = 4
MLSTM_DK = 256
GATE_SOFTCAP = 15.0
MLSTM_CHUNK = 256
GRID_W = 64
NA_HEADS = 16
NA_HEAD_DIM = 128
NA_KH = 8
NA_KW = 16
NA_ROWS_PER_BLOCK = 4
NA_HEADS_PER_GROUP = 4
N_EXPERTS = 8
TOP_K = 2

MM_TM, MM_TN = 1024, 2048
MLSTM_OUT_TM = 512
FFN_TM, FFN_TF = 1024, 512
MOE_TM, MOE_TF = 1024, 512
MOE_SUBTILES = 2
ROUTER_TT = 512
SCATTER_TT = 1024
COMBINE_TT = 512
NORM_ROWS = 128


def _params(*semantics):
    return pltpu.CompilerParams(dimension_semantics=semantics, vmem_limit_bytes=VMEM_LIMIT_BYTES)


def _dot(a, b):
    return jnp.dot(a, b, preferred_element_type=F32)


def _dot_nt(a, b):
    return lax.dot_general(a, b, (((1,), (1,)), ((), ())), preferred_element_type=F32)


def _split3(x):
    hi = x.astype(BF16)
    r1 = x - hi.astype(F32)
    mid = r1.astype(BF16)
    lo = (r1 - mid.astype(F32)).astype(BF16)
    return hi, mid, lo


def _rms_rows(x, gain):
    ms = jnp.mean(x * x, axis=-1, keepdims=True)
    return x * lax.rsqrt(ms + NORM_EPS) * gain


def _fill_rmsnorm(x_ref, g_ref, a_ref):
    def body(c, carry):
        r = pl.multiple_of(c * NORM_ROWS, NORM_ROWS)
        a_ref[pl.ds(r, NORM_ROWS), :] = _rms_rows(x_ref[pl.ds(r, NORM_ROWS), :], g_ref[...]).astype(BF16)
        return carry
    lax.fori_loop(0, x_ref.shape[0] // NORM_ROWS, body, 0)


def _qkv_matmul_kernel(x_ref, g_ref, w_ref, hg_ref, o_ref, a_ref, *, n_norm_tiles):
    j = pl.program_id(1)

    @pl.when(j == 0)
    def _():
        _fill_rmsnorm(x_ref, g_ref, a_ref)
    acc = _dot(a_ref[...], w_ref[...])

    @pl.when(j < n_norm_tiles)
    def _():
        hd = hg_ref.shape[2]
        mean_mat = jnp.full((hd, hd), 1.0 / hd, BF16)
        for g in range(o_ref.shape[1] // hd):
            cols = slice(g * hd, (g + 1) * hd)
            blk = acc[:, cols]
            ms = _dot((blk * blk).astype(BF16), mean_mat)
            o_ref[:, cols] = (blk * lax.rsqrt(ms + NORM_EPS) * hg_ref[0]).astype(o_ref.dtype)

    @pl.when(j >= n_norm_tiles)
    def _():
        o_ref[...] = acc.astype(o_ref.dtype)


def qkv_matmul(x, gain, w, q_norm, k_norm):
    m, k = x.shape
    n = w.shape[1]
    d = n // 3
    hd = NA_HEAD_DIM
    tm, tn = min(MM_TM, m), min(MM_TN, d)
    per = d // tn
    head_gain = jnp.concatenate([jnp.tile((q_norm * hd ** -0.5)[None], (per, 1)), jnp.tile(k_norm[None], (per, 1)),
                                 jnp.ones((per, hd), F32)]).reshape(3 * per, 1, hd)
    return pl.pallas_call(
        functools.partial(_qkv_matmul_kernel, n_norm_tiles=2 * per),
        out_shape=jax.ShapeDtypeStruct((m, n), BF16),
        grid=(m // tm, n // tn),
        in_specs=[pl.BlockSpec((tm, k), lambda i, j: (i, 0)),
                  pl.BlockSpec((1, k), lambda i, j: (0, 0)),
                  pl.BlockSpec((k, tn), lambda i, j: (0, j)),
                  pl.BlockSpec((1, 1, hd), lambda i, j: (j, 0, 0))],
        out_specs=pl.BlockSpec((tm, tn), lambda i, j: (i, j)),
        scratch_shapes=[pltpu.VMEM((tm, k), BF16)],
        compiler_params=_params("arbitrary", "arbitrary"),
        name="qkv_matmul",
    )(x, gain.reshape(1, k), w, head_gain)


def _mlstm_out_kernel(hf_ref, hb_ref, og_ref, g_ref, w_ref, r_ref, o_ref, a_ref, *, heads):
    @pl.when(pl.program_id(1) == 0)
    def _():
        dv = a_ref.shape[1] // heads

        def body(c, carry):
            r = pl.multiple_of(c * NORM_ROWS, NORM_ROWS)
            rows = pl.ds(r, NORM_ROWS)
            for h in range(heads):
                cols = slice(h * dv, (h + 1) * dv)
                hh = hf_ref[rows, cols].astype(F32) + hb_ref[rows, cols].astype(F32)
                y = _rms_rows(hh, g_ref[:, cols]) * jax.nn.sigmoid(og_ref[rows, cols].astype(F32))
                a_ref[rows, cols] = y.astype(BF16)
            return carry
        lax.fori_loop(0, a_ref.shape[0] // NORM_ROWS, body, 0)
    o_ref[...] = r_ref[...] + _dot(a_ref[...], w_ref[...])


def mlstm_out_proj(hf, hb, proj, head_norm, w, res):
    m, d = hf.shape
    n = w.shape[1]
    tm, tn = min(MLSTM_OUT_TM, m), n
    o_col_block = (proj.shape[1] - d) // d
    return pl.pallas_call(
        functools.partial(_mlstm_out_kernel, heads=MLSTM_HEADS),
        out_shape=jax.ShapeDtypeStruct((m, n), F32),
        grid=(m // tm, n // tn),
        in_specs=[pl.BlockSpec((tm, d), lambda i, j: (i, 0)),
                  pl.BlockSpec((tm, d), lambda i, j: (i, 0)),
                  pl.BlockSpec((tm, d), lambda i, j: (i, o_col_block)),
                  pl.BlockSpec((1, d), lambda i, j: (0, 0)),
                  pl.BlockSpec((d, tn), lambda i, j: (0, j), pipeline_mode=pl.Buffered(1)),
                  pl.BlockSpec((tm, tn), lambda i, j: (i, j))],
        out_specs=pl.BlockSpec((tm, tn), lambda i, j: (i, j)),
        scratch_shapes=[pltpu.VMEM((tm, d), BF16)],
        compiler_params=_params("arbitrary", "arbitrary"),
        name="mlstm_out_proj",
    )(hf, hb, proj, head_norm.reshape(1, d), w, res)


def _swiglu(a, wg, wu, wd):
    gate = _dot(a, wg)
    up = _dot(a, wu)
    hidden = (gate * jax.nn.sigmoid(gate) * up).astype(BF16)
    return _dot(hidden, wd)


def _ffn_kernel(x_ref, g_ref, wg_ref, wu_ref, wd_ref, o_ref, a_ref):
    @pl.when(pl.program_id(1) == 0)
    def _():
        _fill_rmsnorm(x_ref, g_ref, a_ref)
        o_ref[...] = x_ref[...]
    o_ref[...] += _swiglu(a_ref[...], wg_ref[...].astype(BF16), wu_ref[...].astype(BF16), wd_ref[...].astype(BF16))


def dense_ffn(x, gain, wg, wu, wd):
    m, d = x.shape
    f = wg.shape[1]
    tm, tf = min(FFN_TM, m), min(FFN_TF, f)
    return pl.pallas_call(
        _ffn_kernel,
        out_shape=jax.ShapeDtypeStruct((m, d), F32),
        grid=(m // tm, f // tf),
        in_specs=[pl.BlockSpec((tm, d), lambda i, j: (i, 0), pipeline_mode=pl.Buffered(1)),
                  pl.BlockSpec((1, d), lambda i, j: (0, 0)),
                  pl.BlockSpec((d, tf), lambda i, j: (0, j)),
                  pl.BlockSpec((d, tf), lambda i, j: (0, j)),
                  pl.BlockSpec((tf, d), lambda i, j: (j, 0))],
        out_specs=pl.BlockSpec((tm, d), lambda i, j: (i, 0), pipeline_mode=pl.Buffered(1)),
        scratch_shapes=[pltpu.VMEM((tm, d), BF16)],
        compiler_params=_params("arbitrary", "arbitrary"),
        name="dense_ffn",
    )(x, gain.reshape(1, d), wg, wu, wd)


def _unpack_bf16_pairs(words):
    hi = lax.bitcast_convert_type(words & jnp.uint32(0xFFFF0000), F32).astype(BF16)
    lo = lax.bitcast_convert_type(words << 16, F32).astype(BF16)
    return hi, lo


def _pack_bf16_pairs(hi, lo):
    return lax.bitcast_convert_type(hi, U32) | (lax.bitcast_convert_type(lo, U32) >> 16)


def _moe_ffn_kernel(te_ref, tr_ref, nu_ref, xs_ref, wg_ref, wu_ref, wd_ref, o_ref, a_ref):
    i, j = pl.program_id(0), pl.program_id(1)

    @pl.when(i < nu_ref[0])
    def _():
        @pl.when(j == 0)
        def _():
            half = a_ref.shape[1] // 2

            def body(c, carry):
                r = pl.multiple_of(c * NORM_ROWS, NORM_ROWS)
                hi, lo = _unpack_bf16_pairs(xs_ref[pl.ds(r, NORM_ROWS), :])
                a_ref[pl.ds(r, NORM_ROWS), :half] = hi
                a_ref[pl.ds(r, NORM_ROWS), half:] = lo
                return carry
            lax.fori_loop(0, a_ref.shape[0] // NORM_ROWS, body, 0)
            o_ref[...] = jnp.zeros_like(o_ref)

        wg, wu, wd = wg_ref[...].astype(BF16), wu_ref[...].astype(BF16), wd_ref[...].astype(BF16)
        sub = a_ref.shape[0] // MOE_SUBTILES
        rows_used = tr_ref[jnp.minimum(i, nu_ref[0] - 1)]
        for sb in range(MOE_SUBTILES):
            def sub_step(rows=slice(sb * sub, (sb + 1) * sub)):
                o_ref[rows, :] += _swiglu(a_ref[rows, :], wg, wu, wd)
            if sb == 0:
                sub_step()
            else:
                pl.when(rows_used > sb * sub)(sub_step)

    @pl.when((i >= nu_ref[0]) & (j == 0))
    def _():
        o_ref[...] = jnp.zeros_like(o_ref)


def moe_ffn(tile_expert, tile_rows, n_used, xs, wg, wu, wd):
    p, half = xs.shape
    d = 2 * half
    f = wg.shape[2]
    tm, tf = MOE_TM, min(MOE_TF, f)
    nf = f // tf

    def row_map(i, j, te, tr, nu):
        return (jnp.minimum(i, nu[0] - 1), 0)

    def f_idx(i, j, nu):
        return jnp.where(i < nu[0], j, nf - 1)

    def te_idx(i, te, nu):
        return te[jnp.minimum(i, nu[0] - 1)]

    return pl.pallas_call(
        _moe_ffn_kernel,
        out_shape=jax.ShapeDtypeStruct((p, d), F32),
        grid_spec=pltpu.PrefetchScalarGridSpec(
            num_scalar_prefetch=3,
            grid=(p // tm, nf),
            in_specs=[pl.BlockSpec((tm, half), row_map, pipeline_mode=pl.Buffered(1)),
                      pl.BlockSpec((None, d, tf), lambda i, j, te, tr, nu: (te_idx(i, te, nu), 0, f_idx(i, j, nu))),
                      pl.BlockSpec((None, d, tf), lambda i, j, te, tr, nu: (te_idx(i, te, nu), 0, f_idx(i, j, nu))),
                      pl.BlockSpec((None, tf, d), lambda i, j, te, tr, nu: (te_idx(i, te, nu), f_idx(i, j, nu), 0))],
            out_specs=pl.BlockSpec((tm, d), lambda i, j, te, tr, nu: (i, 0), pipeline_mode=pl.Buffered(1)),
            scratch_shapes=[pltpu.VMEM((tm, d), BF16)]),
        compiler_params=_params("arbitrary", "arbitrary"),
        name="moe_ffn",
    )(tile_expert, tile_rows, n_used, xs, wg, wu, wd)


def _gate_rows(xn, wt_ref, b_ref):
    xh = xn.astype(BF16)
    xl = (xn - xh.astype(F32)).astype(BF16)
    w = wt_ref[...]
    wh = w.astype(BF16)
    wl = (w - wh.astype(F32)).astype(BF16)
    pre = _dot_nt(wh, xh) + (_dot_nt(wh, xl) + _dot_nt(wl, xh)) + b_ref[...]
    g = GATE_SOFTCAP * jnp.tanh(pre / GATE_SOFTCAP)
    log_sig = jnp.minimum(g, 0.0) - jnp.log1p(jnp.exp(-jnp.abs(g)))
    row = lax.broadcasted_iota(jnp.int32, g.shape, 0)
    is_forget = ((row >= MLSTM_HEADS) & (row < 2 * MLSTM_HEADS)) | (row >= 3 * MLSTM_HEADS)
    return jnp.where(is_forget, log_sig, g)


def _mlstm_in_kernel(x_ref, g_ref, w_ref, wt_ref, b_ref, o_ref, gates_ref, a_ref):
    @pl.when(pl.program_id(1) == 0)
    def _():
        for c in range(x_ref.shape[0] // NORM_ROWS):
            rows = slice(c * NORM_ROWS, (c + 1) * NORM_ROWS)
            xn = _rms_rows(x_ref[rows, :], g_ref[...])
            a_ref[rows, :] = xn.astype(BF16)
            gates_ref[:, rows] = _gate_rows(xn, wt_ref, b_ref)
    o_ref[...] = _dot(a_ref[...], w_ref[...]).astype(o_ref.dtype)


def mlstm_in_proj(x, gain, w, n_cols, w_gates_t, b_gates):
    m, k = x.shape
    ng = w_gates_t.shape[0]
    tm, tn = min(MM_TM, m), min(MM_TN, n_cols)
    return pl.pallas_call(
        _mlstm_in_kernel,
        out_shape=(jax.ShapeDtypeStruct((m, n_cols), BF16), jax.ShapeDtypeStruct((ng, m), F32)),
        grid=(m // tm, n_cols // tn),
        in_specs=[pl.BlockSpec((tm, k), lambda i, j: (i, 0)),
                  pl.BlockSpec((1, k), lambda i, j: (0, 0)),
                  pl.BlockSpec((k, tn), lambda i, j: (0, j)),
                  pl.BlockSpec((ng, k), lambda i, j: (0, 0)),
                  pl.BlockSpec((ng, 1), lambda i, j: (0, 0))],
        out_specs=(pl.BlockSpec((tm, tn), lambda i, j: (i, j)),
                   pl.BlockSpec((ng, tm), lambda i, j: (0, i))),
        scratch_shapes=[pltpu.VMEM((tm, k), BF16)],
        compiler_params=_params("arbitrary", "arbitrary"),
        name="mlstm_in_proj",
    )(x, gain.reshape(1, k), w, w_gates_t, b_gates.reshape(ng, 1))


def _mlstm_chunk(q_ref, k_ref, v_ref, li_ref, lf_ref, h_ref, c_ref, n_ref, m_ref, slot, head, reverse):
    L = q_ref.shape[0]
    dk = c_ref.shape[1]
    dv = c_ref.shape[2]
    q = q_ref[:, head * dk:(head + 1) * dk] * jnp.asarray(dk ** -0.5, BF16)
    k = k_ref[:, head * dk:(head + 1) * dk]
    v = v_ref[:, head * dv:(head + 1) * dv]
    li = li_ref[head]
    lf = lf_ref[head]
    r_i = lax.broadcasted_iota(jnp.int32, (L, L), 0)
    c_i = lax.broadcasted_iota(jnp.int32, (L, L), 1)
    visible = (c_i >= r_i) if reverse else (c_i <= r_i)
    cum = (r_i >= c_i) if reverse else (r_i <= c_i)

    hi, mid, lo = (p.astype(F32) for p in _split3(lf))
    prow = lax.broadcasted_iota(jnp.int32, (BF16_SUBLANES, L), 0)
    pieces = jnp.where(prow == 0, hi, jnp.where(prow == 1, mid, jnp.where(prow == 2, lo, 0.0)))
    b = jnp.sum(_dot(pieces.astype(BF16), jnp.where(cum, 1.0, 0.0).astype(BF16)), axis=0, keepdims=True)
    g = jnp.sum(lf, axis=-1, keepdims=True)
    u = li - b

    m_prev = m_ref[slot][:, :1]
    a_mat = jnp.where(visible, jnp.broadcast_to(u, (L, L)), -jnp.inf)
    m_row = jnp.maximum(jnp.max(a_mat, axis=-1, keepdims=True), m_prev)
    w_intra = (jnp.exp(a_mat - m_row) * _dot_nt(q, k)).astype(BF16)
    w_inter = jnp.exp(m_prev - m_row)

    ones = jnp.ones((L, LANES), BF16)
    num = w_inter * _dot(q, c_ref[slot].astype(BF16)) + _dot(w_intra, v)
    den = w_inter * _dot(q, n_ref[slot].astype(BF16)) + _dot(w_intra, ones)
    b_col = jnp.sum(jnp.where(r_i == c_i, jnp.broadcast_to(b, (L, L)), 0.0), axis=-1, keepdims=True)
    inv = 1.0 / jnp.maximum(jnp.abs(den), jnp.exp(-(b_col + m_row)))
    h_ref[:, head * dv:(head + 1) * dv] = (num * jnp.concatenate([inv] * (dv // LANES), axis=1)).astype(h_ref.dtype)

    m_new = g + jnp.maximum(m_prev, jnp.max(u, axis=-1, keepdims=True))
    decay = jnp.exp(g + m_prev - m_new)
    ks_t = (k.astype(F32).T * jnp.exp(g + u - m_new)).astype(BF16)
    c_ref[slot] = decay * c_ref[slot] + _dot(ks_t, v)
    n_ref[slot] = decay * n_ref[slot] + _dot(ks_t, ones)
    m_ref[slot] = jnp.broadcast_to(m_new, (1, LANES))


def _mlstm_kernel(qf, kf, vf, lif, lff, qb, kb, vb, lib, lfb, hf_ref, hb_ref, c_ref, n_ref, m_ref):
    @pl.when(pl.program_id(1) == 0)
    def _():
        c_ref[...] = jnp.zeros_like(c_ref)
        n_ref[...] = jnp.zeros_like(n_ref)
        m_ref[...] = jnp.full_like(m_ref, -jnp.inf)
    heads = lif.shape[0]
    for h in range(heads):
        _mlstm_chunk(qf, kf, vf, lif, lff, hf_ref, c_ref, n_ref, m_ref, h, h, False)
        _mlstm_chunk(qb, kb, vb, lib, lfb, hb_ref, c_ref, n_ref, m_ref, heads + h, h, True)


def mlstm_scan(proj, gates, batch, seq, d_model):
    heads, dk = MLSTM_HEADS, MLSTM_DK
    dv = d_model // heads
    L = min(MLSTM_CHUNK, seq)
    nc = seq // L
    t = batch * seq
    qw = heads * dk

    def fwd(b, j):
        return b * nc + j

    def bwd(b, j):
        return b * nc + (nc - 1 - j)

    def qkv_specs(pos):
        return [pl.BlockSpec((L, qw), lambda b, j: (pos(b, j), 0)),
                pl.BlockSpec((L, qw), lambda b, j: (pos(b, j), 1)),
                pl.BlockSpec((L, d_model), lambda b, j: (pos(b, j), 2 * qw // d_model))]

    def gate_spec(kind, pos):
        return pl.BlockSpec((heads, 1, L), lambda b, j: (kind, 0, pos(b, j)))

    return pl.pallas_call(
        _mlstm_kernel,
        out_shape=(jax.ShapeDtypeStruct((t, d_model), BF16), jax.ShapeDtypeStruct((t, d_model), BF16)),
        grid=(batch, nc),
        in_specs=(qkv_specs(fwd) + [gate_spec(0, fwd), gate_spec(1, fwd)]
                  + qkv_specs(bwd) + [gate_spec(2, bwd), gate_spec(3, bwd)]),
        out_specs=(pl.BlockSpec((L, d_model), lambda b, j: (fwd(b, j), 0)),
                   pl.BlockSpec((L, d_model), lambda b, j: (bwd(b, j), 0))),
        scratch_shapes=[pltpu.VMEM((2 * heads, dk, dv), F32), pltpu.VMEM((2 * heads, dk, LANES), F32),
                        pltpu.VMEM((2 * heads, 1, LANES), F32)],
        compiler_params=_params("arbitrary", "arbitrary"),
        name="mlstm_scan",
    )(proj, proj, proj, gates, gates, proj, proj, proj, gates, gates)


def _na_fused_kernel(q_ref, k0_ref, k1_ref, k2_ref, v0_ref, v1_ref, v2_ref, bias_ref, wo_ref, x_ref, o_ref):
    hd = NA_HEAD_DIM
    gw = NA_HEADS_PER_GROUP * hd
    for g in range(q_ref.shape[1] // gw):
        outs = []
        for h in range(g * NA_HEADS_PER_GROUP, (g + 1) * NA_HEADS_PER_GROUP):
            cols = slice(h * hd, (h + 1) * hd)
            k = jnp.concatenate([k0_ref[:, cols], k1_ref[:, cols], k2_ref[:, cols]], axis=0)
            v = jnp.concatenate([v0_ref[:, cols], v1_ref[:, cols], v2_ref[:, cols]], axis=0)
            s = _dot_nt(q_ref[:, cols], k) + bias_ref[0, h].astype(F32)
            p = jnp.exp(s - jnp.max(s, axis=-1, keepdims=True))
            o = _dot(p.astype(BF16), v) * (1.0 / jnp.sum(p, axis=-1, keepdims=True))
            outs.append(o.astype(BF16))
        part = _dot(jnp.concatenate(outs, axis=1), wo_ref[g * gw:(g + 1) * gw, :])
        if g == 0:
            o_ref[...] = x_ref[...] + part
        else:
            o_ref[...] += part


def _na_bias_tables(rpb, rows):
    rb, w = NA_ROWS_PER_BLOCK, GRID_W
    nb = rows // rb
    heads, n_dr, n_dc = rpb.shape
    period = 2 * w
    left = (w - 1) - (NA_KW - 1)
    vec = jnp.pad(rpb.astype(BF16), ((0, 0), (0, 0), (left, period - left - n_dc)))
    toep = jnp.tile(vec, (1, 1, w))[:, :, :w * (period - 1)].reshape(heads, n_dr, w, period - 1)[..., w - 1:]
    zero = jnp.zeros((heads, w, w), BF16)
    tables = []
    for blk in (0, 1, nb - 1):
        ws = min(max(blk - 1, 0), nb - 3) * rb
        r = blk * rb + np.arange(rb)[:, None, None, None]
        c = np.arange(w)[None, :, None, None]
        a = ws + np.arange(3 * rb)[None, None, :, None]
        kc = np.arange(w)[None, None, None, :]
        rs = np.clip(r - NA_KH // 2, 0, rows - NA_KH)
        cs = np.clip(c - NA_KW // 2, 0, w - NA_KW)
        ok = (a >= rs) & (a < rs + NA_KH) & (kc >= cs) & (kc < cs + NA_KW)
        ok = np.broadcast_to(ok, (rb, w, 3 * rb, w)).reshape(rb * w, 3 * rb * w)
        parts = []
        for qr in range(rb):
            drs = [ws + ar - (blk * rb + qr) + NA_KH - 1 for ar in range(3 * rb)]
            parts.append(jnp.concatenate([toep[:, dr] if 0 <= dr < n_dr else zero for dr in drs], axis=-1))
        vals = jnp.stack(parts, axis=1).reshape(heads, rb * w, 3 * rb * w)
        tables.append(jnp.where(ok[None], vals, NEG_BIG))
    return jnp.stack(tables)


def na_attention_out(qkv, rpb, w_out, x, batch, seq):
    rb, w = NA_ROWS_PER_BLOCK, GRID_W
    t, d = x.shape
    rows = seq // w
    nb = rows // rb
    tq = rb * w
    bias = _na_bias_tables(rpb, rows)
    once = pl.Buffered(1)

    def kv_spec(part, off):
        return pl.BlockSpec((tq, d), lambda b, i: (b * nb + jnp.clip(i - 1, 0, nb - 3) + off, part))

    def bias_type(i):
        return jnp.where(i == 0, 0, jnp.where(i == nb - 1, 2, 1))

    return pl.pallas_call(
        _na_fused_kernel,
        out_shape=jax.ShapeDtypeStruct((t, d), F32),
        grid=(batch, nb),
        in_specs=([pl.BlockSpec((tq, d), lambda b, i: (b * nb + i, 0))]
                  + [kv_spec(1, off) for off in range(3)] + [kv_spec(2, off) for off in range(3)]
                  + [pl.BlockSpec((1, NA_HEADS, tq, 3 * tq), lambda b, i: (bias_type(i), 0, 0, 0), pipeline_mode=once),
                     pl.BlockSpec((d, d), lambda b, i: (0, 0), pipeline_mode=once),
                     pl.BlockSpec((tq, d), lambda b, i: (b * nb + i, 0))]),
        out_specs=pl.BlockSpec((tq, d), lambda b, i: (b * nb + i, 0)),
        compiler_params=_params("arbitrary", "arbitrary"),
        name="na_attention_out",
    )(qkv, qkv, qkv, qkv, qkv, qkv, qkv, bias, w_out, x)


def _router_kernel(x_ref, g_ref, wr_ref, xp_ref, meta_ref, cnt_ref, carry_ref):
    @pl.when(pl.program_id(0) == 0)
    def _():
        carry_ref[...] = jnp.zeros_like(carry_ref)

    tt = x_ref.shape[0]
    xn = _rms_rows(x_ref[...], g_ref[...])
    xh = xn.astype(BF16)
    xhf = xh.astype(F32)
    xl = (xn - xhf).astype(BF16)
    wr = wr_ref[...]
    wh = wr.astype(BF16)
    wl = (wr - wh.astype(F32)).astype(BF16)
    logits = _dot(xh, wh) + (_dot(xh, wl) + _dot(xl, wh))
    col = lax.broadcasted_iota(jnp.int32, logits.shape, 1).astype(F32)
    lg = jnp.where(col < N_EXPERTS, logits, -jnp.inf)
    m1 = jnp.max(lg, axis=-1, keepdims=True)
    i1 = jnp.min(jnp.where(lg == m1, col, float(LANES)), axis=-1, keepdims=True)
    lg2 = jnp.where(col == i1, -jnp.inf, lg)
    m2 = jnp.max(lg2, axis=-1, keepdims=True)
    i2 = jnp.min(jnp.where(lg2 == m2, col, float(LANES)), axis=-1, keepdims=True)
    e2 = jnp.exp(m2 - m1)
    w1 = 1.0 / (1.0 + e2)
    w2 = e2 * w1

    hit1, hit2 = col == i1, col == i2
    cnt = jnp.where(hit1 | hit2, 1.0, 0.0)
    r_i = lax.broadcasted_iota(jnp.int32, (tt, tt), 0)
    c_i = lax.broadcasted_iota(jnp.int32, (tt, tt), 1)
    before = jnp.where(c_i < r_i, 1.0, 0.0).astype(BF16)
    prefix = _dot(before, cnt.astype(BF16)) + carry_ref[...]
    rank1 = jnp.sum(jnp.where(hit1, prefix, 0.0), axis=-1, keepdims=True)
    rank2 = jnp.sum(jnp.where(hit2, prefix, 0.0), axis=-1, keepdims=True)
    carry_ref[...] += jnp.sum(cnt, axis=0, keepdims=True)
    cnt_ref[...] = carry_ref[...]

    meta = jnp.zeros(logits.shape, F32)
    for lane, val in enumerate((i1, i2, w1, w2, rank1, rank2)):
        meta = jnp.where(col == lane, val, meta)
    meta_ref[...] = meta
    half = xhf.shape[1] // 2
    xp_ref[...] = _pack_bf16_pairs(xhf[:, :half], xhf[:, half:])


def moe_router(x, gain, w_router):
    t, d = x.shape
    tt = min(ROUTER_TT, t)
    wr = jnp.zeros((d, LANES), F32).at[:, :N_EXPERTS].set(w_router)
    return pl.pallas_call(
        _router_kernel,
        out_shape=(jax.ShapeDtypeStruct((t, d // 2), U32), jax.ShapeDtypeStruct((t, LANES), F32),
                   jax.ShapeDtypeStruct((1, LANES), F32)),
        grid=(t // tt,),
        in_specs=[pl.BlockSpec((tt, d), lambda i: (i, 0)),
                  pl.BlockSpec((1, d), lambda i: (0, 0)),
                  pl.BlockSpec((d, LANES), lambda i: (0, 0))],
        out_specs=(pl.BlockSpec((tt, d // 2), lambda i: (i, 0)),
                   pl.BlockSpec((tt, LANES), lambda i: (i, 0)),
                   pl.BlockSpec((1, LANES), lambda i: (0, 0))),
        scratch_shapes=[pltpu.VMEM((1, LANES), F32)],
        compiler_params=_params("arbitrary"),
        name="moe_router",
    )(x, gain.reshape(1, d), wr)


def _scatter_kernel(pad_lo_ref, pad_hi_ref, tail_ref, p0_ref, p1_ref, src_ref, dst_ref, sem):
    tt = p0_ref.shape[2]

    def row_copy(src_row, dst_row):
        return pltpu.make_async_copy(src_ref.at[pl.ds(src_row, 1)], dst_ref.at[pl.ds(dst_row, 1)], sem)

    def block_copy(dst_block):
        return pltpu.make_async_copy(src_ref, dst_ref.at[pl.ds(pl.multiple_of(dst_block * tt, tt), tt)], sem)

    @pl.when(pl.program_id(0) == 0)
    def _():
        for e in range(N_EXPERTS):
            def pad_body(r, carry):
                row_copy(0, r).start()
                return carry
            lax.fori_loop(pad_lo_ref[e], pad_hi_ref[e], pad_body, 0)

            def pad_wait(r, carry):
                row_copy(0, 0).wait()
                return carry
            lax.fori_loop(pad_lo_ref[e], pad_hi_ref[e], pad_wait, 0)

        def tail_body(blk, carry):
            block_copy(blk).start()
            block_copy(blk).wait()
            return carry
        lax.fori_loop(tail_ref[0], tail_ref[1], tail_body, 0)

    def start_body(t, carry):
        row_copy(t, p0_ref[0, 0, t]).start()
        row_copy(t, p1_ref[0, 0, t]).start()
        return carry
    lax.fori_loop(0, tt, start_body, 0, unroll=8)
    block_copy(0).wait()
    block_copy(0).wait()


def moe_scatter_rows(xp, pos0, pos1, pad_lo, pad_hi, tail_blocks, n_rows):
    t, half = xp.shape
    tt = min(SCATTER_TT, t)
    nblk = t // tt
    smem_rows = pl.BlockSpec((1, 1, tt), lambda i, lo, hi, tl: (i, 0, 0), memory_space=pltpu.SMEM)
    return pl.pallas_call(
        _scatter_kernel,
        out_shape=jax.ShapeDtypeStruct((n_rows, half), xp.dtype),
        grid_spec=pltpu.PrefetchScalarGridSpec(
            num_scalar_prefetch=3,
            grid=(nblk,),
            in_specs=[smem_rows, smem_rows, pl.BlockSpec((tt, half), lambda i, lo, hi, tl: (i, 0))],
            out_specs=pl.BlockSpec(memory_space=pl.ANY),
            scratch_shapes=[pltpu.SemaphoreType.DMA]),
        compiler_params=pltpu.CompilerParams(dimension_semantics=("arbitrary",), has_side_effects=True),
        name="moe_scatter_rows",
    )(pad_lo, pad_hi, tail_blocks, pos0.reshape(nblk, 1, tt), pos1.reshape(nblk, 1, tt), xp)


def _combine_kernel(p0_ref, p1_ref, x_ref, meta_ref, y_ref, o_ref, buf_ref, sem):
    tt = x_ref.shape[0]

    def row_copy(k, src_row, t):
        return pltpu.make_async_copy(y_ref.at[pl.ds(src_row, 1)], buf_ref.at[k, pl.ds(t, 1)], sem)

    def start_body(t, carry):
        row_copy(0, p0_ref[0, 0, t], t).start(priority=0)
        row_copy(1, p1_ref[0, 0, t], t).start(priority=1)
        return carry
    lax.fori_loop(0, tt, start_body, 0, unroll=8)
    for k in range(TOP_K):
        pltpu.make_async_copy(y_ref.at[pl.ds(0, tt)], buf_ref.at[k], sem).wait()

    meta = meta_ref[...]
    o_ref[...] = x_ref[...] + (meta[:, 2:3] * buf_ref[0] + meta[:, 3:4] * buf_ref[1])


def moe_combine(x, meta, pos0, pos1, y):
    t, d = x.shape
    tt = min(COMBINE_TT, t)
    nblk = t // tt
    smem_rows = pl.BlockSpec((1, 1, tt), lambda i: (i, 0, 0), memory_space=pltpu.SMEM)
    return pl.pallas_call(
        _combine_kernel,
        out_shape=jax.ShapeDtypeStruct((t, d), F32),
        grid=(nblk,),
        in_specs=[smem_rows, smem_rows,
                  pl.BlockSpec((tt, d), lambda i: (i, 0)),
                  pl.BlockSpec((tt, LANES), lambda i: (i, 0)),
                  pl.BlockSpec(memory_space=pl.ANY)],
        out_specs=pl.BlockSpec((tt, d), lambda i: (i, 0)),
        scratch_shapes=[pltpu.VMEM((2, tt, d), F32), pltpu.SemaphoreType.DMA],
        compiler_params=_params("arbitrary"),
        name="moe_combine",
    )(pos0.reshape(nblk, 1, tt), pos1.reshape(nblk, 1, tt), x, meta, y)


def mlstm_layer(x, batch, seq, mix_norm, w_in, b_gates, head_norm, w_out):
    t, d = x.shape
    n_main = w_in.shape[1] - 4 * MLSTM_HEADS
    proj, gates = mlstm_in_proj(x, mix_norm, w_in.astype(BF16), n_main, w_in[:, n_main:].T, b_gates)
    hf, hb = mlstm_scan(proj, gates.reshape(4 * MLSTM_HEADS, 1, t), batch, seq, d)
    return mlstm_out_proj(hf, hb, proj, head_norm, w_out.astype(BF16), x)


def na_layer(x, batch, seq, mix_norm, w_qkv, q_norm, k_norm, rpb, w_out):
    qkv = qkv_matmul(x, mix_norm, w_qkv.astype(BF16), q_norm, k_norm)
    return na_attention_out(qkv, rpb, w_out.astype(BF16), x, batch, seq)


def moe_layer(x, ffn_norm, w_router, w_gate, w_up, w_down):
    t, d = x.shape
    tm = MOE_TM
    xp, meta, counts = moe_router(x, ffn_norm, w_router)
    counts = counts[0, :N_EXPERTS].astype(jnp.int32)
    tiles = (counts + tm - 1) // tm
    tile_end = jnp.cumsum(tiles)
    row_start = (tile_end - tiles) * tm
    n_tiles = TOP_K * t // tm + N_EXPERTS
    tile_expert = jnp.minimum(jnp.sum(jnp.arange(n_tiles)[:, None] >= tile_end[None, :], axis=1),
                              N_EXPERTS - 1).astype(jnp.int32)
    n_used = tile_end[-1:].astype(jnp.int32)
    first_tile = (tile_end - tiles)[tile_expert]
    tile_rows = jnp.clip(counts[tile_expert] - (jnp.arange(n_tiles) - first_tile) * tm, 0, tm).astype(jnp.int32)
    e0, e1 = meta[:, 0].astype(jnp.int32), meta[:, 1].astype(jnp.int32)
    pos0 = row_start[e0] + meta[:, 4].astype(jnp.int32)
    pos1 = row_start[e1] + meta[:, 5].astype(jnp.int32)
    tt = min(SCATTER_TT, t)
    tail_blocks = jnp.stack([tile_end[-1] * tm // tt, n_tiles * tm // tt]).astype(jnp.int32)
    xs = moe_scatter_rows(xp, pos0, pos1, (row_start + counts).astype(jnp.int32),
                          (tile_end * tm).astype(jnp.int32), tail_blocks, n_tiles * tm)
    y = moe_ffn(tile_expert, tile_rows, n_used, xs, w_gate, w_up, w_down)
    return moe_combine(x, meta, pos0, pos1, y)


def kernel(x, l0_mix_norm, l0_mlstm_w_in, l0_mlstm_b_gates, l0_mlstm_head_norm, l0_mlstm_w_out, l0_ffn_norm, l0_ffn_w_gate, l0_ffn_w_up, l0_ffn_w_down, l1_mix_norm, l1_na_w_qkv, l1_na_q_norm, l1_na_k_norm, l1_na_rpb, l1_na_w_out, l1_ffn_norm, l1_moe_w_router, l1_moe_w_gate, l1_moe_w_up, l1_moe_w_down, l2_mix_norm, l2_mlstm_w_in, l2_mlstm_b_gates, l2_mlstm_head_norm, l2_mlstm_w_out, l2_ffn_norm, l2_ffn_w_gate, l2_ffn_w_up, l2_ffn_w_down, l3_mix_norm, l3_na_w_qkv, l3_na_q_norm, l3_na_k_norm, l3_na_rpb, l3_na_w_out, l3_ffn_norm, l3_moe_w_router, l3_moe_w_gate, l3_moe_w_up, l3_moe_w_down):
    batch, seq, d = x.shape
    h = x.reshape(batch * seq, d)
    h = mlstm_layer(h, batch, seq, l0_mix_norm, l0_mlstm_w_in, l0_mlstm_b_gates, l0_mlstm_head_norm, l0_mlstm_w_out)
    h = dense_ffn(h, l0_ffn_norm, l0_ffn_w_gate, l0_ffn_w_up, l0_ffn_w_down)
    h = na_layer(h, batch, seq, l1_mix_norm, l1_na_w_qkv, l1_na_q_norm, l1_na_k_norm, l1_na_rpb, l1_na_w_out)
    h = moe_layer(h, l1_ffn_norm, l1_moe_w_router, l1_moe_w_gate, l1_moe_w_up, l1_moe_w_down)
    h = mlstm_layer(h, batch, seq, l2_mix_norm, l2_mlstm_w_in, l2_mlstm_b_gates, l2_mlstm_head_norm, l2_mlstm_w_out)
    h = dense_ffn(h, l2_ffn_norm, l2_ffn_w_gate, l2_ffn_w_up, l2_ffn_w_down)
    h = na_layer(h, batch, seq, l3_mix_norm, l3_na_w_qkv, l3_na_q_norm, l3_na_k_norm, l3_na_rpb, l3_na_w_out)
    h = moe_layer(h, l3_ffn_norm, l3_moe_w_router, l3_moe_w_gate, l3_moe_w_up, l3_moe_w_down)
    return h.reshape(batch, seq, d)
```
